```python
import math
import jax, jax.numpy as jnp
from jax import lax
import numpy as np

D_MODEL = 1024
BATCH = 16
SEQ = 2048
DEPTH = 2

HEAD_DIM = 64
ATTN_SCALE = HEAD_DIM ** -0.5
N_HEADS_A = 8
DIL_PATTERNS = ((128, 1), (512, 4), (2048, 16))
WIDTH_A = N_HEADS_A * HEAD_DIM
N_HEADS_B = 4
WIDTH_B = N_HEADS_B * 2 * HEAD_DIM
MIX_WIDTH = WIDTH_A + WIDTH_B
PROJ_WIDTH = 3 * MIX_WIDTH
N_REL_HEADS = N_HEADS_A + N_HEADS_B
N_BUCKETS = 32
MAX_DISTANCE = 128
Q_BLOCK = 128
D_FF = 2816
N_EXPERTS = 8
TOP_K = 2
D_FF_EXPERT = 3584
EXPERT_BLOCK = 512
N_DENSE = (DEPTH + 1) // 2
N_MOE = DEPTH // 2
EPS = 1e-6

kernel_name = "hybrid_dilated_diff_attn_moe_block"


def rms_norm(x, g):
    xf = x.astype(jnp.float32)
    y = xf * lax.rsqrt(jnp.mean(xf * xf, axis=-1, keepdims=True) + EPS)
    return (y * g.astype(jnp.float32)).astype(x.dtype)


def t5_bucket(dist):
    n = jnp.maximum(dist, 0)
    max_exact = N_BUCKETS // 2
    large = max_exact + (jnp.log(jnp.maximum(n, 1).astype(jnp.float32) / max_exact)
                         / math.log(MAX_DISTANCE / max_exact) * (N_BUCKETS - max_exact)).astype(jnp.int32)
    large = jnp.minimum(large, N_BUCKETS - 1)
    return jnp.where(n < max_exact, n, large)


def dilated_branch(q, k, v, bias_tab, window, dil):
    b, h, s, hd = q.shape
    blk = window // dil
    L = s // dil
    nb = -(-L // blk)
    Lp = nb * blk

    def to_sub(t):
        t = t.reshape(b, h, L, dil, hd).transpose(0, 1, 3, 2, 4)
        t = jnp.pad(t, ((0, 0), (0, 0), (0, 0), (0, Lp - L), (0, 0)))
        return t.reshape(b, h, dil, nb, blk, hd)

    def with_prev(t):
        prev = jnp.pad(t, ((0, 0), (0, 0), (0, 0), (1, 0), (0, 0), (0, 0)))[:, :, :, :-1]
        return jnp.concatenate([prev, t], axis=4)

    qs = to_sub(q)
    kb = with_prev(to_sub(k))
    vb = with_prev(to_sub(v))
    i = jnp.arange(blk)[:, None]
    j = jnp.arange(2 * blk)[None, :]
    steps = blk + i - j
    band = (steps >= 0) & (steps <= blk)
    has_prev = (jnp.arange(nb) > 0)[:, None, None] | (j >= blk)[None]
    valid = band[None] & has_prev
    bias = jnp.transpose(bias_tab[t5_bucket(steps * dil)], (2, 0, 1)).astype(jnp.float32)
    logits = jnp.einsum('bhrnqd,bhrnkd->bhrnqk', qs, kb,
                        preferred_element_type=jnp.float32) * ATTN_SCALE + bias[None, :, None, None]
    logits = jnp.where(valid[None, None, None], logits, -jnp.inf)
    lse = jax.scipy.special.logsumexp(logits, axis=-1)
    p = jnp.exp(logits - lse[..., None])
    out = jnp.einsum('bhrnqk,bhrnkd->bhrnqd', p.astype(v.dtype), vb)
    out = out.reshape(b, h, dil, Lp, hd)[:, :, :, :L].transpose(0, 1, 3, 2, 4).reshape(b, h, s, hd)
    lse = lse.reshape(b, h, dil, Lp)[:, :, :, :L].transpose(0, 1, 3, 2).reshape(b, h, s)
    return out, lse


def dilated_attention(q, k, v, bias_tab):
    outs, lses = [], []
    for window, dil in DIL_PATTERNS:
        o, l = dilated_branch(q, k, v, bias_tab, window, dil)
        outs.append(o)
        lses.append(l)
    w = jax.nn.softmax(jnp.stack(lses, axis=0), axis=0)
    o = jnp.stack(outs, axis=0).astype(jnp.float32)
    return jnp.sum(w[..., None] * o, axis=0).astype(q.dtype)


def diff_attention(q1, q2, k1, k2, v, bias_tab, lam, lam_init, subln_g):
    b, h, s, hd = q1.shape
    nq = s // Q_BLOCK
    kpos = jnp.arange(s)

    def blockify(t):
        return jnp.moveaxis(t.reshape(b, h, nq, Q_BLOCK, hd), 2, 0)

    def one_block(args):
        qb1, qb2, bi = args
        qpos = bi * Q_BLOCK + jnp.arange(Q_BLOCK)
        dist = qpos[:, None] - kpos[None, :]
        bias = jnp.moveaxis(bias_tab[t5_bucket(dist)], -1, 0).astype(jnp.float32)
        causal = dist >= 0

        def attn_map(qb, kk):
            logits = jnp.einsum('bhqd,bhkd->bhqk', qb, kk,
                                preferred_element_type=jnp.float32) * ATTN_SCALE + bias
            return jax.nn.softmax(jnp.where(causal, logits, -jnp.inf), axis=-1)

        a = attn_map(qb1, k1) - lam * attn_map(qb2, k2)
        return jnp.einsum('bhqk,bhkd->bhqd', a.astype(v.dtype), v)

    out = lax.map(one_block, (blockify(q1), blockify(q2), jnp.arange(nq)))
    out = jnp.moveaxis(out, 0, 2).reshape(b, h, s, 2 * hd)
    return rms_norm(out, subln_g) * (1.0 - lam_init)


def hybrid_mixer(xn, w_in, w_out, q_norm_a, k_norm_a, q_norm_b, k_norm_b,
                 lq1, lk1, lq2, lk2, subln_g, rel_bias, layer):
    b, s, _ = xn.shape
    proj = xn @ w_in
    qa, ka, va, qb, kb, vb = jnp.split(
        proj, [WIDTH_A, 2 * WIDTH_A, 3 * WIDTH_A, 3 * WIDTH_A + WIDTH_B, 3 * WIDTH_A + 2 * WIDTH_B], axis=-1)

    def heads_a(t):
        return t.reshape(b, s, N_HEADS_A, HEAD_DIM).transpose(0, 2, 1, 3)
    qa = rms_norm(heads_a(qa), q_norm_a)
    ka = rms_norm(heads_a(ka), k_norm_a)
    oa = dilated_attention(qa, ka, heads_a(va), rel_bias[:, :N_HEADS_A])

    def halves_b(t):
        return t.reshape(b, s, N_HEADS_B, 2, HEAD_DIM).transpose(3, 0, 2, 1, 4)
    qb = rms_norm(halves_b(qb), q_norm_b)
    kb = rms_norm(halves_b(kb), k_norm_b)
    vb = vb.reshape(b, s, N_HEADS_B, 2 * HEAD_DIM).transpose(0, 2, 1, 3)
    lam_init = 0.8 - 0.6 * math.exp(-0.3 * layer)
    f32 = jnp.float32
    lam = (jnp.exp(jnp.sum(lq1.astype(f32) * lk1.astype(f32)))
           - jnp.exp(jnp.sum(lq2.astype(f32) * lk2.astype(f32))) + lam_init)
    ob = diff_attention(qb[0], qb[1], kb[0], kb[1], vb, rel_bias[:, N_HEADS_A:], lam, lam_init, subln_g)

    oa = oa.transpose(0, 2, 1, 3).reshape(b, s, WIDTH_A)
    ob = ob.transpose(0, 2, 1, 3).reshape(b, s, WIDTH_B)
    return jnp.concatenate([oa, ob], axis=-1) @ w_out


def swiglu(x, w_gate, w_up, w_down):
    return (jax.nn.silu(x @ w_gate) * (x @ w_up)) @ w_down


def moe_swiglu(x, w_router, w_gate, w_up, w_down):
    b, s, d = x.shape
    xt = x.reshape(-1, d)
    n = xt.shape[0]
    logits = jnp.dot(xt, w_router, preferred_element_type=jnp.float32)
    top_val, top_idx = lax.top_k(logits, TOP_K)
    gates = jax.nn.softmax(top_val, axis=-1)
    e_flat = top_idx.reshape(-1)
    g_flat = gates.reshape(-1)
    tok_flat = jnp.arange(n * TOP_K) // TOP_K
    order = jnp.argsort(e_flat)
    e_sorted = e_flat[order]
    tok_sorted = tok_flat[order]
    g_sorted = g_flat[order]
    counts = jnp.zeros((N_EXPERTS,), jnp.int32).at[e_flat].add(1)
    starts = jnp.cumsum(counts) - counts
    padded = (counts + EXPERT_BLOCK - 1) // EXPERT_BLOCK * EXPERT_BLOCK
    pad_ends = jnp.cumsum(padded)
    pad_starts = pad_ends - padded
    dest = pad_starts[e_sorted] + jnp.arange(n * TOP_K) - starts[e_sorted]
    n_blocks = -(-(n * TOP_K) // EXPERT_BLOCK) + N_EXPERTS
    x_pad = jnp.zeros((n_blocks * EXPERT_BLOCK, d), x.dtype).at[dest].set(xt[tok_sorted])
    block_expert = jnp.minimum(
        jnp.searchsorted(pad_ends, jnp.arange(n_blocks) * EXPERT_BLOCK, side='right'), N_EXPERTS - 1)

    def expert_block(args):
        xb, e = args
        return swiglu(xb, w_gate[e], w_up[e], w_down[e])

    y_pad = lax.map(expert_block, (x_pad.reshape(n_blocks, EXPERT_BLOCK, d), block_expert))
    y_sorted = y_pad.reshape(-1, d)[dest]
    out = jnp.zeros((n, d), x.dtype).at[tok_sorted].add(y_sorted * g_sorted[:, None].astype(x.dtype))
    return out.reshape(b, s, d)


def setup_inputs(seed: int = 0) -> dict:
    key = jax.random.key(seed)
    ks = jax.random.split(key, 24)
    f32 = jnp.float32

    def nrm(k, shape, scale):
        return jax.random.normal(k, shape, f32) * scale

    def gain(k, shape):
        return 1.0 + 0.02 * jax.random.normal(k, shape, f32)

    return {
        "x": nrm(ks[0], (BATCH, SEQ, D_MODEL), 1.0),
        "norm_mix_g": gain(ks[1], (DEPTH, D_MODEL)),
        "norm_ffn_g": gain(ks[2], (DEPTH, D_MODEL)),
        "w_in": nrm(ks[3], (DEPTH, D_MODEL, PROJ_WIDTH), D_MODEL ** -0.5),
        "w_out": nrm(ks[4], (DEPTH, MIX_WIDTH, D_MODEL), MIX_WIDTH ** -0.5),
        "q_norm_a": gain(ks[5], (DEPTH, HEAD_DIM)),
        "k_norm_a": gain(ks[6], (DEPTH, HEAD_DIM)),
        "q_norm_b": gain(ks[7], (DEPTH, HEAD_DIM)),
        "k_norm_b": gain(ks[8], (DEPTH, HEAD_DIM)),
        "lambda_q1": nrm(ks[9], (DEPTH, HEAD_DIM), 0.1),
        "lambda_k1": nrm(ks[10], (DEPTH, HEAD_DIM), 0.1),
        "lambda_q2": nrm(ks[11], (DEPTH, HEAD_DIM), 0.1),
        "lambda_k2": nrm(ks[12], (DEPTH, HEAD_DIM), 0.1),
        "subln_g": gain(ks[13], (DEPTH, 2 * HEAD_DIM)),
        "rel_bias": nrm(ks[14], (N_BUCKETS, N_REL_HEADS), 0.5),
        "w_gate_dense": nrm(ks[15], (N_DENSE, D_MODEL, D_FF), D_MODEL ** -0.5),
        "w_up_dense": nrm(ks[16], (N_DENSE, D_MODEL, D_FF), D_MODEL ** -0.5),
        "w_down_dense": nrm(ks[17], (N_DENSE, D_FF, D_MODEL), D_FF ** -0.5),
        "w_router": nrm(ks[18], (N_MOE, D_MODEL, N_EXPERTS), D_MODEL ** -0.5),
        "w_gate_moe": nrm(ks[19], (N_MOE, N_EXPERTS, D_MODEL, D_FF_EXPERT), D_MODEL ** -0.5),
        "w_up_moe": nrm(ks[20], (N_MOE, N_EXPERTS, D_MODEL, D_FF_EXPERT), D_MODEL ** -0.5),
        "w_down_moe": nrm(ks[21], (N_MOE, N_EXPERTS, D_FF_EXPERT, D_MODEL), D_FF_EXPERT ** -0.5),
    }


def reference(x, norm_mix_g, norm_ffn_g, w_in, w_out, q_norm_a, k_norm_a, q_norm_b, k_norm_b,
              lambda_q1, lambda_k1, lambda_q2, lambda_k2, subln_g, rel_bias,
              w_gate_dense, w_up_dense, w_down_dense, w_router, w_gate_moe, w_up_moe, w_down_moe):
    for layer in range(DEPTH):
        h = rms_norm(x, norm_mix_g[layer])
        x = x + hybrid_mixer(h, w_in[layer], w_out[layer], q_norm_a[layer], k_norm_a[layer],
                             q_norm_b[layer], k_norm_b[layer], lambda_q1[layer], lambda_k1[layer],
                             lambda_q2[layer], lambda_k2[layer], subln_g[layer], rel_bias, layer)
        h = rms_norm(x, norm_ffn_g[layer])
        if layer % 2 == 0:
            i = layer // 2
            x = x + swiglu(h, w_gate_dense[i], w_up_dense[i], w_down_dense[i])
        else:
            i = layer // 2
            x = x + moe_swiglu(h, w_router[i], w_gate_moe[i], w_up_moe[i], w_down_moe[i])
    return x
```

```python
import functools
import math

import numpy as np
import jax
import jax.numpy as jnp
from jax import lax
from jax.experimental import pallas as pl
from jax.experimental.pallas import tpu as pltpu

F32 = jnp.float32
BF16 = jnp.bfloat16

D_MODEL = 1024
SEQ = 2048
HEAD_DIM = 64
ATTN_SCALE = HEAD_DIM ** -0.5
N_HEADS_A = 8
DIL_PATTERNS = ((128, 1), (512, 4), (2048, 16))
DIL_BLOCK = 128
N_HEADS_B = 4
WIDTH_A = N_HEADS_A * HEAD_DIM
WIDTH_B = N_HEADS_B * 2 * HEAD_DIM
PROJ_WIDTH = 3 * (WIDTH_A + WIDTH_B)
N_BUCKETS = 32
MAX_DISTANCE = 128
D_FF = 2816
N_EXPERTS = 8
TOP_K = 2
D_FF_EXPERT = 3584
EPS = 1e-6
NEG = -1e30

LANES = 128
MXU_DIM = 256
VMEM_LIMIT = 56 * 1024 * 1024

ROW_TILE = 512
PROJ_COL_TILE = 512
Q_TILE_B = 256
MOE_BLOCK = 512
MOE_FF_TILE = 512
FFN_CHUNKS = (512, 512, 512, 512, 512, 256)
COMBINE_TILE = 256


def _cparams(sem):
    return pltpu.CompilerParams(dimension_semantics=sem, vmem_limit_bytes=VMEM_LIMIT)


def _resident(shape):
    nd = len(shape)
    return pl.BlockSpec(shape, lambda *_: (0,) * nd, pipeline_mode=pl.Buffered(1))


def _t5_bucket_np(dist):
    n = np.maximum(dist, 0)
    max_exact = N_BUCKETS // 2
    large = max_exact + (np.log(np.maximum(n, 1).astype(np.float32) / max_exact)
                         / math.log(MAX_DISTANCE / max_exact) * (N_BUCKETS - max_exact)).astype(np.int32)
    large = np.minimum(large, N_BUCKETS - 1)
    return np.where(n < max_exact, n, large).astype(np.int32)


def _dilated_bias_tables(rel_bias):
    i = np.arange(DIL_BLOCK)[:, None]
    j = np.arange(2 * DIL_BLOCK)[None, :]
    steps = DIL_BLOCK + i - j
    valid = (steps >= 0) & (steps <= DIL_BLOCK)
    tabs = []
    for _, dil in DIL_PATTERNS:
        bucket = _t5_bucket_np(steps * dil)
        b = jnp.transpose(rel_bias[bucket, :N_HEADS_A], (2, 0, 1)).astype(F32)
        tabs.append(jnp.where(valid[None], b, NEG))
    return jnp.stack(tabs, axis=0)


def _diff_bias_tables(rel_bias):
    t = Q_TILE_B
    qi = np.arange(t)[:, None]
    kj = np.arange(t)[None, :]
    tiles = []
    for off in (0, t):
        dist = qi - kj + off
        b = jnp.transpose(rel_bias[_t5_bucket_np(dist), N_HEADS_A:], (2, 0, 1)).astype(F32)
        tiles.append(jnp.where((dist >= 0)[None], b, NEG))
    far_bucket = int(_t5_bucket_np(np.array([t + 1]))[0])
    assert far_bucket == int(_t5_bucket_np(np.array([SEQ]))[0])
    far = rel_bias[far_bucket, N_HEADS_A:].astype(F32)
    return jnp.stack(tiles, axis=1), far


def _proj_kernel(x_ref, g_ref, w_ref, gain_ref, gmat_ref, o_ref, xn_ref):
    j = pl.program_id(1)

    @pl.when(j == 0)
    def _():
        x = x_ref[...]
        ms = jnp.mean(x * x, axis=-1, keepdims=True)
        xn_ref[...] = (x * lax.rsqrt(ms + EPS) * g_ref[...]).astype(BF16)

    acc = jnp.dot(xn_ref[...], w_ref[...], preferred_element_type=F32)
    is_value = (j % 3) == 2

    @pl.when(is_value)
    def _():
        o_ref[...] = acc.astype(BF16)

    @pl.when(jnp.logical_not(is_value))
    def _():
        sq = (acc * acc).astype(BF16)
        ss = jnp.concatenate(
            [jnp.dot(sq[:, c:c + MXU_DIM], gmat_ref[...], preferred_element_type=F32)
             for c in range(0, PROJ_COL_TILE, MXU_DIM)], axis=1)
        o_ref[...] = (acc * lax.rsqrt(ss * (1.0 / HEAD_DIM) + EPS) * gain_ref[0]).astype(BF16)


def _proj_call(x2d, g, w_bf16, gains, gmat):
    n = x2d.shape[0]
    n_col = PROJ_WIDTH // PROJ_COL_TILE
    return pl.pallas_call(
        _proj_kernel,
        grid=(n // ROW_TILE, n_col),
        in_specs=[
            pl.BlockSpec((ROW_TILE, D_MODEL), lambda i, j: (i, 0)),
            pl.BlockSpec((1, D_MODEL), lambda i, j: (0, 0)),
            pl.BlockSpec((D_MODEL, PROJ_COL_TILE), lambda i, j: (0, j)),
            pl.BlockSpec((1, 1, PROJ_COL_TILE), lambda i, j: (j, 0, 0)),
            pl.BlockSpec((MXU_DIM, MXU_DIM), lambda i, j: (0, 0)),
        ],
        out_specs=pl.BlockSpec((ROW_TILE, PROJ_COL_TILE), lambda i, j: (i, j)),
        out_shape=jax.ShapeDtypeStruct((n, PROJ_WIDTH), BF16),
        scratch_shapes=[pltpu.VMEM((ROW_TILE, D_MODEL), BF16)],
        compiler_params=_cparams(("parallel", "arbitrary")),
        name="norm_in_proj",
    )(x2d, g, w_bf16, gains, gmat)


def _mixer_a_kernel(q_ref, k_ref, v_ref, bias_ref, o_ref, qf, kf, vf, m_s, l_s, acc_s):
    qf[...] = q_ref[0].astype(F32)
    kf[...] = k_ref[0].astype(F32)
    vf[...] = v_ref[0].astype(F32)
    lane = lax.broadcasted_iota(jnp.int32, (DIL_BLOCK, LANES), 1)
    head0 = lane < HEAD_DIM
    n_tiles = SEQ // DIL_BLOCK

    for bi, (window, dil) in enumerate(DIL_PATTERNS):
        nb = SEQ // dil // DIL_BLOCK
        has_prev = nb > 1
        first = bi == 0

        def tile(t, carry, bi=bi, dil=dil, nb=nb, has_prev=has_prev, first=first):
            r = t // nb
            n = t % nb
            start = r + n * (DIL_BLOCK * dil)
            rows = pl.ds(start, DIL_BLOCK, stride=dil) if dil > 1 else pl.ds(start, DIL_BLOCK)
            qt = qf[rows, :]
            kc = kf[rows, :]
            vc = vf[rows, :]
            if has_prev:
                pstart = jnp.maximum(start - DIL_BLOCK * dil, r)
                prows = (pl.ds(pstart, DIL_BLOCK, stride=dil) if dil > 1 else pl.ds(pstart, DIL_BLOCK))
                kt = jnp.concatenate([kf[prows, :], kc], axis=0).astype(BF16)
                vt = jnp.concatenate([vf[prows, :], vc], axis=0).astype(BF16)
                col = lax.broadcasted_iota(jnp.int32, (1, 2 * DIL_BLOCK), 1)
                pen = jnp.where((col < DIL_BLOCK) & (n == 0), NEG, 0.0).astype(F32)
            else:
                kt = kc.astype(BF16)
                vt = vc.astype(BF16)
            reps = 2 if has_prev else 1
            pv_h, alpha_h = [], []
            for hh in range(2):
                qm = jnp.where(head0 if hh == 0 else jnp.logical_not(head0), qt, 0.0).astype(BF16)
                s = lax.dot_general(qm, kt, (((1,), (1,)), ((), ())), preferred_element_type=F32)
                if has_prev:
                    s = s + bias_ref[bi, hh] + pen
                else:
                    s = s + bias_ref[bi, hh, :, DIL_BLOCK:]
                m_cur = jnp.max(s, axis=-1, keepdims=True)
                if first:
                    m_new = jnp.broadcast_to(m_cur, (DIL_BLOCK, LANES))
                else:
                    m_old = m_s[hh, rows, :]
                    m_new = jnp.maximum(m_old, m_cur)
                p = jnp.exp(s - jnp.concatenate([m_new] * reps, axis=1))
                l_cur = jnp.sum(p, axis=-1, keepdims=True)
                pv = jnp.dot(p.astype(BF16), vt, preferred_element_type=F32)
                if first:
                    l_new = jnp.broadcast_to(l_cur, (DIL_BLOCK, LANES))
                else:
                    alpha = jnp.exp(m_old - m_new)
                    l_new = l_s[hh, rows, :] * alpha + l_cur
                    alpha_h.append(alpha)
                m_s[hh, rows, :] = m_new
                l_s[hh, rows, :] = l_new
                pv_h.append(pv)
            pv = jnp.where(head0, pv_h[0], pv_h[1])
            if first:
                acc_s[rows, :] = pv
            else:
                alpha = jnp.where(head0, alpha_h[0], alpha_h[1])
                acc_s[rows, :] = acc_s[rows, :] * alpha + pv
            return carry

        lax.fori_loop(0, n_tiles, tile, 0)

    lane_full = lax.broadcasted_iota(jnp.int32, (SEQ, LANES), 1)
    l_full = jnp.where(lane_full < HEAD_DIM, l_s[0], l_s[1])
    o_ref[0] = (acc_s[...] / l_full).astype(o_ref.dtype)


def _mixer_a_call(proj3d, bias_a):
    b = proj3d.shape[0]
    n_pairs = N_HEADS_A // 2
    q_blk0, k_blk0, v_blk0 = 0, WIDTH_A // LANES, 2 * WIDTH_A // LANES
    return pl.pallas_call(
        _mixer_a_kernel,
        grid=(b, n_pairs),
        in_specs=[
            pl.BlockSpec((1, SEQ, LANES), lambda i, p: (i, 0, q_blk0 + p)),
            pl.BlockSpec((1, SEQ, LANES), lambda i, p: (i, 0, k_blk0 + p)),
            pl.BlockSpec((1, SEQ, LANES), lambda i, p: (i, 0, v_blk0 + p)),
            pl.BlockSpec((len(DIL_PATTERNS), 2, DIL_BLOCK, 2 * DIL_BLOCK), lambda i, p: (0, p, 0, 0)),
        ],
        out_specs=pl.BlockSpec((1, SEQ, LANES), lambda i, p: (i, 0, p)),
        out_shape=jax.ShapeDtypeStruct((b, SEQ, WIDTH_A), BF16),
        scratch_shapes=[
            pltpu.VMEM((SEQ, LANES), F32), pltpu.VMEM((SEQ, LANES), F32), pltpu.VMEM((SEQ, LANES), F32),
            pltpu.VMEM((2, SEQ, LANES), F32), pltpu.VMEM((2, SEQ, LANES), F32),
            pltpu.VMEM((SEQ, LANES), F32),
        ],
        compiler_params=_cparams(("parallel", "parallel")),
        name="dilated_attention",
    )(proj3d, proj3d, proj3d, bias_a)


def _mixer_b_kernel(far_ref, lam_ref, q_ref, k_ref, v_ref, bias_ref, subg_ref, o_ref, *, lam_init):
    h = pl.program_id(1)
    i = pl.program_id(2)
    t = Q_TILE_B
    reps = t // LANES
    q = q_ref[0]
    lane = lax.broadcasted_iota(jnp.int32, (t, LANES), 1)
    zero = jnp.zeros_like(q)
    q_half = (jnp.where(lane < HEAD_DIM, q, zero), jnp.where(lane >= HEAD_DIM, q, zero))

    def step(j, bias, carry):
        koff = pl.multiple_of(j * t, t)
        kt = k_ref[0, pl.ds(koff, t), :]
        vt = v_ref[0, pl.ds(koff, t), :]
        new = []
        for half in range(2):
            m_old, l_old, a_old = carry[half]
            s = lax.dot_general(q_half[half], kt, (((1,), (1,)), ((), ())),
                                preferred_element_type=F32) + bias
            m_new = jnp.maximum(m_old, jnp.max(s, axis=-1, keepdims=True))
            p = jnp.exp(s - jnp.concatenate([m_new] * reps, axis=1))
            alpha = jnp.exp(m_old - m_new)
            l_new = l_old * alpha + jnp.sum(p, axis=-1, keepdims=True)
            a_new = a_old * alpha + jnp.dot(p.astype(BF16), vt, preferred_element_type=F32)
            new.append((m_new, l_new, a_new))
        return tuple(new)

    init_one = (jnp.full((t, LANES), NEG, F32), jnp.zeros((t, LANES), F32), jnp.zeros((t, LANES), F32))
    carry = (init_one, init_one)
    far = far_ref[h]
    carry = lax.fori_loop(0, jnp.maximum(i - 1, 0), lambda j, c: step(j, far, c), carry)
    pen = jnp.where(i == 0, NEG, 0.0).astype(F32)
    carry = step(jnp.maximum(i - 1, 0), bias_ref[0, 1] + pen, carry)
    carry = step(i, bias_ref[0, 0], carry)

    lam_rows = lam_ref[...]
    e1 = jnp.exp(jnp.sum(lam_rows[0:1, :] * lam_rows[1:2, :], axis=-1, keepdims=True))
    e2 = jnp.exp(jnp.sum(lam_rows[2:3, :] * lam_rows[3:4, :], axis=-1, keepdims=True))
    lam = e1 - e2 + lam_init
    (_, l1, a1), (_, l2, a2) = carry
    o = a1 / l1 - lam * (a2 / l2)
    ms = jnp.mean(o * o, axis=-1, keepdims=True)
    o_ref[0] = (o * lax.rsqrt(ms + EPS) * subg_ref[...] * (1.0 - lam_init)).astype(o_ref.dtype)


def _mixer_b_call(proj3d, bias_b, far_b, lam_rows, subg, lam_init):
    b = proj3d.shape[0]
    base = 3 * WIDTH_A // LANES
    q_blk0, k_blk0, v_blk0 = base, base + WIDTH_B // LANES, base + 2 * WIDTH_B // LANES
    t = Q_TILE_B
    grid_spec = pltpu.PrefetchScalarGridSpec(
        num_scalar_prefetch=1,
        grid=(b, N_HEADS_B, SEQ // t),
        in_specs=[
            pl.BlockSpec((8, LANES), lambda bi, h, i, far: (0, 0)),
            pl.BlockSpec((1, t, LANES), lambda bi, h, i, far: (bi, i, q_blk0 + h)),
            pl.BlockSpec((1, SEQ, LANES), lambda bi, h, i, far: (bi, 0, k_blk0 + h)),
            pl.BlockSpec((1, SEQ, LANES), lambda bi, h, i, far: (bi, 0, v_blk0 + h)),
            pl.BlockSpec((1, 2, t, t), lambda bi, h, i, far: (h, 0, 0, 0)),
            pl.BlockSpec((1, LANES), lambda bi, h, i, far: (0, 0)),
        ],
        out_specs=pl.BlockSpec((1, t, LANES), lambda bi, h, i, far: (bi, i, h)),
    )
    return pl.pallas_call(
        functools.partial(_mixer_b_kernel, lam_init=lam_init),
        grid_spec=grid_spec,
        out_shape=jax.ShapeDtypeStruct((b, SEQ, WIDTH_B), BF16),
        compiler_params=_cparams(("parallel", "parallel", "arbitrary")),
        name="diff_attention",
    )(far_b, lam_rows, proj3d, proj3d, proj3d, bias_b, subg)


def _out_proj_kernel(oa_ref, ob_ref, w_ref, x_ref, g_ref, *rest, with_router):
    if with_router:
        wr_ref, x1_ref, h_ref, route_ref = rest
    else:
        x1_ref, h_ref = rest
    acc = jnp.dot(oa_ref[...], w_ref[:WIDTH_A, :], preferred_element_type=F32)
    acc = acc + jnp.dot(ob_ref[...], w_ref[WIDTH_A:, :], preferred_element_type=F32)
    x1 = x_ref[...] + acc
    x1_ref[...] = x1
    ms = jnp.mean(x1 * x1, axis=-1, keepdims=True)
    hn = x1 * lax.rsqrt(ms + EPS) * g_ref[...]
    h_ref[...] = hn.astype(h_ref.dtype)
    if with_router:
        hi = hn.astype(BF16)
        lo = (hn - hi.astype(F32)).astype(BF16)
        lg = (jnp.dot(hi, wr_ref[...], preferred_element_type=F32)
              + jnp.dot(lo, wr_ref[...], preferred_element_type=F32))
        lg = lg + pltpu.roll(lg, LANES - N_EXPERTS, 1)
        lane = lax.broadcasted_iota(jnp.int32, lg.shape, 1)
        lane_f = lane.astype(F32)
        lg = jnp.where(lane < N_EXPERTS, lg, -jnp.inf)
        v1 = jnp.max(lg, axis=-1, keepdims=True)
        i1 = jnp.min(jnp.where(lg == v1, lane_f, float(LANES)), axis=-1, keepdims=True)
        lg2 = jnp.where(lane_f == i1, -jnp.inf, lg)
        v2 = jnp.max(lg2, axis=-1, keepdims=True)
        i2 = jnp.min(jnp.where(lg2 == v2, lane_f, float(LANES)), axis=-1, keepdims=True)
        e = jnp.exp(v2 - v1)
        g1 = 1.0 / (1.0 + e)
        g2 = e / (1.0 + e)
        route_ref[...] = jnp.where(lane == 0, i1, jnp.where(lane == 1, i2,
                                   jnp.where(lane == 2, g1, jnp.where(lane == 3, g2, 0.0))))


def _out_proj_call(oa, ob, w_bf16, x2d, g, w_router_packed, h_dtype):
    n = x2d.shape[0]
    with_router = w_router_packed is not None
    row = lambda i: (i, 0)
    in_specs = [
        pl.BlockSpec((ROW_TILE, WIDTH_A), row),
        pl.BlockSpec((ROW_TILE, WIDTH_B), row),
        _resident((WIDTH_A + WIDTH_B, D_MODEL)),
        pl.BlockSpec((ROW_TILE, D_MODEL), row),
        _resident((1, D_MODEL)),
    ]
    out_specs = [pl.BlockSpec((ROW_TILE, D_MODEL), row), pl.BlockSpec((ROW_TILE, D_MODEL), row)]
    out_shape = [jax.ShapeDtypeStruct((n, D_MODEL), F32), jax.ShapeDtypeStruct((n, D_MODEL), h_dtype)]
    args = [oa, ob, w_bf16, x2d, g]
    if with_router:
        in_specs.append(_resident((D_MODEL, LANES)))
        out_specs.append(pl.BlockSpec((ROW_TILE, LANES), row))
        out_shape.append(jax.ShapeDtypeStruct((n, LANES), F32))
        args.append(w_router_packed)
    return pl.pallas_call(
        functools.partial(_out_proj_kernel, with_router=with_router),
        grid=(n // ROW_TILE,),
        in_specs=in_specs,
        out_specs=out_specs,
        out_shape=out_shape,
        compiler_params=_cparams(("parallel",)),
        name="out_proj_router" if with_router else "out_proj",
    )(*args)


def _swiglu_act(g, u):
    return (g / (1.0 + jnp.exp(-g))) * u


def _ffn_kernel(h_ref, x_ref, wg_ref, wu_ref, wd_ref, o_ref, a_ref):
    h = h_ref[...]
    c0 = 0
    for width in FFN_CHUNKS:
        g = jnp.dot(h, wg_ref[:, c0:c0 + width], preferred_element_type=F32)
        u = jnp.dot(h, wu_ref[:, c0:c0 + width], preferred_element_type=F32)
        a_ref[:, c0:c0 + width] = _swiglu_act(g, u).astype(BF16)
        c0 += width
    o_ref[...] = x_ref[...] + jnp.dot(a_ref[...], wd_ref[...], preferred_element_type=F32)


def _ffn_call(h, x2d, wg, wu, wd):
    n = x2d.shape[0]
    row = lambda i: (i, 0)
    return pl.pallas_call(
        _ffn_kernel,
        grid=(n // ROW_TILE,),
        in_specs=[
            pl.BlockSpec((ROW_TILE, D_MODEL), row),
            pl.BlockSpec((ROW_TILE, D_MODEL), row),
            _resident((D_MODEL, D_FF)),
            _resident((D_MODEL, D_FF)),
            _resident((D_FF, D_MODEL)),
        ],
        out_specs=pl.BlockSpec((ROW_TILE, D_MODEL), row),
        out_shape=jax.ShapeDtypeStruct((n, D_MODEL), F32),
        scratch_shapes=[pltpu.VMEM((ROW_TILE, D_FF), BF16)],
        compiler_params=_cparams(("parallel",)),
        name="dense_swiglu",
    )(h, x2d, wg, wu, wd)


def _row_copy(src_hbm, row, dst_ref, slot, sem):
    return pltpu.make_async_copy(src_hbm.at[pl.ds(row, 1)], dst_ref.at[pl.ds(slot, 1)], sem)


def _gather_kernel(tok_ref, h_hbm, o_ref, sem):
    def start(r, c):
        _row_copy(h_hbm, tok_ref[0, 0, r], o_ref, r, sem).start()
        return c

    def wait(r, c):
        _row_copy(h_hbm, 0, o_ref, r, sem).wait()
        return c

    lax.fori_loop(0, MOE_BLOCK, start, 0)
    lax.fori_loop(0, MOE_BLOCK, wait, 0)


def _gather_call(row_tok3d, h):
    n_blocks = row_tok3d.shape[0]
    return pl.pallas_call(
        _gather_kernel,
        grid=(n_blocks,),
        in_specs=[
            pl.BlockSpec((1, 1, MOE_BLOCK), lambda b: (b, 0, 0), memory_space=pltpu.SMEM),
            pl.BlockSpec(memory_space=pl.ANY),
        ],
        out_specs=pl.BlockSpec((MOE_BLOCK, D_MODEL), lambda b: (b, 0)),
        out_shape=jax.ShapeDtypeStruct((n_blocks * MOE_BLOCK, D_MODEL), h.dtype),
        scratch_shapes=[pltpu.SemaphoreType.DMA(())],
        compiler_params=_cparams(("arbitrary",)),
        name="moe_gather",
    )(row_tok3d, h)


def _moe_kernel(be_ref, nact_ref, x_ref, gate_ref, wg_ref, wu_ref, wd_ref, o_ref, xb_ref, acc_ref):
    b = pl.program_id(0)
    c = pl.program_id(1)
    last = pl.num_programs(1) - 1
    active = b < nact_ref[0]

    @pl.when(active)
    def _():
        @pl.when(c == 0)
        def _():
            xb_ref[...] = x_ref[...].astype(BF16)

        xb = xb_ref[...]
        g = jnp.dot(xb, wg_ref[0], preferred_element_type=F32)
        u = jnp.dot(xb, wu_ref[0], preferred_element_type=F32)
        contrib = jnp.dot(_swiglu_act(g, u).astype(BF16), wd_ref[0], preferred_element_type=F32)

        @pl.when(c == 0)
        def _():
            acc_ref[...] = contrib

        @pl.when(c > 0)
        def _():
            acc_ref[...] += contrib

        @pl.when(c == last)
        def _():
            o_ref[...] = acc_ref[...] * gate_ref[...]

    @pl.when(jnp.logical_not(active) & (c == last))
    def _():
        o_ref[...] = jnp.zeros_like(o_ref)


def _moe_call(block_expert, n_active, x_pad, gate_pad, wg, wu, wd):
    n_rows = x_pad.shape[0]
    n_blocks = n_rows // MOE_BLOCK
    n_chunks = D_FF_EXPERT // MOE_FF_TILE
    grid_spec = pltpu.PrefetchScalarGridSpec(
        num_scalar_prefetch=2,
        grid=(n_blocks, n_chunks),
        in_specs=[
            pl.BlockSpec((MOE_BLOCK, D_MODEL), lambda b, c, be, na: (b, 0)),
            pl.BlockSpec((MOE_BLOCK, 1), lambda b, c, be, na: (b, 0)),
            pl.BlockSpec((1, D_MODEL, MOE_FF_TILE), lambda b, c, be, na: (be[b], 0, c)),
            pl.BlockSpec((1, D_MODEL, MOE_FF_TILE), lambda b, c, be, na: (be[b], 0, c)),
            pl.BlockSpec((1, MOE_FF_TILE, D_MODEL), lambda b, c, be, na: (be[b], c, 0)),
        ],
        out_specs=pl.BlockSpec((MOE_BLOCK, D_MODEL), lambda b, c, be, na: (b, 0)),
        scratch_shapes=[pltpu.VMEM((MOE_BLOCK, D_MODEL), BF16), pltpu.VMEM((MOE_BLOCK, D_MODEL), F32)],
    )
    return pl.pallas_call(
        _moe_kernel,
        grid_spec=grid_spec,
        out_shape=jax.ShapeDtypeStruct((n_rows, D_MODEL), F32),
        compiler_params=_cparams(("arbitrary", "arbitrary")),
        name="moe_swiglu",
    )(block_expert, n_active, x_pad, gate_pad, wg, wu, wd)


def _combine_kernel(pos_ref, y_hbm, x_ref, o_ref, buf, sem):
    def start(t, c):
        for k in range(TOP_K):
            _row_copy(y_hbm, pos_ref[0, 0, TOP_K * t + k], buf.at[k], t, sem).start()
        return c

    def wait(t, c):
        for k in range(TOP_K):
            _row_copy(y_hbm, 0, buf.at[k], t, sem).wait()
        return c

    lax.fori_loop(0, COMBINE_TILE, start, 0)
    lax.fori_loop(0, COMBINE_TILE, wait, 0)
    o_ref[...] = x_ref[...] + buf[0] + buf[1]


def _combine_call(pos3d, y_pad, x2d):
    n = x2d.shape[0]
    row = lambda i: (i, 0)
    return pl.pallas_call(
        _combine_kernel,
        grid=(n // COMBINE_TILE,),
        in_specs=[
            pl.BlockSpec((1, 1, TOP_K * COMBINE_TILE), lambda i: (i, 0, 0), memory_space=pltpu.SMEM),
            pl.BlockSpec(memory_space=pl.ANY),
            pl.BlockSpec((COMBINE_TILE, D_MODEL), row),
        ],
        out_specs=pl.BlockSpec((COMBINE_TILE, D_MODEL), row),
        out_shape=jax.ShapeDtypeStruct((n, D_MODEL), F32),
        scratch_shapes=[pltpu.VMEM((TOP_K, COMBINE_TILE, D_MODEL), F32), pltpu.SemaphoreType.DMA(())],
        compiler_params=_cparams(("arbitrary",)),
        name="moe_combine",
    )(pos3d, y_pad, x2d)


def _moe_layer(x1, hn, route, wg, wu, wd):
    n = x1.shape[0]
    e_flat = route[:, :TOP_K].astype(jnp.int32).reshape(-1)
    g_flat = route[:, TOP_K:2 * TOP_K].reshape(-1)
    onehot = (e_flat[:, None] == jnp.arange(N_EXPERTS, dtype=jnp.int32)[None, :]).astype(jnp.int32)
    csum = jnp.cumsum(onehot, axis=0)
    rank = jnp.sum((csum - onehot) * onehot, axis=1)
    counts = csum[-1]
    padded = (counts + MOE_BLOCK - 1) // MOE_BLOCK * MOE_BLOCK
    pad_ends = jnp.cumsum(padded)
    pad_starts = pad_ends - padded
    dest = pad_starts[e_flat] + rank
    n_blocks = -(-(n * TOP_K) // MOE_BLOCK) + N_EXPERTS
    n_rows = n_blocks * MOE_BLOCK
    row_tok = jnp.zeros((n_rows,), jnp.int32).at[dest].set(jnp.arange(n * TOP_K, dtype=jnp.int32) // TOP_K)
    gate_pad = jnp.zeros((n_rows,), F32).at[dest].set(g_flat)
    block_expert = jnp.minimum(
        jnp.searchsorted(pad_ends, jnp.arange(n_blocks, dtype=jnp.int32) * MOE_BLOCK, side='right'),
        N_EXPERTS - 1).astype(jnp.int32)
    n_active = (pad_ends[-1:] // MOE_BLOCK).astype(jnp.int32)

    x_pad = _gather_call(row_tok.reshape(n_blocks, 1, MOE_BLOCK), hn)
    y_pad = _moe_call(block_expert, n_active, x_pad, gate_pad.reshape(n_rows, 1), wg, wu, wd)
    pos3d = dest.reshape(n // COMBINE_TILE, 1, TOP_K * COMBINE_TILE)
    return _combine_call(pos3d, y_pad, x1)


def _head_group_matrix():
    idx = np.arange(MXU_DIM) // HEAD_DIM
    return jnp.asarray((idx[:, None] == idx[None, :]).astype(np.float32), dtype=BF16)


def _pack_router(w_router):
    hi = w_router.astype(BF16)
    lo = (w_router - hi.astype(F32)).astype(BF16)
    pad = jnp.zeros((D_MODEL, LANES - 2 * N_EXPERTS), BF16)
    return jnp.concatenate([hi, lo, pad], axis=1)


def kernel(x, norm_mix_g, norm_ffn_g, w_in, w_out, q_norm_a, k_norm_a, q_norm_b, k_norm_b,
           lambda_q1, lambda_k1, lambda_q2, lambda_k2, subln_g, rel_bias,
           w_gate_dense, w_up_dense, w_down_dense, w_router, w_gate_moe, w_up_moe, w_down_moe):
    b, s, d = x.shape
    assert (s, d) == (SEQ, D_MODEL)
    depth = w_in.shape[0]
    n = b * s
    x2d = x.reshape(n, d)
    gmat = _head_group_matrix()
    bias_a = _dilated_bias_tables(rel_bias)
    bias_b, far_b = _diff_bias_tables(rel_bias)
    ones = jnp.ones((PROJ_COL_TILE,), F32)
    rep = PROJ_COL_TILE // HEAD_DIM

    for layer in range(depth):
        gains = jnp.stack([
            jnp.tile(q_norm_a[layer], rep) * ATTN_SCALE, jnp.tile(k_norm_a[layer], rep), ones,
            jnp.tile(q_norm_b[layer], rep) * ATTN_SCALE, jnp.tile(k_norm_b[layer], rep), ones,
        ]).reshape(-1, 1, PROJ_COL_TILE)
        proj = _proj_call(x2d, norm_mix_g[layer].reshape(1, d), w_in[layer].astype(BF16), gains, gmat)
        proj3d = proj.reshape(b, s, PROJ_WIDTH)

        oa = _mixer_a_call(proj3d, bias_a)
        lam_init = 0.8 - 0.6 * math.exp(-0.3 * layer)
        lam_rows = jnp.zeros((8, LANES), F32).at[:4, :HEAD_DIM].set(
            jnp.stack([lambda_q1[layer], lambda_k1[layer], lambda_q2[layer], lambda_k2[layer]]))
        ob = _mixer_b_call(proj3d, bias_b, far_b, lam_rows, subln_g[layer].reshape(1, LANES), lam_init)

        is_moe = layer % 2 == 1
        i = layer // 2
        outs = _out_proj_call(
            oa.reshape(n, WIDTH_A), ob.reshape(n, WIDTH_B), w_out[layer].astype(BF16), x2d,
            norm_ffn_g[layer].reshape(1, d), _pack_router(w_router[i]) if is_moe else None,
            F32 if is_moe else BF16)
        if is_moe:
            x1, hn, route = outs
            x2d = _moe_layer(x1, hn, route, w_gate_moe[i].astype(BF16), w_up_moe[i].astype(BF16),
                             w_down_moe[i].astype(BF16))
        else:
            x1, hn = outs
            x2d = _ffn_call(hn, x1, w_gate_dense[i].astype(BF16), w_up_dense[i].astype(BF16),
                            w_down_dense[i].astype(BF16))
    return x2d.reshape(b, s, d)
```

```python
import functools
import math

import numpy as np
import jax
import jax.numpy as jnp
from jax import lax
from jax.experimental import pallas as pl
from jax.experimental.pallas import tpu as pltpu

F32 = jnp.float32
BF16 = jnp.bfloat16

D_MODEL = 1024
SEQ = 2048
HEAD_DIM = 64
ATTN_SCALE = HEAD_DIM ** -0.5
N_HEADS_A = 8
DIL_PATTERNS = ((128, 1), (512, 4), (2048, 16))
DIL_BLOCK = 128
N_HEADS_B = 4
WIDTH_A = N_HEADS_A * HEAD_DIM
WIDTH_B = N_HEADS_B * 2 * HEAD_DIM
PROJ_WIDTH = 3 * (WIDTH_A + WIDTH_B)
N_BUCKETS = 32
MAX_DISTANCE = 128
D_FF = 2816
N_EXPERTS = 8
TOP_K = 2
D_FF_EXPERT = 3584
EPS = 1e-6
NEG = -1e30

LANES = 128
MXU_DIM = 256
VMEM_LIMIT = 56 * 1024 * 1024

ROW_TILE = 512
PROJ_COL_TILE = 512
Q_TILE_B = 256
MOE_BLOCK = 512
MOE_FF_TILE = 512
FFN_CHUNKS = (512, 512, 512, 512, 512, 256)
DISPATCH_TILE = 256
COMBINE_TILE = 256


def _cparams(sem):
    return pltpu.CompilerParams(dimension_semantics=sem, vmem_limit_bytes=VMEM_LIMIT)


def _resident(shape):
    nd = len(shape)
    return pl.BlockSpec(shape, lambda *_: (0,) * nd, pipeline_mode=pl.Buffered(1))


def _t5_bucket_np(dist):
    n = np.maximum(dist, 0)
    max_exact = N_BUCKETS // 2
    large = max_exact + (np.log(np.maximum(n, 1).astype(np.float32) / max_exact)
                         / math.log(MAX_DISTANCE / max_exact) * (N_BUCKETS - max_exact)).astype(np.int32)
    large = np.minimum(large, N_BUCKETS - 1)
    return np.where(n < max_exact, n, large).astype(np.int32)


def _dilated_bias_tables(rel_bias):
    i = np.arange(DIL_BLOCK)[:, None]
    j = np.arange(2 * DIL_BLOCK)[None, :]
    steps = DIL_BLOCK + i - j
    valid = (steps >= 0) & (steps <= DIL_BLOCK)
    tabs = []
    for _, dil in DIL_PATTERNS:
        bucket = _t5_bucket_np(steps * dil)
        b = jnp.transpose(rel_bias[bucket, :N_HEADS_A], (2, 0, 1)).astype(F32)
        tabs.append(jnp.where(valid[None], b, NEG))
    return jnp.stack(tabs, axis=0)


def _diff_bias_tables(rel_bias):
    t = Q_TILE_B
    qi = np.arange(t)[:, None]
    kj = np.arange(t)[None, :]
    tiles = []
    for off in (0, t):
        dist = qi - kj + off
        b = jnp.transpose(rel_bias[_t5_bucket_np(dist), N_HEADS_A:], (2, 0, 1)).astype(F32)
        tiles.append(jnp.where((dist >= 0)[None], b, NEG))
    far_bucket = int(_t5_bucket_np(np.array([t + 1]))[0])
    assert far_bucket == int(_t5_bucket_np(np.array([SEQ]))[0])
    far = rel_bias[far_bucket, N_HEADS_A:].astype(F32)
    return jnp.stack(tiles, axis=1), far


def _proj_kernel(x_ref, g_ref, w_ref, gain_ref, gmat_ref, o_ref, xn_ref):
    j = pl.program_id(1)

    @pl.when(j == 0)
    def _():
        x = x_ref[...]
        ms = jnp.mean(x * x, axis=-1, keepdims=True)
        xn_ref[...] = (x * lax.rsqrt(ms + EPS) * g_ref[...]).astype(BF16)

    acc = jnp.dot(xn_ref[...], w_ref[...], preferred_element_type=F32)
    is_value = (j % 3) == 2

    @pl.when(is_value)
    def _():
        o_ref[...] = acc.astype(BF16)

    @pl.when(jnp.logical_not(is_value))
    def _():
        sq = (acc * acc).astype(BF16)
        ss = jnp.concatenate(
            [jnp.dot(sq[:, c:c + MXU_DIM], gmat_ref[...], preferred_element_type=F32)
             for c in range(0, PROJ_COL_TILE, MXU_DIM)], axis=1)
        o_ref[...] = (acc * lax.rsqrt(ss * (1.0 / HEAD_DIM) + EPS) * gain_ref[0]).astype(BF16)


def _proj_call(x2d, g, w_bf16, gains, gmat):
    n = x2d.shape[0]
    n_col = PROJ_WIDTH // PROJ_COL_TILE
    return pl.pallas_call(
        _proj_kernel,
        grid=(n // ROW_TILE, n_col),
        in_specs=[
            pl.BlockSpec((ROW_TILE, D_MODEL), lambda i, j: (i, 0)),
            pl.BlockSpec((1, D_MODEL), lambda i, j: (0, 0)),
            pl.BlockSpec((D_MODEL, PROJ_COL_TILE), lambda i, j: (0, j)),
            pl.BlockSpec((1, 1, PROJ_COL_TILE), lambda i, j: (j, 0, 0)),
            pl.BlockSpec((MXU_DIM, MXU_DIM), lambda i, j: (0, 0)),
        ],
        out_specs=pl.BlockSpec((ROW_TILE, PROJ_COL_TILE), lambda i, j: (i, j)),
        out_shape=jax.ShapeDtypeStruct((n, PROJ_WIDTH), BF16),
        scratch_shapes=[pltpu.VMEM((ROW_TILE, D_MODEL), BF16)],
        compiler_params=_cparams(("parallel", "arbitrary")),
        name="norm_in_proj",
    )(x2d, g, w_bf16, gains, gmat)


def _mixer_a_kernel(q_ref, k_ref, v_ref, bias_ref, o_ref, qf, kf, vf, m_s, l_s, acc_s):
    qf[...] = q_ref[0].astype(F32)
    kf[...] = k_ref[0].astype(F32)
    vf[...] = v_ref[0].astype(F32)
    lane = lax.broadcasted_iota(jnp.int32, (DIL_BLOCK, LANES), 1)
    head0 = lane < HEAD_DIM
    n_tiles = SEQ // DIL_BLOCK

    for bi, (window, dil) in enumerate(DIL_PATTERNS):
        nb = SEQ // dil // DIL_BLOCK
        has_prev = nb > 1
        first = bi == 0

        def tile(t, carry, bi=bi, dil=dil, nb=nb, has_prev=has_prev, first=first):
            r = t // nb
            n = t % nb
            start = r + n * (DIL_BLOCK * dil)
            rows = pl.ds(start, DIL_BLOCK, stride=dil) if dil > 1 else pl.ds(start, DIL_BLOCK)
            qt = qf[rows, :]
            kc = kf[rows, :]
            vc = vf[rows, :]
            if has_prev:
                pstart = jnp.maximum(start - DIL_BLOCK * dil, r)
                prows = (pl.ds(pstart, DIL_BLOCK, stride=dil) if dil > 1 else pl.ds(pstart, DIL_BLOCK))
                kt = jnp.concatenate([kf[prows, :], kc], axis=0).astype(BF16)
                vt = jnp.concatenate([vf[prows, :], vc], axis=0).astype(BF16)
                col = lax.broadcasted_iota(jnp.int32, (1, 2 * DIL_BLOCK), 1)
                pen = jnp.where((col < DIL_BLOCK) & (n == 0), NEG, 0.0).astype(F32)
            else:
                kt = kc.astype(BF16)
                vt = vc.astype(BF16)
            reps = 2 if has_prev else 1
            pv_h, alpha_h = [], []
            for hh in range(2):
                qm = jnp.where(head0 if hh == 0 else jnp.logical_not(head0), qt, 0.0).astype(BF16)
                s = lax.dot_general(qm, kt, (((1,), (1,)), ((), ())), preferred_element_type=F32)
                if has_prev:
                    s = s + bias_ref[bi, hh] + pen
                else:
                    s = s + bias_ref[bi, hh, :, DIL_BLOCK:]
                m_cur = jnp.max(s, axis=-1, keepdims=True)
                if first:
                    m_new = jnp.broadcast_to(m_cur, (DIL_BLOCK, LANES))
                else:
                    m_old = m_s[hh, rows, :]
                    m_new = jnp.maximum(m_old, m_cur)
                p = jnp.exp(s - jnp.concatenate([m_new] * reps, axis=1))
                l_cur = jnp.sum(p, axis=-1, keepdims=True)
                pv = jnp.dot(p.astype(BF16), vt, preferred_element_type=F32)
                if first:
                    l_new = jnp.broadcast_to(l_cur, (DIL_BLOCK, LANES))
                else:
                    alpha = jnp.exp(m_old - m_new)
                    l_new = l_s[hh, rows, :] * alpha + l_cur
                    alpha_h.append(alpha)
                m_s[hh, rows, :] = m_new
                l_s[hh, rows, :] = l_new
                pv_h.append(pv)
            pv = jnp.where(head0, pv_h[0], pv_h[1])
            if first:
                acc_s[rows, :] = pv
            else:
                alpha = jnp.where(head0, alpha_h[0], alpha_h[1])
                acc_s[rows, :] = acc_s[rows, :] * alpha + pv
            return carry

        lax.fori_loop(0, n_tiles, tile, 0)

    lane_full = lax.broadcasted_iota(jnp.int32, (SEQ, LANES), 1)
    l_full = jnp.where(lane_full < HEAD_DIM, l_s[0], l_s[1])
    o_ref[0] = (acc_s[...] / l_full).astype(o_ref.dtype)


def _mixer_a_call(proj3d, bias_a):
    b = proj3d.shape[0]
    n_pairs = N_HEADS_A // 2
    q_blk0, k_blk0, v_blk0 = 0, WIDTH_A // LANES, 2 * WIDTH_A // LANES
    return pl.pallas_call(
        _mixer_a_kernel,
        grid=(b, n_pairs),
        in_specs=[
            pl.BlockSpec((1, SEQ, LANES), lambda i, p: (i, 0, q_blk0 + p)),
            pl.BlockSpec((1, SEQ, LANES), lambda i, p: (i, 0, k_blk0 + p)),
            pl.BlockSpec((1, SEQ, LANES), lambda i, p: (i, 0, v_blk0 + p)),
            pl.BlockSpec((len(DIL_PATTERNS), 2, DIL_BLOCK, 2 * DIL_BLOCK), lambda i, p: (0, p, 0, 0)),
        ],
        out_specs=pl.BlockSpec((1, SEQ, LANES), lambda i, p: (i, 0, p)),
        out_shape=jax.ShapeDtypeStruct((b, SEQ, WIDTH_A), BF16),
        scratch_shapes=[
            pltpu.VMEM((SEQ, LANES), F32), pltpu.VMEM((SEQ, LANES), F32), pltpu.VMEM((SEQ, LANES), F32),
            pltpu.VMEM((2, SEQ, LANES), F32), pltpu.VMEM((2, SEQ, LANES), F32),
            pltpu.VMEM((SEQ, LANES), F32),
        ],
        compiler_params=_cparams(("parallel", "parallel")),
        name="dilated_attention",
    )(proj3d, proj3d, proj3d, bias_a)


def _mixer_b_kernel(far_ref, lam_ref, q_ref, k_ref, v_ref, bias_ref, subg_ref, o_ref, *, lam_init):
    h = pl.program_id(1)
    i = pl.program_id(2)
    t = Q_TILE_B
    reps = t // LANES
    q = q_ref[0]
    lane = lax.broadcasted_iota(jnp.int32, (t, LANES), 1)
    zero = jnp.zeros_like(q)
    q_half = (jnp.where(lane < HEAD_DIM, q, zero), jnp.where(lane >= HEAD_DIM, q, zero))

    def step(j, bias, carry):
        koff = pl.multiple_of(j * t, t)
        kt = k_ref[0, pl.ds(koff, t), :]
        vt = v_ref[0, pl.ds(koff, t), :]
        new = []
        for half in range(2):
            m_old, l_old, a_old = carry[half]
            s = lax.dot_general(q_half[half], kt, (((1,), (1,)), ((), ())),
                                preferred_element_type=F32) + bias
            m_new = jnp.maximum(m_old, jnp.max(s, axis=-1, keepdims=True))
            p = jnp.exp(s - jnp.concatenate([m_new] * reps, axis=1))
            alpha = jnp.exp(m_old - m_new)
            l_new = l_old * alpha + jnp.sum(p, axis=-1, keepdims=True)
            a_new = a_old * alpha + jnp.dot(p.astype(BF16), vt, preferred_element_type=F32)
            new.append((m_new, l_new, a_new))
        return tuple(new)

    init_one = (jnp.full((t, LANES), NEG, F32), jnp.zeros((t, LANES), F32), jnp.zeros((t, LANES), F32))
    carry = (init_one, init_one)
    far = far_ref[h]
    carry = lax.fori_loop(0, jnp.maximum(i - 1, 0), lambda j, c: step(j, far, c), carry)
    pen = jnp.where(i == 0, NEG, 0.0).astype(F32)
    carry = step(jnp.maximum(i - 1, 0), bias_ref[0, 1] + pen, carry)
    carry = step(i, bias_ref[0, 0], carry)

    lam_rows = lam_ref[...]
    e1 = jnp.exp(jnp.sum(lam_rows[0:1, :] * lam_rows[1:2, :], axis=-1, keepdims=True))
    e2 = jnp.exp(jnp.sum(lam_rows[2:3, :] * lam_rows[3:4, :], axis=-1, keepdims=True))
    lam = e1 - e2 + lam_init
    (_, l1, a1), (_, l2, a2) = carry
    o = a1 / l1 - lam * (a2 / l2)
    ms = jnp.mean(o * o, axis=-1, keepdims=True)
    o_ref[0] = (o * lax.rsqrt(ms + EPS) * subg_ref[...] * (1.0 - lam_init)).astype(o_ref.dtype)


def _mixer_b_call(proj3d, bias_b, far_b, lam_rows, subg, lam_init):
    b = proj3d.shape[0]
    base = 3 * WIDTH_A // LANES
    q_blk0, k_blk0, v_blk0 = base, base + WIDTH_B // LANES, base + 2 * WIDTH_B // LANES
    t = Q_TILE_B
    grid_spec = pltpu.PrefetchScalarGridSpec(
        num_scalar_prefetch=1,
        grid=(b, N_HEADS_B, SEQ // t),
        in_specs=[
            pl.BlockSpec((8, LANES), lambda bi, h, i, far: (0, 0)),
            pl.BlockSpec((1, t, LANES), lambda bi, h, i, far: (bi, i, q_blk0 + h)),
            pl.BlockSpec((1, SEQ, LANES), lambda bi, h, i, far: (bi, 0, k_blk0 + h)),
            pl.BlockSpec((1, SEQ, LANES), lambda bi, h, i, far: (bi, 0, v_blk0 + h)),
            pl.BlockSpec((1, 2, t, t), lambda bi, h, i, far: (h, 0, 0, 0)),
            pl.BlockSpec((1, LANES), lambda bi, h, i, far: (0, 0)),
        ],
        out_specs=pl.BlockSpec((1, t, LANES), lambda bi, h, i, far: (bi, i, h)),
    )
    return pl.pallas_call(
        functools.partial(_mixer_b_kernel, lam_init=lam_init),
        grid_spec=grid_spec,
        out_shape=jax.ShapeDtypeStruct((b, SEQ, WIDTH_B), BF16),
        compiler_params=_cparams(("parallel", "parallel", "arbitrary")),
        name="diff_attention",
    )(far_b, lam_rows, proj3d, proj3d, proj3d, bias_b, subg)


def _out_proj_kernel(oa_ref, ob_ref, w_ref, x_ref, g_ref, *rest, with_router):
    if with_router:
        wr_ref, x1_ref, h_ref, route_ref = rest
    else:
        x1_ref, h_ref = rest
    acc = jnp.dot(oa_ref[...], w_ref[:WIDTH_A, :], preferred_element_type=F32)
    acc = acc + jnp.dot(ob_ref[...], w_ref[WIDTH_A:, :], preferred_element_type=F32)
    x1 = x_ref[...] + acc
    x1_ref[...] = x1
    ms = jnp.mean(x1 * x1, axis=-1, keepdims=True)
    hn = x1 * lax.rsqrt(ms + EPS) * g_ref[...]
    h_ref[...] = hn.astype(h_ref.dtype)
    if with_router:
        hi = hn.astype(BF16)
        lo = (hn - hi.astype(F32)).astype(BF16)
        lg = (jnp.dot(hi, wr_ref[...], preferred_element_type=F32)
              + jnp.dot(lo, wr_ref[...], preferred_element_type=F32))
        lg = lg + pltpu.roll(lg, LANES - N_EXPERTS, 1)
        lane = lax.broadcasted_iota(jnp.int32, lg.shape, 1)
        lane_f = lane.astype(F32)
        lg = jnp.where(lane < N_EXPERTS, lg, -jnp.inf)
        v1 = jnp.max(lg, axis=-1, keepdims=True)
        i1 = jnp.min(jnp.where(lg == v1, lane_f, float(LANES)), axis=-1, keepdims=True)
        lg2 = jnp.where(lane_f == i1, -jnp.inf, lg)
        v2 = jnp.max(lg2, axis=-1, keepdims=True)
        i2 = jnp.min(jnp.where(lg2 == v2, lane_f, float(LANES)), axis=-1, keepdims=True)
        e = jnp.exp(v2 - v1)
        g1 = 1.0 / (1.0 + e)
        g2 = e / (1.0 + e)
        route_ref[...] = jnp.where(lane == 0, i1, jnp.where(lane == 1, i2,
                                   jnp.where(lane == 2, g1, jnp.where(lane == 3, g2, 0.0))))


def _out_proj_call(oa, ob, w_bf16, x2d, g, w_router_packed, h_dtype):
    n = x2d.shape[0]
    with_router = w_router_packed is not None
    row = lambda i: (i, 0)
    in_specs = [
        pl.BlockSpec((ROW_TILE, WIDTH_A), row),
        pl.BlockSpec((ROW_TILE, WIDTH_B), row),
        _resident((WIDTH_A + WIDTH_B, D_MODEL)),
        pl.BlockSpec((ROW_TILE, D_MODEL), row),
        _resident((1, D_MODEL)),
    ]
    out_specs = [pl.BlockSpec((ROW_TILE, D_MODEL), row), pl.BlockSpec((ROW_TILE, D_MODEL), row)]
    out_shape = [jax.ShapeDtypeStruct((n, D_MODEL), F32), jax.ShapeDtypeStruct((n, D_MODEL), h_dtype)]
    args = [oa, ob, w_bf16, x2d, g]
    if with_router:
        in_specs.append(_resident((D_MODEL, LANES)))
        out_specs.append(pl.BlockSpec((ROW_TILE, LANES), row))
        out_shape.append(jax.ShapeDtypeStruct((n, LANES), F32))
        args.append(w_router_packed)
    return pl.pallas_call(
        functools.partial(_out_proj_kernel, with_router=with_router),
        grid=(n // ROW_TILE,),
        in_specs=in_specs,
        out_specs=out_specs,
        out_shape=out_shape,
        compiler_params=_cparams(("parallel",)),
        name="out_proj_router" if with_router else "out_proj",
    )(*args)


def _swiglu_act(g, u):
    return (g / (1.0 + jnp.exp(-g))) * u


def _ffn_kernel(h_ref, x_ref, wg_ref, wu_ref, wd_ref, o_ref, a_ref):
    h = h_ref[...]
    c0 = 0
    for width in FFN_CHUNKS:
        g = jnp.dot(h, wg_ref[:, c0:c0 + width], preferred_element_type=F32)
        u = jnp.dot(h, wu_ref[:, c0:c0 + width], preferred_element_type=F32)
        a_ref[:, c0:c0 + width] = _swiglu_act(g, u).astype(BF16)
        c0 += width
    o_ref[...] = x_ref[...] + jnp.dot(a_ref[...], wd_ref[...], preferred_element_type=F32)


def _ffn_call(h, x2d, wg, wu, wd):
    n = x2d.shape[0]
    row = lambda i: (i, 0)
    return pl.pallas_call(
        _ffn_kernel,
        grid=(n // ROW_TILE,),
        in_specs=[
            pl.BlockSpec((ROW_TILE, D_MODEL), row),
            pl.BlockSpec((ROW_TILE, D_MODEL), row),
            _resident((D_MODEL, D_FF)),
            _resident((D_MODEL, D_FF)),
            _resident((D_FF, D_MODEL)),
        ],
        out_specs=pl.BlockSpec((ROW_TILE, D_MODEL), row),
        out_shape=jax.ShapeDtypeStruct((n, D_MODEL), F32),
        scratch_shapes=[pltpu.VMEM((ROW_TILE, D_FF), BF16)],
        compiler_params=_cparams(("parallel",)),
        name="dense_swiglu",
    )(h, x2d, wg, wu, wd)


def _row_copy(src_ref, src_row, dst_ref, dst_row, sem):
    return pltpu.make_async_copy(src_ref.at[pl.ds(src_row, 1)], dst_ref.at[pl.ds(dst_row, 1)], sem)


def _dispatch_kernel(fill_ref, dest_ref, h_ref, o_hbm, zero_ref, sem):
    i = pl.program_id(0)

    def start(t, c):
        for k in range(TOP_K):
            _row_copy(h_ref, t, o_hbm, dest_ref[0, 0, TOP_K * t + k], sem).start()
        return c

    def wait(t, c):
        for k in range(TOP_K):
            _row_copy(h_ref, t, o_hbm, 0, sem).wait()
        return c

    lax.fori_loop(0, DISPATCH_TILE, start, 0, unroll=4)
    lax.fori_loop(0, DISPATCH_TILE, wait, 0, unroll=4)

    @pl.when(i == pl.num_programs(0) - 1)
    def _():
        zero_ref[...] = jnp.zeros_like(zero_ref)
        for e in range(N_EXPERTS):
            lo, hi = fill_ref[e], fill_ref[N_EXPERTS + e]
            lax.fori_loop(lo, hi, lambda r, c: (_row_copy(zero_ref, 0, o_hbm, r, sem).start(), c)[1], 0)
            lax.fori_loop(lo, hi, lambda r, c: (_row_copy(zero_ref, 0, o_hbm, 0, sem).wait(), c)[1], 0)


def _dispatch_call(fill, dest3d, h, n_rows):
    n = h.shape[0]
    grid_spec = pltpu.PrefetchScalarGridSpec(
        num_scalar_prefetch=1,
        grid=(n // DISPATCH_TILE,),
        in_specs=[
            pl.BlockSpec((1, 1, TOP_K * DISPATCH_TILE), lambda i, f: (i, 0, 0), memory_space=pltpu.SMEM),
            pl.BlockSpec((DISPATCH_TILE, D_MODEL), lambda i, f: (i, 0)),
        ],
        out_specs=pl.BlockSpec(memory_space=pl.ANY),
        scratch_shapes=[pltpu.VMEM((8, D_MODEL), h.dtype), pltpu.SemaphoreType.DMA(())],
    )
    return pl.pallas_call(
        _dispatch_kernel,
        grid_spec=grid_spec,
        out_shape=jax.ShapeDtypeStruct((n_rows, D_MODEL), h.dtype),
        compiler_params=_cparams(("arbitrary",)),
        name="moe_dispatch",
    )(fill, dest3d, h)


def _moe_kernel(be_ref, nact_ref, x_ref, wg_ref, wu_ref, wd_ref, o_ref, xb_ref, acc_ref):
    b = pl.program_id(0)
    c = pl.program_id(1)
    last = pl.num_programs(1) - 1
    active = b < nact_ref[0]

    @pl.when(active)
    def _():
        @pl.when(c == 0)
        def _():
            xb_ref[...] = x_ref[...].astype(BF16)

        xb = xb_ref[...]
        g = jnp.dot(xb, wg_ref[0], preferred_element_type=F32)
        u = jnp.dot(xb, wu_ref[0], preferred_element_type=F32)
        contrib = jnp.dot(_swiglu_act(g, u).astype(BF16), wd_ref[0], preferred_element_type=F32)

        @pl.when(c == 0)
        def _():
            acc_ref[...] = contrib

        @pl.when(c > 0)
        def _():
            acc_ref[...] += contrib

        @pl.when(c == last)
        def _():
            o_ref[...] = acc_ref[...]

    @pl.when(jnp.logical_not(active) & (c == last))
    def _():
        o_ref[...] = jnp.zeros_like(o_ref)


def _moe_call(block_expert, n_active, x_pad, wg, wu, wd):
    n_rows = x_pad.shape[0]
    n_blocks = n_rows // MOE_BLOCK
    n_chunks = D_FF_EXPERT // MOE_FF_TILE
    grid_spec = pltpu.PrefetchScalarGridSpec(
        num_scalar_prefetch=2,
        grid=(n_blocks, n_chunks),
        in_specs=[
            pl.BlockSpec((MOE_BLOCK, D_MODEL), lambda b, c, be, na: (b, 0)),
            pl.BlockSpec((1, D_MODEL, MOE_FF_TILE), lambda b, c, be, na: (be[b], 0, c)),
            pl.BlockSpec((1, D_MODEL, MOE_FF_TILE), lambda b, c, be, na: (be[b], 0, c)),
            pl.BlockSpec((1, MOE_FF_TILE, D_MODEL), lambda b, c, be, na: (be[b], c, 0)),
        ],
        out_specs=pl.BlockSpec((MOE_BLOCK, D_MODEL), lambda b, c, be, na: (b, 0)),
        scratch_shapes=[pltpu.VMEM((MOE_BLOCK, D_MODEL), BF16), pltpu.VMEM((MOE_BLOCK, D_MODEL), F32)],
    )
    return pl.pallas_call(
        _moe_kernel,
        grid_spec=grid_spec,
        out_shape=jax.ShapeDtypeStruct((n_rows, D_MODEL), F32),
        compiler_params=_cparams(("arbitrary", "arbitrary")),
        name="moe_swiglu",
    )(block_expert, n_active, x_pad, wg, wu, wd)


def _combine_kernel(pos_ref, y_hbm, x_ref, route_ref, o_ref, buf, sem):
    def start(t, c):
        for k in range(TOP_K):
            _row_copy(y_hbm, pos_ref[0, 0, TOP_K * t + k], buf.at[k], t, sem).start()
        return c

    def wait(t, c):
        for k in range(TOP_K):
            _row_copy(y_hbm, 0, buf.at[k], t, sem).wait()
        return c

    lax.fori_loop(0, COMBINE_TILE, start, 0, unroll=4)
    lax.fori_loop(0, COMBINE_TILE, wait, 0, unroll=4)
    route = route_ref[...]
    g0 = route[:, TOP_K:TOP_K + 1]
    g1 = route[:, TOP_K + 1:TOP_K + 2]
    o_ref[...] = x_ref[...] + g0 * buf[0] + g1 * buf[1]


def _combine_call(pos3d, y_pad, x2d, route):
    n = x2d.shape[0]
    row = lambda i: (i, 0)
    return pl.pallas_call(
        _combine_kernel,
        grid=(n // COMBINE_TILE,),
        in_specs=[
            pl.BlockSpec((1, 1, TOP_K * COMBINE_TILE), lambda i: (i, 0, 0), memory_space=pltpu.SMEM),
            pl.BlockSpec(memory_space=pl.ANY),
            pl.BlockSpec((COMBINE_TILE, D_MODEL), row),
            pl.BlockSpec((COMBINE_TILE, LANES), row),
        ],
        out_specs=pl.BlockSpec((COMBINE_TILE, D_MODEL), row),
        out_shape=jax.ShapeDtypeStruct((n, D_MODEL), F32),
        scratch_shapes=[pltpu.VMEM((TOP_K, COMBINE_TILE, D_MODEL), F32), pltpu.SemaphoreType.DMA(())],
        compiler_params=_cparams(("arbitrary",)),
        name="moe_combine",
    )(pos3d, y_pad, x2d, route)


def _moe_layer(x1, hn, route, wg, wu, wd):
    n = x1.shape[0]
    e_flat = route[:, :TOP_K].astype(jnp.int32).reshape(-1)
    onehot = (e_flat[:, None] == jnp.arange(N_EXPERTS, dtype=jnp.int32)[None, :]).astype(jnp.int32)
    csum = jnp.cumsum(onehot, axis=0)
    counts = csum[-1]
    padded = (counts + MOE_BLOCK - 1) // MOE_BLOCK * MOE_BLOCK
    pad_ends = jnp.cumsum(padded)
    pad_starts = pad_ends - padded
    dest = jnp.sum((csum - onehot + pad_starts[None, :]) * onehot, axis=1)
    n_blocks = -(-(n * TOP_K) // MOE_BLOCK) + N_EXPERTS
    n_rows = n_blocks * MOE_BLOCK
    block_start = jnp.arange(n_blocks, dtype=jnp.int32) * MOE_BLOCK
    block_expert = jnp.minimum(
        jnp.sum((pad_ends[None, :] <= block_start[:, None]).astype(jnp.int32), axis=1), N_EXPERTS - 1)
    n_active = (pad_ends[-1:] // MOE_BLOCK).astype(jnp.int32)
    fill = jnp.concatenate([pad_starts + counts, pad_starts[1:], jnp.full((1,), n_rows, jnp.int32)])

    x_pad = _dispatch_call(fill.astype(jnp.int32), dest.reshape(n // DISPATCH_TILE, 1, TOP_K * DISPATCH_TILE),
                           hn, n_rows)
    y_pad = _moe_call(block_expert.astype(jnp.int32), n_active, x_pad, wg, wu, wd)
    pos3d = dest.reshape(n // COMBINE_TILE, 1, TOP_K * COMBINE_TILE)
    return _combine_call(pos3d, y_pad, x1, route)


def _head_group_matrix():
    idx = np.arange(MXU_DIM) // HEAD_DIM
    return jnp.asarray((idx[:, None] == idx[None, :]).astype(np.float32), dtype=BF16)


def _pack_router(w_router):
    hi = w_router.astype(BF16)
    lo = (w_router - hi.astype(F32)).astype(BF16)
    pad = jnp.zeros((D_MODEL, LANES - 2 * N_EXPERTS), BF16)
    return jnp.concatenate([hi, lo, pad], axis=1)


def kernel(x, norm_mix_g, norm_ffn_g, w_in, w_out, q_norm_a, k_norm_a, q_norm_b, k_norm_b,
           lambda_q1, lambda_k1, lambda_q2, lambda_k2, subln_g, rel_bias,
           w_gate_dense, w_up_dense, w_down_dense, w_router, w_gate_moe, w_up_moe, w_down_moe):
    b, s, d = x.shape
    assert (s, d) == (SEQ, D_MODEL)
    depth = w_in.shape[0]
    n = b * s
    x2d = x.reshape(n, d)
    gmat = _head_group_matrix()
    bias_a = _dilated_bias_tables(rel_bias)
    bias_b, far_b = _diff_bias_tables(rel_bias)
    ones = jnp.ones((PROJ_COL_TILE,), F32)
    rep = PROJ_COL_TILE // HEAD_DIM

    for layer in range(depth):
        gains = jnp.stack([
            jnp.tile(q_norm_a[layer], rep) * ATTN_SCALE, jnp.tile(k_norm_a[layer], rep), ones,
            jnp.tile(q_norm_b[layer], rep) * ATTN_SCALE, jnp.tile(k_norm_b[layer], rep), ones,
        ]).reshape(-1, 1, PROJ_COL_TILE)
        proj = _proj_call(x2d, norm_mix_g[layer].reshape(1, d), w_in[layer].astype(BF16), gains, gmat)
        proj3d = proj.reshape(b, s, PROJ_WIDTH)

        oa = _mixer_a_call(proj3d, bias_a)
        lam_init = 0.8 - 0.6 * math.exp(-0.3 * layer)
        lam_rows = jnp.zeros((8, LANES), F32).at[:4, :HEAD_DIM].set(
            jnp.stack([lambda_q1[layer], lambda_k1[layer], lambda_q2[layer], lambda_k2[layer]]))
        ob = _mixer_b_call(proj3d, bias_b, far_b, lam_rows, subln_g[layer].reshape(1, LANES), lam_init)

        is_moe = layer % 2 == 1
        i = layer // 2
        outs = _out_proj_call(
            oa.reshape(n, WIDTH_A), ob.reshape(n, WIDTH_B), w_out[layer].astype(BF16), x2d,
            norm_ffn_g[layer].reshape(1, d), _pack_router(w_router[i]) if is_moe else None,
            F32 if is_moe else BF16)
        if is_moe:
            x1, hn, route = outs
            x2d = _moe_layer(x1, hn, route, w_gate_moe[i].astype(BF16), w_up_moe[i].astype(BF16),
                             w_down_moe[i].astype(BF16))
        else:
            x1, hn = outs
            x2d = _ffn_call(hn, x1, w_gate_dense[i].astype(BF16), w_up_dense[i].astype(BF16),
                            w_down_dense[i].astype(BF16))
    return x2d.reshape(b, s, d)
```

```python
import functools
import math

import numpy as np
import jax
import jax.numpy as jnp
from jax import lax
from jax.experimental import pallas as pl
from jax.experimental.pallas import tpu as pltpu

F32 = jnp.float32
BF16 = jnp.bfloat16

D_MODEL = 1024
SEQ = 2048
HEAD_DIM = 64
ATTN_SCALE = HEAD_DIM ** -0.5
N_HEADS_A = 8
DIL_PATTERNS = ((128, 1), (512, 4), (2048, 16))
DIL_BLOCK = 128
N_HEADS_B = 4
WIDTH_A = N_HEADS_A * HEAD_DIM
WIDTH_B = N_HEADS_B * 2 * HEAD_DIM
PROJ_WIDTH = 3 * (WIDTH_A + WIDTH_B)
N_BUCKETS = 32
MAX_DISTANCE = 128
D_FF = 2816
N_EXPERTS = 8
TOP_K = 2
D_FF_EXPERT = 3584
EPS = 1e-6
NEG = -1e30

LANES = 128
MXU_DIM = 256
VMEM_LIMIT = 56 * 1024 * 1024

ROW_TILE = 512
PROJ_COL_TILE = 512
Q_TILE_B = 512
DIL_GROUP = 16
MOE_BLOCK = 512
MOE_FF_TILE = 512
FFN_CHUNKS = (512, 512, 512, 512, 512, 256)
DISPATCH_TILE = 256
COMBINE_TILE = 256


def _cparams(sem):
    return pltpu.CompilerParams(dimension_semantics=sem, vmem_limit_bytes=VMEM_LIMIT)


def _resident(shape):
    nd = len(shape)
    return pl.BlockSpec(shape, lambda *_: (0,) * nd, pipeline_mode=pl.Buffered(1))


def _t5_bucket_np(dist):
    n = np.maximum(dist, 0)
    max_exact = N_BUCKETS // 2
    large = max_exact + (np.log(np.maximum(n, 1).astype(np.float32) / max_exact)
                         / math.log(MAX_DISTANCE / max_exact) * (N_BUCKETS - max_exact)).astype(np.int32)
    large = np.minimum(large, N_BUCKETS - 1)
    return np.where(n < max_exact, n, large).astype(np.int32)


def _dilated_buckets():
    i = np.arange(DIL_BLOCK)[:, None]
    j = np.arange(2 * DIL_BLOCK)[None, :]
    steps = DIL_BLOCK + i - j
    valid = (steps >= 0) & (steps <= DIL_BLOCK)
    tabs = []
    for _, dil in DIL_PATTERNS:
        bucket = np.where(valid, _t5_bucket_np(steps * dil), -1)
        tabs.append(bucket)
        tabs.append(np.where(j >= DIL_BLOCK, bucket, -1))
    return np.stack(tabs).astype(np.int32)


def _diff_buckets():
    t = Q_TILE_B
    qi = np.arange(t)[:, None]
    kj = np.arange(t)[None, :]
    tabs = []
    for off in (0, t):
        dist = qi - kj + off
        tabs.append(np.where(dist >= 0, _t5_bucket_np(dist), -1))
    return np.stack(tabs).astype(np.int32)


def _far_bucket():
    b = _t5_bucket_np(np.arange(Q_TILE_B + 1, SEQ + 1))
    assert (b == b[0]).all()
    return int(b[0])


def _bias_table_kernel(rel_ref, bucket_ref, o_ref, *, head0):
    col = head0 + pl.program_id(1)
    bucket = bucket_ref[0]
    val = jnp.full(bucket.shape, NEG, F32)
    for b in range(N_BUCKETS):
        val = jnp.where(bucket == b, rel_ref[b, col], val)
    o_ref[0, 0] = val


def _bias_tables(rel_bias, buckets, head0, n_heads):
    n_tab, rows, cols = buckets.shape
    return pl.pallas_call(
        functools.partial(_bias_table_kernel, head0=head0),
        grid=(n_tab, n_heads),
        in_specs=[
            pl.BlockSpec(memory_space=pltpu.SMEM),
            pl.BlockSpec((1, rows, cols), lambda t, h: (t, 0, 0)),
        ],
        out_specs=pl.BlockSpec((1, 1, rows, cols), lambda t, h: (t, h, 0, 0)),
        out_shape=jax.ShapeDtypeStruct((n_tab, n_heads, rows, cols), F32),
        compiler_params=_cparams(("parallel", "parallel")),
        name="bias_tables",
    )(rel_bias, jnp.asarray(buckets))


def _proj_kernel(x_ref, g_ref, w_ref, gain_ref, gmat_ref, o_ref):
    x = x_ref[...]
    ms = jnp.mean(x * x, axis=-1, keepdims=True)
    xn = (x * lax.rsqrt(ms + EPS) * g_ref[...]).astype(BF16)
    for j in range(PROJ_WIDTH // PROJ_COL_TILE):
        cols = slice(j * PROJ_COL_TILE, (j + 1) * PROJ_COL_TILE)
        acc = jnp.dot(xn, w_ref[:, cols], preferred_element_type=F32)
        if j % 3 == 2:
            o_ref[:, cols] = acc.astype(BF16)
        else:
            sq = (acc * acc).astype(BF16)
            ss = jnp.concatenate(
                [jnp.dot(sq[:, c:c + MXU_DIM], gmat_ref[...], preferred_element_type=F32)
                 for c in range(0, PROJ_COL_TILE, MXU_DIM)], axis=1)
            o_ref[:, cols] = (acc * lax.rsqrt(ss * (1.0 / HEAD_DIM) + EPS) * gain_ref[:, cols]).astype(BF16)


def _proj_call(x2d, g, w_bf16, gains, gmat):
    n = x2d.shape[0]
    return pl.pallas_call(
        _proj_kernel,
        grid=(n // ROW_TILE,),
        in_specs=[
            pl.BlockSpec((ROW_TILE, D_MODEL), lambda i: (i, 0)),
            _resident((1, D_MODEL)),
            _resident((D_MODEL, PROJ_WIDTH)),
            _resident((1, PROJ_WIDTH)),
            _resident((MXU_DIM, MXU_DIM)),
        ],
        out_specs=pl.BlockSpec((ROW_TILE, PROJ_WIDTH), lambda i: (i, 0)),
        out_shape=jax.ShapeDtypeStruct((n, PROJ_WIDTH), BF16),
        compiler_params=_cparams(("parallel",)),
        name="norm_in_proj",
    )(x2d, g, w_bf16, gains, gmat)


def _deinterleave(x, dil):
    if dil == 1:
        return x
    return jnp.swapaxes(x.reshape(SEQ // dil, dil, LANES), 0, 1).reshape(SEQ, LANES)


def _interleave(x, dil):
    if dil == 1:
        return x
    return jnp.swapaxes(x.reshape(dil, SEQ // dil, LANES), 0, 1).reshape(SEQ, LANES)


def _mixer_a_kernel(q_ref, k_ref, v_ref, bias_ref, o_ref, qp, kp, vp, out_s, lse_s):
    n_tiles = SEQ // DIL_BLOCK
    n_branch = len(DIL_PATTERNS)
    lane_full = lax.broadcasted_iota(jnp.int32, (SEQ, LANES), 1)
    lane = lax.broadcasted_iota(jnp.int32, (DIL_BLOCK, LANES), 1)
    head0 = lane < HEAD_DIM

    q = q_ref[0].astype(F32)
    k = k_ref[0].astype(F32)
    v = v_ref[0].astype(F32)
    pad = jnp.zeros((DIL_BLOCK, LANES), BF16)
    for bi, (_, dil) in enumerate(DIL_PATTERNS):
        qd = _deinterleave(q, dil)
        qp[bi, 0] = jnp.where(lane_full < HEAD_DIM, qd, 0.0).astype(BF16)
        qp[bi, 1] = jnp.where(lane_full >= HEAD_DIM, qd, 0.0).astype(BF16)
        kp[bi, :DIL_BLOCK, :] = pad
        vp[bi, :DIL_BLOCK, :] = pad
        kp[bi, DIL_BLOCK:, :] = _deinterleave(k, dil).astype(BF16)
        vp[bi, DIL_BLOCK:, :] = _deinterleave(v, dil).astype(BF16)

    for bi, (_, dil) in enumerate(DIL_PATTERNS):
        nb = SEQ // dil // DIL_BLOCK

        def tile(t, bi=bi, nb=nb):
            row0 = t * DIL_BLOCK if isinstance(t, int) else pl.multiple_of(t * DIL_BLOCK, DIL_BLOCK)
            rows = pl.ds(row0, DIL_BLOCK)
            keys = pl.ds(row0, 2 * DIL_BLOCK)
            tab = 2 * bi + (t % nb == 0)
            kt = kp[bi, keys, :]
            vt = vp[bi, keys, :]
            m_h, l_h, pv_h = [], [], []
            for hh in range(2):
                s = lax.dot_general(qp[bi, hh, rows, :], kt, (((1,), (1,)), ((), ())),
                                    preferred_element_type=F32) + bias_ref[tab, hh]
                m = jnp.broadcast_to(jnp.max(s, axis=-1, keepdims=True), (DIL_BLOCK, LANES))
                p = jnp.exp(s - jnp.concatenate([m, m], axis=1))
                m_h.append(m)
                l_h.append(jnp.broadcast_to(jnp.sum(p, axis=-1, keepdims=True), (DIL_BLOCK, LANES)))
                pv_h.append(jnp.dot(p.astype(BF16), vt, preferred_element_type=F32))
            l = jnp.where(head0, l_h[0], l_h[1])
            out_s[bi, rows, :] = jnp.where(head0, pv_h[0], pv_h[1]) / l
            lse_s[bi, rows, :] = jnp.where(head0, m_h[0], m_h[1]) + jnp.log(l)

        if DIL_GROUP == n_tiles:
            for t in range(n_tiles):
                tile(t)
        else:
            def group(g, carry, tile=tile):
                for u in range(DIL_GROUP):
                    tile(g * DIL_GROUP + u)
                return carry

            lax.fori_loop(0, n_tiles // DIL_GROUP, group, 0)

    lse_b = [_interleave(lse_s[bi], dil) for bi, (_, dil) in enumerate(DIL_PATTERNS)]
    lse_max = functools.reduce(jnp.maximum, lse_b)
    w_b = [jnp.exp(lse - lse_max) for lse in lse_b]
    num = sum(w * _interleave(out_s[bi], dil) for w, (bi, (_, dil)) in zip(w_b, enumerate(DIL_PATTERNS)))
    o_ref[0] = (num / sum(w_b)).astype(o_ref.dtype)


def _mixer_a_call(proj3d, bias_a):
    b = proj3d.shape[0]
    n_pairs = N_HEADS_A // 2
    n_branch = len(DIL_PATTERNS)
    q_blk0, k_blk0, v_blk0 = 0, WIDTH_A // LANES, 2 * WIDTH_A // LANES
    return pl.pallas_call(
        _mixer_a_kernel,
        grid=(b, n_pairs),
        in_specs=[
            pl.BlockSpec((1, SEQ, LANES), lambda i, p: (i, 0, q_blk0 + p)),
            pl.BlockSpec((1, SEQ, LANES), lambda i, p: (i, 0, k_blk0 + p)),
            pl.BlockSpec((1, SEQ, LANES), lambda i, p: (i, 0, v_blk0 + p)),
            pl.BlockSpec((bias_a.shape[0], 2, DIL_BLOCK, 2 * DIL_BLOCK), lambda i, p: (0, p, 0, 0)),
        ],
        out_specs=pl.BlockSpec((1, SEQ, LANES), lambda i, p: (i, 0, p)),
        out_shape=jax.ShapeDtypeStruct((b, SEQ, WIDTH_A), BF16),
        scratch_shapes=[
            pltpu.VMEM((n_branch, 2, SEQ, LANES), BF16),
            pltpu.VMEM((n_branch, SEQ + DIL_BLOCK, LANES), BF16),
            pltpu.VMEM((n_branch, SEQ + DIL_BLOCK, LANES), BF16),
            pltpu.VMEM((n_branch, SEQ, LANES), F32),
            pltpu.VMEM((n_branch, SEQ, LANES), F32),
        ],
        compiler_params=_cparams(("parallel", "parallel")),
        name="dilated_attention",
    )(proj3d, proj3d, proj3d, bias_a)


def _mixer_b_kernel(far_ref, lam_ref, q_ref, k_ref, v_ref, bias_ref, subg_ref, o_ref, m_s, l_s, acc_s,
                    *, lam_init):
    pair = pl.program_id(1)
    i = pl.program_id(2)
    t = Q_TILE_B
    reps = t // LANES
    lane = lax.broadcasted_iota(jnp.int32, (t, LANES), 1)
    q_maps = []
    for hd in range(2):
        q = q_ref[0, :, hd * LANES:(hd + 1) * LANES]
        zero = jnp.zeros_like(q)
        q_maps.append((jnp.where(lane < HEAD_DIM, q, zero), jnp.where(lane >= HEAD_DIM, q, zero)))

    m_s[...] = jnp.full(m_s.shape, NEG, F32)
    l_s[...] = jnp.zeros(l_s.shape, F32)
    acc_s[...] = jnp.zeros(acc_s.shape, F32)

    def step(j, bias_of_head):
        koff = pl.multiple_of(j * t, t)
        for hd in range(2):
            kt = k_ref[0, pl.ds(koff, t), hd * LANES:(hd + 1) * LANES]
            vt = v_ref[0, pl.ds(koff, t), hd * LANES:(hd + 1) * LANES]
            bias = bias_of_head(hd)
            for half in range(2):
                c = 2 * hd + half
                s = lax.dot_general(q_maps[hd][half], kt, (((1,), (1,)), ((), ())),
                                    preferred_element_type=F32) + bias
                m_old = m_s[c]
                m_new = jnp.maximum(m_old, jnp.max(s, axis=-1, keepdims=True))
                p = jnp.exp(s - jnp.concatenate([m_new] * reps, axis=1))
                alpha = jnp.exp(m_old - m_new)
                l_s[c] = l_s[c] * alpha + jnp.sum(p, axis=-1, keepdims=True)
                acc_s[c] = acc_s[c] * alpha + jnp.dot(p.astype(BF16), vt, preferred_element_type=F32)
                m_s[c] = m_new

    def far_step(j, carry):
        step(j, lambda hd: far_ref[2 * pair + hd])
        return carry

    lax.fori_loop(0, jnp.maximum(i - 1, 0), far_step, 0)

    @pl.when(i > 0)
    def _():
        step(i - 1, lambda hd: bias_ref[1, hd])

    step(i, lambda hd: bias_ref[0, hd])

    lam_rows = lam_ref[...]
    e1 = jnp.exp(jnp.sum(lam_rows[0:1, :] * lam_rows[1:2, :], axis=-1, keepdims=True))
    e2 = jnp.exp(jnp.sum(lam_rows[2:3, :] * lam_rows[3:4, :], axis=-1, keepdims=True))
    lam = e1 - e2 + lam_init
    for hd in range(2):
        o = acc_s[2 * hd] / l_s[2 * hd] - lam * (acc_s[2 * hd + 1] / l_s[2 * hd + 1])
        ms = jnp.mean(o * o, axis=-1, keepdims=True)
        o_ref[0, :, hd * LANES:(hd + 1) * LANES] = (
            o * lax.rsqrt(ms + EPS) * subg_ref[...] * (1.0 - lam_init)).astype(o_ref.dtype)


def _mixer_b_call(proj3d, bias_b, far_b, lam_rows, subg, lam_init):
    b = proj3d.shape[0]
    width = 2 * LANES
    base = 3 * WIDTH_A // width
    q_blk0, k_blk0, v_blk0 = base, base + WIDTH_B // width, base + 2 * WIDTH_B // width
    t = Q_TILE_B
    grid_spec = pltpu.PrefetchScalarGridSpec(
        num_scalar_prefetch=1,
        grid=(b, N_HEADS_B // 2, SEQ // t),
        in_specs=[
            pl.BlockSpec((8, LANES), lambda bi, p, i, far: (0, 0)),
            pl.BlockSpec((1, t, width), lambda bi, p, i, far: (bi, i, q_blk0 + p)),
            pl.BlockSpec((1, SEQ, width), lambda bi, p, i, far: (bi, 0, k_blk0 + p)),
            pl.BlockSpec((1, SEQ, width), lambda bi, p, i, far: (bi, 0, v_blk0 + p)),
            pl.BlockSpec((2, 2, t, t), lambda bi, p, i, far: (0, p, 0, 0)),
            pl.BlockSpec((1, LANES), lambda bi, p, i, far: (0, 0)),
        ],
        out_specs=pl.BlockSpec((1, t, width), lambda bi, p, i, far: (bi, i, p)),
        scratch_shapes=[pltpu.VMEM((4, t, LANES), F32), pltpu.VMEM((4, t, LANES), F32),
                        pltpu.VMEM((4, t, LANES), F32)],
    )
    return pl.pallas_call(
        functools.partial(_mixer_b_kernel, lam_init=lam_init),
        grid_spec=grid_spec,
        out_shape=jax.ShapeDtypeStruct((b, SEQ, WIDTH_B), BF16),
        compiler_params=_cparams(("parallel", "parallel", "arbitrary")),
        name="diff_attention",
    )(far_b, lam_rows, proj3d, proj3d, proj3d, bias_b, subg)


def _out_proj_kernel(oa_ref, ob_ref, w_ref, x_ref, g_ref, *rest, with_router):
    if with_router:
        wr_ref, x1_ref, h_ref, route_ref = rest
    else:
        x1_ref, h_ref = rest
    acc = jnp.dot(oa_ref[...], w_ref[:WIDTH_A, :], preferred_element_type=F32)
    acc = acc + jnp.dot(ob_ref[...], w_ref[WIDTH_A:, :], preferred_element_type=F32)
    x1 = x_ref[...] + acc
    x1_ref[...] = x1
    ms = jnp.mean(x1 * x1, axis=-1, keepdims=True)
    hn = x1 * lax.rsqrt(ms + EPS) * g_ref[...]
    h_ref[...] = hn.astype(h_ref.dtype)
    if with_router:
        hi = hn.astype(BF16)
        lo = (hn - hi.astype(F32)).astype(BF16)
        lg = (jnp.dot(hi, wr_ref[...], preferred_element_type=F32)
              + jnp.dot(lo, wr_ref[...], preferred_element_type=F32))
        lg = lg + pltpu.roll(lg, LANES - N_EXPERTS, 1)
        lane = lax.broadcasted_iota(jnp.int32, lg.shape, 1)
        lane_f = lane.astype(F32)
        lg = jnp.where(lane < N_EXPERTS, lg, -jnp.inf)
        v1 = jnp.max(lg, axis=-1, keepdims=True)
        i1 = jnp.min(jnp.where(lg == v1, lane_f, float(LANES)), axis=-1, keepdims=True)
        lg2 = jnp.where(lane_f == i1, -jnp.inf, lg)
        v2 = jnp.max(lg2, axis=-1, keepdims=True)
        i2 = jnp.min(jnp.where(lg2 == v2, lane_f, float(LANES)), axis=-1, keepdims=True)
        e = jnp.exp(v2 - v1)
        g1 = 1.0 / (1.0 + e)
        g2 = e / (1.0 + e)
        route_ref[...] = jnp.where(lane == 0, i1, jnp.where(lane == 1, i2,
                                   jnp.where(lane == 2, g1, jnp.where(lane == 3, g2, 0.0))))


def _out_proj_call(oa, ob, w_bf16, x2d, g, w_router_packed, h_dtype):
    n = x2d.shape[0]
    with_router = w_router_packed is not None
    row = lambda i: (i, 0)
    in_specs = [
        pl.BlockSpec((ROW_TILE, WIDTH_A), row),
        pl.BlockSpec((ROW_TILE, WIDTH_B), row),
        _resident((WIDTH_A + WIDTH_B, D_MODEL)),
        pl.BlockSpec((ROW_TILE, D_MODEL), row),
        _resident((1, D_MODEL)),
    ]
    out_specs = [pl.BlockSpec((ROW_TILE, D_MODEL), row), pl.BlockSpec((ROW_TILE, D_MODEL), row)]
    out_shape = [jax.ShapeDtypeStruct((n, D_MODEL), F32), jax.ShapeDtypeStruct((n, D_MODEL), h_dtype)]
    args = [oa, ob, w_bf16, x2d, g]
    if with_router:
        in_specs.append(_resident((D_MODEL, LANES)))
        out_specs.append(pl.BlockSpec((ROW_TILE, LANES), row))
        out_shape.append(jax.ShapeDtypeStruct((n, LANES), F32))
        args.append(w_router_packed)
    return pl.pallas_call(
        functools.partial(_out_proj_kernel, with_router=with_router),
        grid=(n // ROW_TILE,),
        in_specs=in_specs,
        out_specs=out_specs,
        out_shape=out_shape,
        compiler_params=_cparams(("parallel",)),
        name="out_proj_router" if with_router else "out_proj",
    )(*args)


def _swiglu_act(g, u):
    return (g / (1.0 + jnp.exp(-g))) * u


def _ffn_kernel(h_ref, x_ref, wg_ref, wu_ref, wd_ref, o_ref, a_ref):
    h = h_ref[...]
    c0 = 0
    for width in FFN_CHUNKS:
        g = jnp.dot(h, wg_ref[:, c0:c0 + width], preferred_element_type=F32)
        u = jnp.dot(h, wu_ref[:, c0:c0 + width], preferred_element_type=F32)
        a_ref[:, c0:c0 + width] = _swiglu_act(g, u).astype(BF16)
        c0 += width
    o_ref[...] = x_ref[...] + jnp.dot(a_ref[...], wd_ref[...], preferred_element_type=F32)


def _ffn_call(h, x2d, wg, wu, wd):
    n = x2d.shape[0]
    row = lambda i: (i, 0)
    return pl.pallas_call(
        _ffn_kernel,
        grid=(n // ROW_TILE,),
        in_specs=[
            pl.BlockSpec((ROW_TILE, D_MODEL), row),
            pl.BlockSpec((ROW_TILE, D_MODEL), row),
            _resident((D_MODEL, D_FF)),
            _resident((D_MODEL, D_FF)),
            _resident((D_FF, D_MODEL)),
        ],
        out_specs=pl.BlockSpec((ROW_TILE, D_MODEL), row),
        out_shape=jax.ShapeDtypeStruct((n, D_MODEL), F32),
        scratch_shapes=[pltpu.VMEM((ROW_TILE, D_FF), BF16)],
        compiler_params=_cparams(("parallel",)),
        name="dense_swiglu",
    )(h, x2d, wg, wu, wd)


def _row_copy(src_ref, src_row, dst_ref, dst_row, sem):
    return pltpu.make_async_copy(src_ref.at[pl.ds(src_row, 1)], dst_ref.at[pl.ds(dst_row, 1)], sem)


def _dispatch_kernel(fill_ref, dest_ref, h_ref, o_hbm, zero_ref, sem):
    i = pl.program_id(0)

    def start(t, c):
        for k in range(TOP_K):
            _row_copy(h_ref, t, o_hbm, dest_ref[0, 0, TOP_K * t + k], sem).start()
        return c

    def wait(t, c):
        for k in range(TOP_K):
            _row_copy(h_ref, t, o_hbm, 0, sem).wait()
        return c

    lax.fori_loop(0, DISPATCH_TILE, start, 0, unroll=4)
    lax.fori_loop(0, DISPATCH_TILE, wait, 0, unroll=4)

    @pl.when(i == pl.num_programs(0) - 1)
    def _():
        zero_ref[...] = jnp.zeros_like(zero_ref)
        for e in range(N_EXPERTS):
            lo, hi = fill_ref[e], fill_ref[N_EXPERTS + e]
            lax.fori_loop(lo, hi, lambda r, c: (_row_copy(zero_ref, 0, o_hbm, r, sem).start(), c)[1], 0)
            lax.fori_loop(lo, hi, lambda r, c: (_row_copy(zero_ref, 0, o_hbm, 0, sem).wait(), c)[1], 0)


def _dispatch_call(fill, dest3d, h, n_rows):
    n = h.shape[0]
    grid_spec = pltpu.PrefetchScalarGridSpec(
        num_scalar_prefetch=1,
        grid=(n // DISPATCH_TILE,),
        in_specs=[
            pl.BlockSpec((1, 1, TOP_K * DISPATCH_TILE), lambda i, f: (i, 0, 0), memory_space=pltpu.SMEM),
            pl.BlockSpec((DISPATCH_TILE, D_MODEL), lambda i, f: (i, 0)),
        ],
        out_specs=pl.BlockSpec(memory_space=pl.ANY),
        scratch_shapes=[pltpu.VMEM((8, D_MODEL), h.dtype), pltpu.SemaphoreType.DMA(())],
    )
    return pl.pallas_call(
        _dispatch_kernel,
        grid_spec=grid_spec,
        out_shape=jax.ShapeDtypeStruct((n_rows, D_MODEL), h.dtype),
        compiler_params=_cparams(("arbitrary",)),
        name="moe_dispatch",
    )(fill, dest3d, h)


def _moe_kernel(be_ref, nact_ref, x_ref, wg_ref, wu_ref, wd_ref, o_ref, xb_ref, acc_ref):
    b = pl.program_id(0)
    c = pl.program_id(1)
    last = pl.num_programs(1) - 1
    active = b < nact_ref[0]

    @pl.when(active)
    def _():
        @pl.when(c == 0)
        def _():
            xb_ref[...] = x_ref[...].astype(BF16)

        xb = xb_ref[...]
        g = jnp.dot(xb, wg_ref[0], preferred_element_type=F32)
        u = jnp.dot(xb, wu_ref[0], preferred_element_type=F32)
        contrib = jnp.dot(_swiglu_act(g, u).astype(BF16), wd_ref[0], preferred_element_type=F32)

        @pl.when(c == 0)
        def _():
            acc_ref[...] = contrib

        @pl.when(c > 0)
        def _():
            acc_ref[...] += contrib

        @pl.when(c == last)
        def _():
            o_ref[...] = acc_ref[...]

    @pl.when(jnp.logical_not(active) & (c == last))
    def _():
        o_ref[...] = jnp.zeros_like(o_ref)


def _moe_call(block_expert, n_active, x_pad, wg, wu, wd):
    n_rows = x_pad.shape[0]
    n_blocks = n_rows // MOE_BLOCK
    n_chunks = D_FF_EXPERT // MOE_FF_TILE
    grid_spec = pltpu.PrefetchScalarGridSpec(
        num_scalar_prefetch=2,
        grid=(n_blocks, n_chunks),
        in_specs=[
            pl.BlockSpec((MOE_BLOCK, D_MODEL), lambda b, c, be, na: (b, 0)),
            pl.BlockSpec((1, D_MODEL, MOE_FF_TILE), lambda b, c, be, na: (be[b], 0, c)),
            pl.BlockSpec((1, D_MODEL, MOE_FF_TILE), lambda b, c, be, na: (be[b], 0, c)),
            pl.BlockSpec((1, MOE_FF_TILE, D_MODEL), lambda b, c, be, na: (be[b], c, 0)),
        ],
        out_specs=pl.BlockSpec((MOE_BLOCK, D_MODEL), lambda b, c, be, na: (b, 0)),
        scratch_shapes=[pltpu.VMEM((MOE_BLOCK, D_MODEL), BF16), pltpu.VMEM((MOE_BLOCK, D_MODEL), F32)],
    )
    return pl.pallas_call(
        _moe_kernel,
        grid_spec=grid_spec,
        out_shape=jax.ShapeDtypeStruct((n_rows, D_MODEL), F32),
        compiler_params=_cparams(("arbitrary", "arbitrary")),
        name="moe_swiglu",
    )(block_expert, n_active, x_pad, wg, wu, wd)


def _combine_kernel(pos_ref, y_hbm, x_ref, route_ref, o_ref, buf, sem):
    def start(t, c):
        for k in range(TOP_K):
            _row_copy(y_hbm, pos_ref[0, 0, TOP_K * t + k], buf.at[k], t, sem).start()
        return c

    def wait(t, c):
        for k in range(TOP_K):
            _row_copy(y_hbm, 0, buf.at[k], t, sem).wait()
        return c

    lax.fori_loop(0, COMBINE_TILE, start, 0, unroll=4)
    lax.fori_loop(0, COMBINE_TILE, wait, 0, unroll=4)
    route = route_ref[...]
    g0 = route[:, TOP_K:TOP_K + 1]
    g1 = route[:, TOP_K + 1:TOP_K + 2]
    o_ref[...] = x_ref[...] + g0 * buf[0] + g1 * buf[1]


def _combine_call(pos3d, y_pad, x2d, route):
    n = x2d.shape[0]
    row = lambda i: (i, 0)
    return pl.pallas_call(
        _combine_kernel,
        grid=(n // COMBINE_TILE,),
        in_specs=[
            pl.BlockSpec((1, 1, TOP_K * COMBINE_TILE), lambda i: (i, 0, 0), memory_space=pltpu.SMEM),
            pl.BlockSpec(memory_space=pl.ANY),
            pl.BlockSpec((COMBINE_TILE, D_MODEL), row),
            pl.BlockSpec((COMBINE_TILE, LANES), row),
        ],
        out_specs=pl.BlockSpec((COMBINE_TILE, D_MODEL), row),
        out_shape=jax.ShapeDtypeStruct((n, D_MODEL), F32),
        scratch_shapes=[pltpu.VMEM((TOP_K, COMBINE_TILE, D_MODEL), F32), pltpu.SemaphoreType.DMA(())],
        compiler_params=_cparams(("arbitrary",)),
        name="moe_combine",
    )(pos3d, y_pad, x2d, route)


def _moe_layer(x1, hn, route, wg, wu, wd):
    n = x1.shape[0]
    e_flat = route[:, :TOP_K].astype(jnp.int32).reshape(-1)
    onehot = (e_flat[:, None] == jnp.arange(N_EXPERTS, dtype=jnp.int32)[None, :]).astype(jnp.int32)
    csum = jnp.cumsum(onehot, axis=0)
    counts = csum[-1]
    padded = (counts + MOE_BLOCK - 1) // MOE_BLOCK * MOE_BLOCK
    pad_ends = jnp.cumsum(padded)
    pad_starts = pad_ends - padded
    dest = jnp.sum((csum - onehot + pad_starts[None, :]) * onehot, axis=1)
    n_blocks = -(-(n * TOP_K) // MOE_BLOCK) + N_EXPERTS
    n_rows = n_blocks * MOE_BLOCK
    block_start = jnp.arange(n_blocks, dtype=jnp.int32) * MOE_BLOCK
    block_expert = jnp.minimum(
        jnp.sum((pad_ends[None, :] <= block_start[:, None]).astype(jnp.int32), axis=1), N_EXPERTS - 1)
    n_active = (pad_ends[-1:] // MOE_BLOCK).astype(jnp.int32)
    fill = jnp.concatenate([pad_starts + counts, pad_starts[1:], jnp.full((1,), n_rows, jnp.int32)])

    x_pad = _dispatch_call(fill.astype(jnp.int32), dest.reshape(n // DISPATCH_TILE, 1, TOP_K * DISPATCH_TILE),
                           hn, n_rows)
    y_pad = _moe_call(block_expert.astype(jnp.int32), n_active, x_pad, wg, wu, wd)
    pos3d = dest.reshape(n // COMBINE_TILE, 1, TOP_K * COMBINE_TILE)
    return _combine_call(pos3d, y_pad, x1, route)


def _head_group_matrix():
    idx = np.arange(MXU_DIM) // HEAD_DIM
    return jnp.asarray((idx[:, None] == idx[None, :]).astype(np.float32), dtype=BF16)


def _pack_router(w_router):
    hi = w_router.astype(BF16)
    lo = (w_router - hi.astype(F32)).astype(BF16)
    pad = jnp.zeros((D_MODEL, LANES - 2 * N_EXPERTS), BF16)
    return jnp.concatenate([hi, lo, pad], axis=1)


def kernel(x, norm_mix_g, norm_ffn_g, w_in, w_out, q_norm_a, k_norm_a, q_norm_b, k_norm_b,
           lambda_q1, lambda_k1, lambda_q2, lambda_k2, subln_g, rel_bias,
           w_gate_dense, w_up_dense, w_down_dense, w_router, w_gate_moe, w_up_moe, w_down_moe):
    b, s, d = x.shape
    assert (s, d) == (SEQ, D_MODEL)
    depth = w_in.shape[0]
    n = b * s
    x2d = x.reshape(n, d)
    gmat = _head_group_matrix()
    bias_a = _bias_tables(rel_bias, _dilated_buckets(), 0, N_HEADS_A)
    bias_b = _bias_tables(rel_bias, _diff_buckets(), N_HEADS_A, N_HEADS_B)
    far_b = rel_bias[_far_bucket(), N_HEADS_A:]
    ones = jnp.ones((PROJ_COL_TILE,), F32)
    rep = PROJ_COL_TILE // HEAD_DIM

    for layer in range(depth):
        gains = jnp.concatenate([
            jnp.tile(q_norm_a[layer], rep) * ATTN_SCALE, jnp.tile(k_norm_a[layer], rep), ones,
            jnp.tile(q_norm_b[layer], rep) * ATTN_SCALE, jnp.tile(k_norm_b[layer], rep), ones,
        ]).reshape(1, PROJ_WIDTH)
        proj = _proj_call(x2d, norm_mix_g[layer].reshape(1, d), w_in[layer].astype(BF16), gains, gmat)
        proj3d = proj.reshape(b, s, PROJ_WIDTH)

        oa = _mixer_a_call(proj3d, bias_a)
        lam_init = 0.8 - 0.6 * math.exp(-0.3 * layer)
        lam_rows = jnp.zeros((8, LANES), F32).at[:4, :HEAD_DIM].set(
            jnp.stack([lambda_q1[layer], lambda_k1[layer], lambda_q2[layer], lambda_k2[layer]]))
        ob = _mixer_b_call(proj3d, bias_b, far_b, lam_rows, subln_g[layer].reshape(1, LANES), lam_init)

        is_moe = layer % 2 == 1
        i = layer // 2
        outs = _out_proj_call(
            oa.reshape(n, WIDTH_A), ob.reshape(n, WIDTH_B), w_out[layer].astype(BF16), x2d,
            norm_ffn_g[layer].reshape(1, d), _pack_router(w_router[i]) if is_moe else None,
            F32 if is_moe else BF16)
        if is_moe:
            x1, hn, route = outs
            x2d = _moe_layer(x1, hn, route, w_gate_moe[i].astype(BF16), w_up_moe[i].astype(BF16),
                             w_down_moe[i].astype(BF16))
        else:
            x1, hn = outs
            x2d = _ffn_call(hn, x1, w_gate_dense[i].astype(BF16), w_up_dense[i].astype(BF16),
                            w_down_dense[i].astype(BF16))
    return x2d.reshape(b, s, d)
```

```python
import functools
import math

import numpy as np
import jax
import jax.numpy as jnp
from jax import lax
from jax.experimental import pallas as pl
from jax.experimental.pallas import tpu as pltpu

F32 = jnp.float32
BF16 = jnp.bfloat16

D_MODEL = 1024
SEQ = 2048
HEAD_DIM = 64
ATTN_SCALE = HEAD_DIM ** -0.5
LOG2E = math.log2(math.e)
N_HEADS_A = 8
DIL_PATTERNS = ((128, 1), (512, 4), (2048, 16))
DIL_BLOCK = 128
N_HEADS_B = 4
WIDTH_A = N_HEADS_A * HEAD_DIM
WIDTH_B = N_HEADS_B * 2 * HEAD_DIM
PROJ_WIDTH = 3 * (WIDTH_A + WIDTH_B)
N_BUCKETS = 32
MAX_DISTANCE = 128
D_FF = 2816
N_EXPERTS = 8
TOP_K = 2
D_FF_EXPERT = 3584
EPS = 1e-6
NEG = -1e30

LANES = 128
MXU_DIM = 256
VMEM_LIMIT = 56 * 1024 * 1024

ROW_TILE = 512
PROJ_COL_TILE = 512
Q_TILE_B = 512
DIL_GROUP = 16
MOE_BLOCK = 512
MOE_FF_TILE = 1792
FFN_CHUNKS = (512, 512, 512, 512, 512, 256)
DISPATCH_TILE = 256
COMBINE_TILE = 256


def _cparams(sem):
    return pltpu.CompilerParams(dimension_semantics=sem, vmem_limit_bytes=VMEM_LIMIT)


def _resident(shape):
    nd = len(shape)
    return pl.BlockSpec(shape, lambda *_: (0,) * nd, pipeline_mode=pl.Buffered(1))


def _t5_bucket_np(dist):
    n = np.maximum(dist, 0)
    max_exact = N_BUCKETS // 2
    large = max_exact + (np.log(np.maximum(n, 1).astype(np.float32) / max_exact)
                         / math.log(MAX_DISTANCE / max_exact) * (N_BUCKETS - max_exact)).astype(np.int32)
    large = np.minimum(large, N_BUCKETS - 1)
    return np.where(n < max_exact, n, large).astype(np.int32)


def _dilated_buckets():
    i = np.arange(DIL_BLOCK)[:, None]
    j = np.arange(2 * DIL_BLOCK)[None, :]
    steps = DIL_BLOCK + i - j
    valid = (steps >= 0) & (steps <= DIL_BLOCK)
    tabs = []
    for _, dil in DIL_PATTERNS:
        bucket = np.where(valid, _t5_bucket_np(steps * dil), -1)
        tabs.append(bucket)
        tabs.append(np.where(j >= DIL_BLOCK, bucket, -1))
    return np.stack(tabs).astype(np.int32)


def _diff_buckets():
    t = Q_TILE_B
    qi = np.arange(t)[:, None]
    kj = np.arange(t)[None, :]
    tabs = []
    for off in (0, t):
        dist = qi - kj + off
        tabs.append(np.where(dist >= 0, _t5_bucket_np(dist), -1))
    return np.stack(tabs).astype(np.int32)


def _far_bucket():
    b = _t5_bucket_np(np.arange(Q_TILE_B + 1, SEQ + 1))
    assert (b == b[0]).all()
    return int(b[0])


def _bias_table_kernel(rel_ref, bucket_ref, o_ref, *, head0):
    col = head0 + pl.program_id(1)
    bucket = bucket_ref[0]
    val = jnp.full(bucket.shape, NEG, F32)
    for b in range(N_BUCKETS):
        val = jnp.where(bucket == b, rel_ref[b, col] * LOG2E, val)
    o_ref[0, 0] = val


def _bias_tables(rel_bias, buckets, head0, n_heads):
    n_tab, rows, cols = buckets.shape
    return pl.pallas_call(
        functools.partial(_bias_table_kernel, head0=head0),
        grid=(n_tab, n_heads),
        in_specs=[
            pl.BlockSpec(memory_space=pltpu.SMEM),
            pl.BlockSpec((1, rows, cols), lambda t, h: (t, 0, 0)),
        ],
        out_specs=pl.BlockSpec((1, 1, rows, cols), lambda t, h: (t, h, 0, 0)),
        out_shape=jax.ShapeDtypeStruct((n_tab, n_heads, rows, cols), F32),
        compiler_params=_cparams(("parallel", "parallel")),
        name="bias_tables",
    )(rel_bias, jnp.asarray(buckets))


def _proj_kernel(x_ref, g_ref, w_ref, gain_ref, gmat_ref, o_ref):
    x = x_ref[...]
    ms = jnp.mean(x * x, axis=-1, keepdims=True)
    xn = (x * lax.rsqrt(ms + EPS) * g_ref[...]).astype(BF16)
    for j in range(PROJ_WIDTH // PROJ_COL_TILE):
        cols = slice(j * PROJ_COL_TILE, (j + 1) * PROJ_COL_TILE)
        acc = jnp.dot(xn, w_ref[:, cols], preferred_element_type=F32)
        if j % 3 == 2:
            o_ref[:, cols] = acc.astype(BF16)
        else:
            sq = (acc * acc).astype(BF16)
            ss = jnp.concatenate(
                [jnp.dot(sq[:, c:c + MXU_DIM], gmat_ref[...], preferred_element_type=F32)
                 for c in range(0, PROJ_COL_TILE, MXU_DIM)], axis=1)
            o_ref[:, cols] = (acc * lax.rsqrt(ss * (1.0 / HEAD_DIM) + EPS) * gain_ref[:, cols]).astype(BF16)


def _proj_call(x2d, g, w_bf16, gains, gmat):
    n = x2d.shape[0]
    return pl.pallas_call(
        _proj_kernel,
        grid=(n // ROW_TILE,),
        in_specs=[
            pl.BlockSpec((ROW_TILE, D_MODEL), lambda i: (i, 0)),
            _resident((1, D_MODEL)),
            _resident((D_MODEL, PROJ_WIDTH)),
            _resident((1, PROJ_WIDTH)),
            _resident((MXU_DIM, MXU_DIM)),
        ],
        out_specs=pl.BlockSpec((ROW_TILE, PROJ_WIDTH), lambda i: (i, 0)),
        out_shape=jax.ShapeDtypeStruct((n, PROJ_WIDTH), BF16),
        compiler_params=_cparams(("parallel",)),
        name="norm_in_proj",
    )(x2d, g, w_bf16, gains, gmat)


def _deinterleave(x, dil):
    if dil == 1:
        return x
    return jnp.swapaxes(x.reshape(SEQ // dil, dil, LANES), 0, 1).reshape(SEQ, LANES)


def _interleave(x, dil):
    if dil == 1:
        return x
    return jnp.swapaxes(x.reshape(dil, SEQ // dil, LANES), 0, 1).reshape(SEQ, LANES)


def _mixer_a_kernel(q_ref, k_ref, v_ref, bias_ref, o_ref, qp, kp, vp, out_s, lse_s):
    n_tiles = SEQ // DIL_BLOCK
    n_branch = len(DIL_PATTERNS)
    lane_full = lax.broadcasted_iota(jnp.int32, (SEQ, LANES), 1)
    lane = lax.broadcasted_iota(jnp.int32, (DIL_BLOCK, LANES), 1)
    head0 = lane < HEAD_DIM

    q = q_ref[0].astype(F32)
    k = k_ref[0].astype(F32)
    v = v_ref[0].astype(F32)
    pad = jnp.zeros((DIL_BLOCK, LANES), BF16)
    for bi, (_, dil) in enumerate(DIL_PATTERNS):
        qd = _deinterleave(q, dil)
        qp[bi, 0] = jnp.where(lane_full < HEAD_DIM, qd, 0.0).astype(BF16)
        qp[bi, 1] = jnp.where(lane_full >= HEAD_DIM, qd, 0.0).astype(BF16)
        kp[bi, :DIL_BLOCK, :] = pad
        vp[bi, :DIL_BLOCK, :] = pad
        kp[bi, DIL_BLOCK:, :] = _deinterleave(k, dil).astype(BF16)
        vp[bi, DIL_BLOCK:, :] = _deinterleave(v, dil).astype(BF16)

    for bi, (_, dil) in enumerate(DIL_PATTERNS):
        nb = SEQ // dil // DIL_BLOCK

        def tile(t, bi=bi, nb=nb):
            row0 = t * DIL_BLOCK if isinstance(t, int) else pl.multiple_of(t * DIL_BLOCK, DIL_BLOCK)
            rows = pl.ds(row0, DIL_BLOCK)
            keys = pl.ds(row0, 2 * DIL_BLOCK)
            tab = 2 * bi + (t % nb == 0)
            kt = kp[bi, keys, :]
            vt = jnp.concatenate([vp[bi, keys, :], jnp.ones((2 * DIL_BLOCK, LANES), BF16)], axis=1)
            m_h, pv_h = [], []
            for hh in range(2):
                s = lax.dot_general(qp[bi, hh, rows, :], kt, (((1,), (1,)), ((), ())),
                                    preferred_element_type=F32) + bias_ref[tab, hh]
                m = jnp.broadcast_to(jnp.max(s, axis=-1, keepdims=True), (DIL_BLOCK, LANES))
                p = jnp.exp2(s - jnp.concatenate([m, m], axis=1))
                m_h.append(m)
                pv_h.append(jnp.dot(p.astype(BF16), vt, preferred_element_type=F32))
            l = jnp.where(head0, pv_h[0][:, LANES:], pv_h[1][:, LANES:])
            out_s[bi, rows, :] = jnp.where(head0, pv_h[0][:, :LANES], pv_h[1][:, :LANES]) / l
            lse_s[bi, rows, :] = jnp.where(head0, m_h[0], m_h[1]) + jnp.log2(l)

        if DIL_GROUP == n_tiles:
            for t in range(n_tiles):
                tile(t)
        else:
            def group(g, carry, tile=tile):
                for u in range(DIL_GROUP):
                    tile(g * DIL_GROUP + u)
                return carry

            lax.fori_loop(0, n_tiles // DIL_GROUP, group, 0)

    lse_b = [_interleave(lse_s[bi], dil) for bi, (_, dil) in enumerate(DIL_PATTERNS)]
    lse_max = functools.reduce(jnp.maximum, lse_b)
    w_b = [jnp.exp2(lse - lse_max) for lse in lse_b]
    num = sum(w * _interleave(out_s[bi], dil) for w, (bi, (_, dil)) in zip(w_b, enumerate(DIL_PATTERNS)))
    o_ref[0] = (num / sum(w_b)).astype(o_ref.dtype)


def _mixer_a_call(proj3d, bias_a):
    b = proj3d.shape[0]
    n_pairs = N_HEADS_A // 2
    n_branch = len(DIL_PATTERNS)
    q_blk0, k_blk0, v_blk0 = 0, WIDTH_A // LANES, 2 * WIDTH_A // LANES
    return pl.pallas_call(
        _mixer_a_kernel,
        grid=(b, n_pairs),
        in_specs=[
            pl.BlockSpec((1, SEQ, LANES), lambda i, p: (i, 0, q_blk0 + p)),
            pl.BlockSpec((1, SEQ, LANES), lambda i, p: (i, 0, k_blk0 + p)),
            pl.BlockSpec((1, SEQ, LANES), lambda i, p: (i, 0, v_blk0 + p)),
            pl.BlockSpec((bias_a.shape[0], 2, DIL_BLOCK, 2 * DIL_BLOCK), lambda i, p: (0, p, 0, 0)),
        ],
        out_specs=pl.BlockSpec((1, SEQ, LANES), lambda i, p: (i, 0, p)),
        out_shape=jax.ShapeDtypeStruct((b, SEQ, WIDTH_A), BF16),
        scratch_shapes=[
            pltpu.VMEM((n_branch, 2, SEQ, LANES), BF16),
            pltpu.VMEM((n_branch, SEQ + DIL_BLOCK, LANES), BF16),
            pltpu.VMEM((n_branch, SEQ + DIL_BLOCK, LANES), BF16),
            pltpu.VMEM((n_branch, SEQ, LANES), F32),
            pltpu.VMEM((n_branch, SEQ, LANES), F32),
        ],
        compiler_params=_cparams(("parallel", "parallel")),
        name="dilated_attention",
    )(proj3d, proj3d, proj3d, bias_a)


def _mixer_b_kernel(far_ref, lam_ref, q_ref, k_ref, v_ref, bias_ref, subg_ref, o_ref, m_s, acc_s,
                    *, lam_init):
    pair = pl.program_id(1)
    i = pl.program_id(2)
    t = Q_TILE_B
    reps = t // LANES
    lane = lax.broadcasted_iota(jnp.int32, (t, LANES), 1)
    q_maps = []
    for hd in range(2):
        q = q_ref[0, :, hd * LANES:(hd + 1) * LANES]
        zero = jnp.zeros_like(q)
        q_maps.append((jnp.where(lane < HEAD_DIM, q, zero), jnp.where(lane >= HEAD_DIM, q, zero)))

    m_s[...] = jnp.full(m_s.shape, NEG, F32)
    acc_s[...] = jnp.zeros(acc_s.shape, F32)
    ones = jnp.ones((t, LANES), BF16)

    def step(j, bias_of_head, bias_is_scalar):
        koff = pl.multiple_of(j * t, t)
        for hd in range(2):
            kt = k_ref[0, pl.ds(koff, t), hd * LANES:(hd + 1) * LANES]
            vt = jnp.concatenate([v_ref[0, pl.ds(koff, t), hd * LANES:(hd + 1) * LANES], ones], axis=1)
            bias = bias_of_head(hd)
            for half in range(2):
                c = 2 * hd + half
                s = lax.dot_general(q_maps[hd][half], kt, (((1,), (1,)), ((), ())),
                                    preferred_element_type=F32)
                m_old = m_s[c]
                if bias_is_scalar:
                    m_new = jnp.maximum(m_old, jnp.max(s, axis=-1, keepdims=True) + bias)
                    shift = m_new - bias
                else:
                    s = s + bias
                    m_new = jnp.maximum(m_old, jnp.max(s, axis=-1, keepdims=True))
                    shift = m_new
                p = jnp.exp2(s - jnp.concatenate([shift] * reps, axis=1))
                alpha = jnp.exp2(m_old - m_new)
                acc_s[c] = (acc_s[c] * jnp.concatenate([alpha, alpha], axis=1)
                            + jnp.dot(p.astype(BF16), vt, preferred_element_type=F32))
                m_s[c] = m_new

    def far_step(j, carry):
        step(j, lambda hd: far_ref[2 * pair + hd] * LOG2E, True)
        return carry

    lax.fori_loop(0, jnp.maximum(i - 1, 0), far_step, 0)

    @pl.when(i > 0)
    def _():
        step(i - 1, lambda hd: bias_ref[1, hd], False)

    step(i, lambda hd: bias_ref[0, hd], False)

    lam_rows = lam_ref[...]
    e1 = jnp.exp(jnp.sum(lam_rows[0:1, :] * lam_rows[1:2, :], axis=-1, keepdims=True))
    e2 = jnp.exp(jnp.sum(lam_rows[2:3, :] * lam_rows[3:4, :], axis=-1, keepdims=True))
    lam = e1 - e2 + lam_init
    for hd in range(2):
        a1, a2 = acc_s[2 * hd], acc_s[2 * hd + 1]
        o = a1[:, :LANES] / a1[:, LANES:] - lam * (a2[:, :LANES] / a2[:, LANES:])
        ms = jnp.mean(o * o, axis=-1, keepdims=True)
        o_ref[0, :, hd * LANES:(hd + 1) * LANES] = (
            o * lax.rsqrt(ms + EPS) * subg_ref[...] * (1.0 - lam_init)).astype(o_ref.dtype)


def _mixer_b_call(proj3d, bias_b, far_b, lam_rows, subg, lam_init):
    b = proj3d.shape[0]
    width = 2 * LANES
    base = 3 * WIDTH_A // width
    q_blk0, k_blk0, v_blk0 = base, base + WIDTH_B // width, base + 2 * WIDTH_B // width
    t = Q_TILE_B
    grid_spec = pltpu.PrefetchScalarGridSpec(
        num_scalar_prefetch=1,
        grid=(b, N_HEADS_B // 2, SEQ // t),
        in_specs=[
            pl.BlockSpec((8, LANES), lambda bi, p, i, far: (0, 0)),
            pl.BlockSpec((1, t, width), lambda bi, p, i, far: (bi, i, q_blk0 + p)),
            pl.BlockSpec((1, SEQ, width), lambda bi, p, i, far: (bi, 0, k_blk0 + p)),
            pl.BlockSpec((1, SEQ, width), lambda bi, p, i, far: (bi, 0, v_blk0 + p)),
            pl.BlockSpec((2, 2, t, t), lambda bi, p, i, far: (0, p, 0, 0)),
            pl.BlockSpec((1, LANES), lambda bi, p, i, far: (0, 0)),
        ],
        out_specs=pl.BlockSpec((1, t, width), lambda bi, p, i, far: (bi, i, p)),
        scratch_shapes=[pltpu.VMEM((4, t, LANES), F32), pltpu.VMEM((4, t, 2 * LANES), F32)],
    )
    return pl.pallas_call(
        functools.partial(_mixer_b_kernel, lam_init=lam_init),
        grid_spec=grid_spec,
        out_shape=jax.ShapeDtypeStruct((b, SEQ, WIDTH_B), BF16),
        compiler_params=_cparams(("parallel", "parallel", "arbitrary")),
        name="diff_attention",
    )(far_b, lam_rows, proj3d, proj3d, proj3d, bias_b, subg)


def _out_proj_kernel(oa_ref, ob_ref, w_ref, x_ref, g_ref, *rest, with_router):
    if with_router:
        wr_ref, x1_ref, h_ref, route_ref = rest
    else:
        x1_ref, h_ref = rest
    acc = jnp.dot(oa_ref[...], w_ref[:WIDTH_A, :], preferred_element_type=F32)
    acc = acc + jnp.dot(ob_ref[...], w_ref[WIDTH_A:, :], preferred_element_type=F32)
    x1 = x_ref[...] + acc
    x1_ref[...] = x1
    ms = jnp.mean(x1 * x1, axis=-1, keepdims=True)
    hn = x1 * lax.rsqrt(ms + EPS) * g_ref[...]
    h_ref[...] = hn.astype(h_ref.dtype)
    if with_router:
        hi = hn.astype(BF16)
        lo = (hn - hi.astype(F32)).astype(BF16)
        lg = (jnp.dot(hi, wr_ref[...], preferred_element_type=F32)
              + jnp.dot(lo, wr_ref[...], preferred_element_type=F32))
        lg = lg + pltpu.roll(lg, LANES - N_EXPERTS, 1)
        lane = lax.broadcasted_iota(jnp.int32, lg.shape, 1)
        lane_f = lane.astype(F32)
        lg = jnp.where(lane < N_EXPERTS, lg, -jnp.inf)
        v1 = jnp.max(lg, axis=-1, keepdims=True)
        i1 = jnp.min(jnp.where(lg == v1, lane_f, float(LANES)), axis=-1, keepdims=True)
        lg2 = jnp.where(lane_f == i1, -jnp.inf, lg)
        v2 = jnp.max(lg2, axis=-1, keepdims=True)
        i2 = jnp.min(jnp.where(lg2 == v2, lane_f, float(LANES)), axis=-1, keepdims=True)
        e = jnp.exp(v2 - v1)
        g1 = 1.0 / (1.0 + e)
        g2 = e / (1.0 + e)
        route_ref[...] = jnp.where(lane == 0, i1, jnp.where(lane == 1, i2,
                                   jnp.where(lane == 2, g1, jnp.where(lane == 3, g2, 0.0))))


def _out_proj_call(oa, ob, w_bf16, x2d, g, w_router_packed, h_dtype):
    n = x2d.shape[0]
    with_router = w_router_packed is not None
    row = lambda i: (i, 0)
    in_specs = [
        pl.BlockSpec((ROW_TILE, WIDTH_A), row),
        pl.BlockSpec((ROW_TILE, WIDTH_B), row),
        _resident((WIDTH_A + WIDTH_B, D_MODEL)),
        pl.BlockSpec((ROW_TILE, D_MODEL), row),
        _resident((1, D_MODEL)),
    ]
    out_specs = [pl.BlockSpec((ROW_TILE, D_MODEL), row), pl.BlockSpec((ROW_TILE, D_MODEL), row)]
    out_shape = [jax.ShapeDtypeStruct((n, D_MODEL), F32), jax.ShapeDtypeStruct((n, D_MODEL), h_dtype)]
    args = [oa, ob, w_bf16, x2d, g]
    if with_router:
        in_specs.append(_resident((D_MODEL, LANES)))
        out_specs.append(pl.BlockSpec((ROW_TILE, LANES), row))
        out_shape.append(jax.ShapeDtypeStruct((n, LANES), F32))
        args.append(w_router_packed)
    return pl.pallas_call(
        functools.partial(_out_proj_kernel, with_router=with_router),
        grid=(n // ROW_TILE,),
        in_specs=in_specs,
        out_specs=out_specs,
        out_shape=out_shape,
        compiler_params=_cparams(("parallel",)),
        name="out_proj_router" if with_router else "out_proj",
    )(*args)


def _swiglu_act(g, u):
    return (g / (1.0 + jnp.exp(-g))) * u


def _ffn_kernel(h_ref, x_ref, wg_ref, wu_ref, wd_ref, o_ref, a_ref):
    h = h_ref[...]
    c0 = 0
    for width in FFN_CHUNKS:
        g = jnp.dot(h, wg_ref[:, c0:c0 + width], preferred_element_type=F32)
        u = jnp.dot(h, wu_ref[:, c0:c0 + width], preferred_element_type=F32)
        a_ref[:, c0:c0 + width] = _swiglu_act(g, u).astype(BF16)
        c0 += width
    o_ref[...] = x_ref[...] + jnp.dot(a_ref[...], wd_ref[...], preferred_element_type=F32)


def _ffn_call(h, x2d, wg, wu, wd):
    n = x2d.shape[0]
    row = lambda i: (i, 0)
    return pl.pallas_call(
        _ffn_kernel,
        grid=(n // ROW_TILE,),
        in_specs=[
            pl.BlockSpec((ROW_TILE, D_MODEL), row),
            pl.BlockSpec((ROW_TILE, D_MODEL), row),
            _resident((D_MODEL, D_FF)),
            _resident((D_MODEL, D_FF)),
            _resident((D_FF, D_MODEL)),
        ],
        out_specs=pl.BlockSpec((ROW_TILE, D_MODEL), row),
        out_shape=jax.ShapeDtypeStruct((n, D_MODEL), F32),
        scratch_shapes=[pltpu.VMEM((ROW_TILE, D_FF), BF16)],
        compiler_params=_cparams(("parallel",)),
        name="dense_swiglu",
    )(h, x2d, wg, wu, wd)


def _row_copy(src_ref, src_row, dst_ref, dst_row, sem):
    return pltpu.make_async_copy(src_ref.at[pl.ds(src_row, 1)], dst_ref.at[pl.ds(dst_row, 1)], sem)


def _dispatch_kernel(fill_ref, dest_ref, h_ref, o_hbm, zero_ref, sem):
    i = pl.program_id(0)

    def start(t, c):
        for k in range(TOP_K):
            _row_copy(h_ref, t, o_hbm, dest_ref[0, 0, TOP_K * t + k], sem).start()
        return c

    def wait(t, c):
        for k in range(TOP_K):
            _row_copy(h_ref, t, o_hbm, 0, sem).wait()
        return c

    lax.fori_loop(0, DISPATCH_TILE, start, 0, unroll=4)
    lax.fori_loop(0, DISPATCH_TILE, wait, 0, unroll=4)

    @pl.when(i == pl.num_programs(0) - 1)
    def _():
        zero_ref[...] = jnp.zeros_like(zero_ref)
        for e in range(N_EXPERTS):
            lo, hi = fill_ref[e], fill_ref[N_EXPERTS + e]
            lax.fori_loop(lo, hi, lambda r, c: (_row_copy(zero_ref, 0, o_hbm, r, sem).start(), c)[1], 0)
            lax.fori_loop(lo, hi, lambda r, c: (_row_copy(zero_ref, 0, o_hbm, 0, sem).wait(), c)[1], 0)


def _dispatch_call(fill, dest3d, h, n_rows):
    n = h.shape[0]
    grid_spec = pltpu.PrefetchScalarGridSpec(
        num_scalar_prefetch=1,
        grid=(n // DISPATCH_TILE,),
        in_specs=[
            pl.BlockSpec((1, 1, TOP_K * DISPATCH_TILE), lambda i, f: (i, 0, 0), memory_space=pltpu.SMEM),
            pl.BlockSpec((DISPATCH_TILE, D_MODEL), lambda i, f: (i, 0)),
        ],
        out_specs=pl.BlockSpec(memory_space=pl.ANY),
        scratch_shapes=[pltpu.VMEM((8, D_MODEL), h.dtype), pltpu.SemaphoreType.DMA(())],
    )
    return pl.pallas_call(
        _dispatch_kernel,
        grid_spec=grid_spec,
        out_shape=jax.ShapeDtypeStruct((n_rows, D_MODEL), h.dtype),
        compiler_params=_cparams(("arbitrary",)),
        name="moe_dispatch",
    )(fill, dest3d, h)


def _moe_kernel(be_ref, nact_ref, x_ref, wg_ref, wu_ref, wd_ref, o_ref, xb_ref, acc_ref):
    b = pl.program_id(0)
    c = pl.program_id(1)
    last = pl.num_programs(1) - 1
    active = b < nact_ref[0]

    @pl.when(active)
    def _():
        @pl.when(c == 0)
        def _():
            xb_ref[...] = x_ref[...].astype(BF16)

        xb = xb_ref[...]
        g = jnp.dot(xb, wg_ref[0], preferred_element_type=F32)
        u = jnp.dot(xb, wu_ref[0], preferred_element_type=F32)
        contrib = jnp.dot(_swiglu_act(g, u).astype(BF16), wd_ref[0], preferred_element_type=F32)

        @pl.when(c == 0)
        def _():
            acc_ref[...] = contrib

        @pl.when(c > 0)
        def _():
            acc_ref[...] += contrib

        @pl.when(c == last)
        def _():
            o_ref[...] = acc_ref[...]

    @pl.when(jnp.logical_not(active) & (c == last))
    def _():
        o_ref[...] = jnp.zeros_like(o_ref)


def _moe_call(block_expert, n_active, x_pad, wg, wu, wd):
    n_rows = x_pad.shape[0]
    n_blocks = n_rows // MOE_BLOCK
    n_chunks = D_FF_EXPERT // MOE_FF_TILE
    grid_spec = pltpu.PrefetchScalarGridSpec(
        num_scalar_prefetch=2,
        grid=(n_blocks, n_chunks),
        in_specs=[
            pl.BlockSpec((MOE_BLOCK, D_MODEL), lambda b, c, be, na: (b, 0)),
            pl.BlockSpec((1, D_MODEL, MOE_FF_TILE), lambda b, c, be, na: (be[b], 0, c)),
            pl.BlockSpec((1, D_MODEL, MOE_FF_TILE), lambda b, c, be, na: (be[b], 0, c)),
            pl.BlockSpec((1, MOE_FF_TILE, D_MODEL), lambda b, c, be, na: (be[b], c, 0)),
        ],
        out_specs=pl.BlockSpec((MOE_BLOCK, D_MODEL), lambda b, c, be, na: (b, 0)),
        scratch_shapes=[pltpu.VMEM((MOE_BLOCK, D_MODEL), BF16), pltpu.VMEM((MOE_BLOCK, D_MODEL), F32)],
    )
    return pl.pallas_call(
        _moe_kernel,
        grid_spec=grid_spec,
        out_shape=jax.ShapeDtypeStruct((n_rows, D_MODEL), F32),
        compiler_params=_cparams(("arbitrary", "arbitrary")),
        name="moe_swiglu",
    )(block_expert, n_active, x_pad, wg, wu, wd)


def _combine_kernel(pos_ref, y_hbm, x_ref, route_ref, o_ref, buf, sem):
    def start(t, c):
        for k in range(TOP_K):
            _row_copy(y_hbm, pos_ref[0, 0, TOP_K * t + k], buf.at[k], t, sem).start()
        return c

    def wait(t, c):
        for k in range(TOP_K):
            _row_copy(y_hbm, 0, buf.at[k], t, sem).wait()
        return c

    lax.fori_loop(0, COMBINE_TILE, start, 0, unroll=4)
    lax.fori_loop(0, COMBINE_TILE, wait, 0, unroll=4)
    route = route_ref[...]
    g0 = route[:, TOP_K:TOP_K + 1]
    g1 = route[:, TOP_K + 1:TOP_K + 2]
    o_ref[...] = x_ref[...] + g0 * buf[0] + g1 * buf[1]


def _combine_call(pos3d, y_pad, x2d, route):
    n = x2d.shape[0]
    row = lambda i: (i, 0)
    return pl.pallas_call(
        _combine_kernel,
        grid=(n // COMBINE_TILE,),
        in_specs=[
            pl.BlockSpec((1, 1, TOP_K * COMBINE_TILE), lambda i: (i, 0, 0), memory_space=pltpu.SMEM),
            pl.BlockSpec(memory_space=pl.ANY),
            pl.BlockSpec((COMBINE_TILE, D_MODEL), row),
            pl.BlockSpec((COMBINE_TILE, LANES), row),
        ],
        out_specs=pl.BlockSpec((COMBINE_TILE, D_MODEL), row),
        out_shape=jax.ShapeDtypeStruct((n, D_MODEL), F32),
        scratch_shapes=[pltpu.VMEM((TOP_K, COMBINE_TILE, D_MODEL), F32), pltpu.SemaphoreType.DMA(())],
        compiler_params=_cparams(("arbitrary",)),
        name="moe_combine",
    )(pos3d, y_pad, x2d, route)


def _moe_layer(x1, hn, route, wg, wu, wd):
    n = x1.shape[0]
    e_flat = route[:, :TOP_K].astype(jnp.int32).reshape(-1)
    onehot = (e_flat[:, None] == jnp.arange(N_EXPERTS, dtype=jnp.int32)[None, :]).astype(jnp.int32)
    csum = jnp.cumsum(onehot, axis=0)
    counts = csum[-1]
    padded = (counts + MOE_BLOCK - 1) // MOE_BLOCK * MOE_BLOCK
    pad_ends = jnp.cumsum(padded)
    pad_starts = pad_ends - padded
    dest = jnp.sum((csum - onehot + pad_starts[None, :]) * onehot, axis=1)
    n_blocks = -(-(n * TOP_K) // MOE_BLOCK) + N_EXPERTS
    n_rows = n_blocks * MOE_BLOCK
    block_start = jnp.arange(n_blocks, dtype=jnp.int32) * MOE_BLOCK
    block_expert = jnp.minimum(
        jnp.sum((pad_ends[None, :] <= block_start[:, None]).astype(jnp.int32), axis=1), N_EXPERTS - 1)
    n_active = (pad_ends[-1:] // MOE_BLOCK).astype(jnp.int32)
    fill = jnp.concatenate([pad_starts + counts, pad_starts[1:], jnp.full((1,), n_rows, jnp.int32)])

    x_pad = _dispatch_call(fill.astype(jnp.int32), dest.reshape(n // DISPATCH_TILE, 1, TOP_K * DISPATCH_TILE),
                           hn, n_rows)
    y_pad = _moe_call(block_expert.astype(jnp.int32), n_active, x_pad, wg, wu, wd)
    pos3d = dest.reshape(n // COMBINE_TILE, 1, TOP_K * COMBINE_TILE)
    return _combine_call(pos3d, y_pad, x1, route)


def _head_group_matrix():
    idx = np.arange(MXU_DIM) // HEAD_DIM
    return jnp.asarray((idx[:, None] == idx[None, :]).astype(np.float32), dtype=BF16)


def _pack_router(w_router):
    hi = w_router.astype(BF16)
    lo = (w_router - hi.astype(F32)).astype(BF16)
    pad = jnp.zeros((D_MODEL, LANES - 2 * N_EXPERTS), BF16)
    return jnp.concatenate([hi, lo, pad], axis=1)


def kernel(x, norm_mix_g, norm_ffn_g, w_in, w_out, q_norm_a, k_norm_a, q_norm_b, k_norm_b,
           lambda_q1, lambda_k1, lambda_q2, lambda_k2, subln_g, rel_bias,
           w_gate_dense, w_up_dense, w_down_dense, w_router, w_gate_moe, w_up_moe, w_down_moe):
    b, s, d = x.shape
    assert (s, d) == (SEQ, D_MODEL)
    depth = w_in.shape[0]
    n = b * s
    x2d = x.reshape(n, d)
    gmat = _head_group_matrix()
    bias_a = _bias_tables(rel_bias, _dilated_buckets(), 0, N_HEADS_A)
    bias_b = _bias_tables(rel_bias, _diff_buckets(), N_HEADS_A, N_HEADS_B)
    far_b = rel_bias[_far_bucket(), N_HEADS_A:]
    ones = jnp.ones((PROJ_COL_TILE,), F32)
    rep = PROJ_COL_TILE // HEAD_DIM

    for layer in range(depth):
        gains = jnp.concatenate([
            jnp.tile(q_norm_a[layer], rep) * (ATTN_SCALE * LOG2E), jnp.tile(k_norm_a[layer], rep), ones,
            jnp.tile(q_norm_b[layer], rep) * (ATTN_SCALE * LOG2E), jnp.tile(k_norm_b[layer], rep), ones,
        ]).reshape(1, PROJ_WIDTH)
        proj = _proj_call(x2d, norm_mix_g[layer].reshape(1, d), w_in[layer].astype(BF16), gains, gmat)
        proj3d = proj.reshape(b, s, PROJ_WIDTH)

        oa = _mixer_a_call(proj3d, bias_a)
        lam_init = 0.8 - 0.6 * math.exp(-0.3 * layer)
        lam_rows = jnp.zeros((8, LANES), F32).at[:4, :HEAD_DIM].set(
            jnp.stack([lambda_q1[layer], lambda_k1[layer], lambda_q2[layer], lambda_k2[layer]]))
        ob = _mixer_b_call(proj3d, bias_b, far_b, lam_rows, subln_g[layer].reshape(1, LANES), lam_init)

        is_moe = layer % 2 == 1
        i = layer // 2
        outs = _out_proj_call(
            oa.reshape(n, WIDTH_A), ob.reshape(n, WIDTH_B), w_out[layer].astype(BF16), x2d,
            norm_ffn_g[layer].reshape(1, d), _pack_router(w_router[i]) if is_moe else None,
            F32 if is_moe else BF16)
        if is_moe:
            x1, hn, route = outs
            x2d = _moe_layer(x1, hn, route, w_gate_moe[i].astype(BF16), w_up_moe[i].astype(BF16),
                             w_down_moe[i].astype(BF16))
        else:
            x1, hn = outs
            x2d = _ffn_call(hn, x1, w_gate_dense[i].astype(BF16), w_up_dense[i].astype(BF16),
                            w_down_dense[i].astype(BF16))
    return x2d.reshape(b, s, d)
```

```python
import functools
import math

import numpy as np
import jax
import jax.numpy as jnp
from jax import lax
from jax.experimental import pallas as pl
from jax.experimental.pallas import tpu as pltpu

F32 = jnp.float32
BF16 = jnp.bfloat16

D_MODEL = 1024
SEQ = 2048
HEAD_DIM = 64
ATTN_SCALE = HEAD_DIM ** -0.5
LOG2E = math.log2(math.e)
N_HEADS_A = 8
DIL_PATTERNS = ((128, 1), (512, 4), (2048, 16))
DIL_BLOCK = 128
N_HEADS_B = 4
WIDTH_A = N_HEADS_A * HEAD_DIM
WIDTH_B = N_HEADS_B * 2 * HEAD_DIM
PROJ_WIDTH = 3 * (WIDTH_A + WIDTH_B)
N_BUCKETS = 32
MAX_DISTANCE = 128
D_FF = 2816
N_EXPERTS = 8
TOP_K = 2
D_FF_EXPERT = 3584
EPS = 1e-6
NEG = -1e30

LANES = 128
MXU_DIM = 256
VMEM_LIMIT = 56 * 1024 * 1024

ROW_TILE = 512
PROJ_COL_TILE = 512
Q_TILE_B = 512
DIL_GROUP = 16
MOE_BLOCK = 512
MOE_FF_TILE = 1792
FFN_CHUNKS = (512, 512, 512, 512, 512, 256)
DISPATCH_TILE = 256
COMBINE_TILE = 256


def _cparams(sem):
    return pltpu.CompilerParams(dimension_semantics=sem, vmem_limit_bytes=VMEM_LIMIT)


def _resident(shape):
    nd = len(shape)
    return pl.BlockSpec(shape, lambda *_: (0,) * nd, pipeline_mode=pl.Buffered(1))


def _t5_bucket_np(dist):
    n = np.maximum(dist, 0)
    max_exact = N_BUCKETS // 2
    large = max_exact + (np.log(np.maximum(n, 1).astype(np.float32) / max_exact)
                         / math.log(MAX_DISTANCE / max_exact) * (N_BUCKETS - max_exact)).astype(np.int32)
    large = np.minimum(large, N_BUCKETS - 1)
    return np.where(n < max_exact, n, large).astype(np.int32)


def _dilated_buckets():
    i = np.arange(DIL_BLOCK)[:, None]
    j = np.arange(2 * DIL_BLOCK)[None, :]
    steps = DIL_BLOCK + i - j
    valid = (steps >= 0) & (steps <= DIL_BLOCK)
    tabs = []
    for _, dil in DIL_PATTERNS:
        bucket = np.where(valid, _t5_bucket_np(steps * dil), -1)
        tabs.append(bucket)
        tabs.append(np.where(j >= DIL_BLOCK, bucket, -1))
    return np.stack(tabs).astype(np.int32)


def _diff_buckets():
    t = Q_TILE_B
    qi = np.arange(t)[:, None]
    kj = np.arange(t)[None, :]
    tabs = []
    for off in (0, t):
        dist = qi - kj + off
        tabs.append(np.where(dist >= 0, _t5_bucket_np(dist), -1))
    return np.stack(tabs).astype(np.int32)


def _far_bucket():
    b = _t5_bucket_np(np.arange(Q_TILE_B + 1, SEQ + 1))
    assert (b == b[0]).all()
    return int(b[0])


def _bias_table_kernel(rel_ref, bucket_ref, o_ref, *, head0):
    col = head0 + pl.program_id(1)
    bucket = bucket_ref[0]
    val = jnp.full(bucket.shape, NEG, F32)
    for b in range(N_BUCKETS):
        val = jnp.where(bucket == b, rel_ref[b, col] * LOG2E, val)
    o_ref[0, 0] = val


def _bias_tables(rel_bias, buckets, head0, n_heads):
    n_tab, rows, cols = buckets.shape
    return pl.pallas_call(
        functools.partial(_bias_table_kernel, head0=head0),
        grid=(n_tab, n_heads),
        in_specs=[
            pl.BlockSpec(memory_space=pltpu.SMEM),
            pl.BlockSpec((1, rows, cols), lambda t, h: (t, 0, 0)),
        ],
        out_specs=pl.BlockSpec((1, 1, rows, cols), lambda t, h: (t, h, 0, 0)),
        out_shape=jax.ShapeDtypeStruct((n_tab, n_heads, rows, cols), F32),
        compiler_params=_cparams(("parallel", "parallel")),
        name="bias_tables",
    )(rel_bias, jnp.asarray(buckets))


def _proj_kernel(x_ref, g_ref, w_ref, gain_ref, gmat_ref, o_ref):
    x = x_ref[...]
    ms = jnp.mean(x * x, axis=-1, keepdims=True)
    xn = (x * lax.rsqrt(ms + EPS) * g_ref[...]).astype(BF16)
    for j in range(PROJ_WIDTH // PROJ_COL_TILE):
        cols = slice(j * PROJ_COL_TILE, (j + 1) * PROJ_COL_TILE)
        acc = jnp.dot(xn, w_ref[:, cols], preferred_element_type=F32)
        if j % 3 == 2:
            o_ref[:, cols] = acc.astype(BF16)
        else:
            sq = (acc * acc).astype(BF16)
            ss = jnp.concatenate(
                [jnp.dot(sq[:, c:c + MXU_DIM], gmat_ref[...], preferred_element_type=F32)
                 for c in range(0, PROJ_COL_TILE, MXU_DIM)], axis=1)
            o_ref[:, cols] = (acc * lax.rsqrt(ss * (1.0 / HEAD_DIM) + EPS) * gain_ref[:, cols]).astype(BF16)


def _proj_call(x2d, g, w_bf16, gains, gmat):
    n = x2d.shape[0]
    return pl.pallas_call(
        _proj_kernel,
        grid=(n // ROW_TILE,),
        in_specs=[
            pl.BlockSpec((ROW_TILE, D_MODEL), lambda i: (i, 0)),
            _resident((1, D_MODEL)),
            _resident((D_MODEL, PROJ_WIDTH)),
            _resident((1, PROJ_WIDTH)),
            _resident((MXU_DIM, MXU_DIM)),
        ],
        out_specs=pl.BlockSpec((ROW_TILE, PROJ_WIDTH), lambda i: (i, 0)),
        out_shape=jax.ShapeDtypeStruct((n, PROJ_WIDTH), BF16),
        compiler_params=_cparams(("parallel",)),
        name="norm_in_proj",
    )(x2d, g, w_bf16, gains, gmat)


def _deinterleave(x, dil):
    if dil == 1:
        return x
    return jnp.swapaxes(x.reshape(SEQ // dil, dil, LANES), 0, 1).reshape(SEQ, LANES)


def _interleave(x, dil):
    if dil == 1:
        return x
    return jnp.swapaxes(x.reshape(dil, SEQ // dil, LANES), 0, 1).reshape(SEQ, LANES)


def _mixer_a_kernel(q_ref, k_ref, v_ref, bias_ref, o_ref, qp, kp, vp, out_s, lse_s):
    n_tiles = SEQ // DIL_BLOCK
    n_branch = len(DIL_PATTERNS)
    lane_full = lax.broadcasted_iota(jnp.int32, (SEQ, LANES), 1)
    lane = lax.broadcasted_iota(jnp.int32, (DIL_BLOCK, LANES), 1)
    head0 = lane < HEAD_DIM

    q = q_ref[0].astype(F32)
    k = k_ref[0].astype(F32)
    v = v_ref[0].astype(F32)
    pad = jnp.zeros((DIL_BLOCK, LANES), BF16)
    for bi, (_, dil) in enumerate(DIL_PATTERNS):
        qd = _deinterleave(q, dil)
        qp[bi, 0] = jnp.where(lane_full < HEAD_DIM, qd, 0.0).astype(BF16)
        qp[bi, 1] = jnp.where(lane_full >= HEAD_DIM, qd, 0.0).astype(BF16)
        kp[bi, :DIL_BLOCK, :] = pad
        vp[bi, :DIL_BLOCK, :] = pad
        kp[bi, DIL_BLOCK:, :] = _deinterleave(k, dil).astype(BF16)
        vp[bi, DIL_BLOCK:, :] = _deinterleave(v, dil).astype(BF16)

    for bi, (_, dil) in enumerate(DIL_PATTERNS):
        nb = SEQ // dil // DIL_BLOCK

        def tile(t, bi=bi, nb=nb):
            row0 = t * DIL_BLOCK if isinstance(t, int) else pl.multiple_of(t * DIL_BLOCK, DIL_BLOCK)
            rows = pl.ds(row0, DIL_BLOCK)
            keys = pl.ds(row0, 2 * DIL_BLOCK)
            tab = 2 * bi + (t % nb == 0)
            kt = kp[bi, keys, :]
            vt = jnp.concatenate([vp[bi, keys, :], jnp.ones((2 * DIL_BLOCK, LANES), BF16)], axis=1)
            m_h, pv_h = [], []
            for hh in range(2):
                s = lax.dot_general(qp[bi, hh, rows, :], kt, (((1,), (1,)), ((), ())),
                                    preferred_element_type=F32) + bias_ref[tab, hh]
                m = jnp.broadcast_to(jnp.max(s, axis=-1, keepdims=True), (DIL_BLOCK, LANES))
                p = jnp.exp2(s - jnp.concatenate([m, m], axis=1))
                m_h.append(m)
                pv_h.append(jnp.dot(p.astype(BF16), vt, preferred_element_type=F32))
            l = jnp.where(head0, pv_h[0][:, LANES:], pv_h[1][:, LANES:])
            out_s[bi, rows, :] = jnp.where(head0, pv_h[0][:, :LANES], pv_h[1][:, :LANES]) / l
            lse_s[bi, rows, :] = jnp.where(head0, m_h[0], m_h[1]) + jnp.log2(l)

        if DIL_GROUP == n_tiles:
            for t in range(n_tiles):
                tile(t)
        else:
            def group(g, carry, tile=tile):
                for u in range(DIL_GROUP):
                    tile(g * DIL_GROUP + u)
                return carry

            lax.fori_loop(0, n_tiles // DIL_GROUP, group, 0)

    lse_b = [_interleave(lse_s[bi], dil) for bi, (_, dil) in enumerate(DIL_PATTERNS)]
    lse_max = functools.reduce(jnp.maximum, lse_b)
    w_b = [jnp.exp2(lse - lse_max) for lse in lse_b]
    num = sum(w * _interleave(out_s[bi], dil) for w, (bi, (_, dil)) in zip(w_b, enumerate(DIL_PATTERNS)))
    o_ref[0] = (num / sum(w_b)).astype(o_ref.dtype)


def _mixer_a_call(proj3d, bias_a):
    b = proj3d.shape[0]
    n_pairs = N_HEADS_A // 2
    n_branch = len(DIL_PATTERNS)
    q_blk0, k_blk0, v_blk0 = 0, WIDTH_A // LANES, 2 * WIDTH_A // LANES
    return pl.pallas_call(
        _mixer_a_kernel,
        grid=(b, n_pairs),
        in_specs=[
            pl.BlockSpec((1, SEQ, LANES), lambda i, p: (i, 0, q_blk0 + p)),
            pl.BlockSpec((1, SEQ, LANES), lambda i, p: (i, 0, k_blk0 + p)),
            pl.BlockSpec((1, SEQ, LANES), lambda i, p: (i, 0, v_blk0 + p)),
            pl.BlockSpec((bias_a.shape[0], 2, DIL_BLOCK, 2 * DIL_BLOCK), lambda i, p: (0, p, 0, 0)),
        ],
        out_specs=pl.BlockSpec((1, SEQ, LANES), lambda i, p: (i, 0, p)),
        out_shape=jax.ShapeDtypeStruct((b, SEQ, WIDTH_A), BF16),
        scratch_shapes=[
            pltpu.VMEM((n_branch, 2, SEQ, LANES), BF16),
            pltpu.VMEM((n_branch, SEQ + DIL_BLOCK, LANES), BF16),
            pltpu.VMEM((n_branch, SEQ + DIL_BLOCK, LANES), BF16),
            pltpu.VMEM((n_branch, SEQ, LANES), F32),
            pltpu.VMEM((n_branch, SEQ, LANES), F32),
        ],
        compiler_params=_cparams(("parallel", "parallel")),
        name="dilated_attention",
    )(proj3d, proj3d, proj3d, bias_a)


def _mixer_b_kernel(far_ref, lam_ref, q_ref, k_ref, v_ref, bias_ref, subg_ref, o_ref, m_s, acc_s,
                    *, lam_init):
    pair = pl.program_id(1)
    i = pl.program_id(2)
    t = Q_TILE_B
    reps = t // LANES
    lane = lax.broadcasted_iota(jnp.int32, (t, LANES), 1)
    q_maps = []
    for hd in range(2):
        q = q_ref[0, :, hd * LANES:(hd + 1) * LANES]
        zero = jnp.zeros_like(q)
        q_maps.append((jnp.where(lane < HEAD_DIM, q, zero), jnp.where(lane >= HEAD_DIM, q, zero)))

    m_s[...] = jnp.full(m_s.shape, NEG, F32)
    acc_s[...] = jnp.zeros(acc_s.shape, F32)
    ones = jnp.ones((t, LANES), BF16)

    def step(j, bias_of_head, bias_is_scalar):
        koff = pl.multiple_of(j * t, t)
        for hd in range(2):
            kt = k_ref[0, pl.ds(koff, t), hd * LANES:(hd + 1) * LANES]
            vt = jnp.concatenate([v_ref[0, pl.ds(koff, t), hd * LANES:(hd + 1) * LANES], ones], axis=1)
            bias = bias_of_head(hd)
            for half in range(2):
                c = 2 * hd + half
                s = lax.dot_general(q_maps[hd][half], kt, (((1,), (1,)), ((), ())),
                                    preferred_element_type=F32)
                m_old = m_s[c]
                if bias_is_scalar:
                    m_new = jnp.maximum(m_old, jnp.max(s, axis=-1, keepdims=True) + bias)
                    shift = m_new - bias
                else:
                    s = s + bias
                    m_new = jnp.maximum(m_old, jnp.max(s, axis=-1, keepdims=True))
                    shift = m_new
                p = jnp.exp2(s - jnp.concatenate([shift] * reps, axis=1))
                alpha = jnp.exp2(m_old - m_new)
                acc_s[c] = (acc_s[c] * jnp.concatenate([alpha, alpha], axis=1)
                            + jnp.dot(p.astype(BF16), vt, preferred_element_type=F32))
                m_s[c] = m_new

    def far_step(j, carry):
        step(j, lambda hd: far_ref[2 * pair + hd] * LOG2E, True)
        return carry

    lax.fori_loop(0, jnp.maximum(i - 1, 0), far_step, 0)

    @pl.when(i > 0)
    def _():
        step(i - 1, lambda hd: bias_ref[1, hd], False)

    step(i, lambda hd: bias_ref[0, hd], False)

    lam_rows = lam_ref[...]
    e1 = jnp.exp(jnp.sum(lam_rows[0:1, :] * lam_rows[1:2, :], axis=-1, keepdims=True))
    e2 = jnp.exp(jnp.sum(lam_rows[2:3, :] * lam_rows[3:4, :], axis=-1, keepdims=True))
    lam = e1 - e2 + lam_init
    for hd in range(2):
        a1, a2 = acc_s[2 * hd], acc_s[2 * hd + 1]
        o = a1[:, :LANES] / a1[:, LANES:] - lam * (a2[:, :LANES] / a2[:, LANES:])
        ms = jnp.mean(o * o, axis=-1, keepdims=True)
        o_ref[0, :, hd * LANES:(hd + 1) * LANES] = (
            o * lax.rsqrt(ms + EPS) * subg_ref[...] * (1.0 - lam_init)).astype(o_ref.dtype)


def _mixer_b_call(proj3d, bias_b, far_b, lam_rows, subg, lam_init):
    b = proj3d.shape[0]
    width = 2 * LANES
    base = 3 * WIDTH_A // width
    q_blk0, k_blk0, v_blk0 = base, base + WIDTH_B // width, base + 2 * WIDTH_B // width
    t = Q_TILE_B
    grid_spec = pltpu.PrefetchScalarGridSpec(
        num_scalar_prefetch=1,
        grid=(b, N_HEADS_B // 2, SEQ // t),
        in_specs=[
            pl.BlockSpec((8, LANES), lambda bi, p, i, far: (0, 0)),
            pl.BlockSpec((1, t, width), lambda bi, p, i, far: (bi, i, q_blk0 + p)),
            pl.BlockSpec((1, SEQ, width), lambda bi, p, i, far: (bi, 0, k_blk0 + p)),
            pl.BlockSpec((1, SEQ, width), lambda bi, p, i, far: (bi, 0, v_blk0 + p)),
            pl.BlockSpec((2, 2, t, t), lambda bi, p, i, far: (0, p, 0, 0)),
            pl.BlockSpec((1, LANES), lambda bi, p, i, far: (0, 0)),
        ],
        out_specs=pl.BlockSpec((1, t, width), lambda bi, p, i, far: (bi, i, p)),
        scratch_shapes=[pltpu.VMEM((4, t, LANES), F32), pltpu.VMEM((4, t, 2 * LANES), F32)],
    )
    return pl.pallas_call(
        functools.partial(_mixer_b_kernel, lam_init=lam_init),
        grid_spec=grid_spec,
        out_shape=jax.ShapeDtypeStruct((b, SEQ, WIDTH_B), BF16),
        compiler_params=_cparams(("parallel", "parallel", "arbitrary")),
        name="diff_attention",
    )(far_b, lam_rows, proj3d, proj3d, proj3d, bias_b, subg)


def _out_proj_kernel(oa_ref, ob_ref, w_ref, x_ref, g_ref, *rest, with_router):
    if with_router:
        wr_ref, x1_ref, h_ref, route_ref = rest
    else:
        x1_ref, h_ref = rest
    acc = jnp.dot(oa_ref[...], w_ref[:WIDTH_A, :], preferred_element_type=F32)
    acc = acc + jnp.dot(ob_ref[...], w_ref[WIDTH_A:, :], preferred_element_type=F32)
    x1 = x_ref[...] + acc
    x1_ref[...] = x1
    ms = jnp.mean(x1 * x1, axis=-1, keepdims=True)
    hn = x1 * lax.rsqrt(ms + EPS) * g_ref[...]
    h_ref[...] = _to_slabs(hn) if with_router else hn.astype(h_ref.dtype)
    if with_router:
        hi = hn.astype(BF16)
        lo = (hn - hi.astype(F32)).astype(BF16)
        lg = (jnp.dot(hi, wr_ref[...], preferred_element_type=F32)
              + jnp.dot(lo, wr_ref[...], preferred_element_type=F32))
        lg = lg + pltpu.roll(lg, LANES - N_EXPERTS, 1)
        lane = lax.broadcasted_iota(jnp.int32, lg.shape, 1)
        lane_f = lane.astype(F32)
        lg = jnp.where(lane < N_EXPERTS, lg, -jnp.inf)
        v1 = jnp.max(lg, axis=-1, keepdims=True)
        i1 = jnp.min(jnp.where(lg == v1, lane_f, float(LANES)), axis=-1, keepdims=True)
        lg2 = jnp.where(lane_f == i1, -jnp.inf, lg)
        v2 = jnp.max(lg2, axis=-1, keepdims=True)
        i2 = jnp.min(jnp.where(lg2 == v2, lane_f, float(LANES)), axis=-1, keepdims=True)
        e = jnp.exp(v2 - v1)
        g1 = 1.0 / (1.0 + e)
        g2 = e / (1.0 + e)
        route_ref[...] = jnp.where(lane == 0, i1, jnp.where(lane == 1, i2,
                                   jnp.where(lane == 2, g1, jnp.where(lane == 3, g2, 0.0))))


def _out_proj_call(oa, ob, w_bf16, x2d, g, w_router_packed):
    n = x2d.shape[0]
    with_router = w_router_packed is not None
    row = lambda i: (i, 0)
    if with_router:
        h_spec = pl.BlockSpec((ROW_TILE, *ROW_SLAB), lambda i: (i, 0, 0))
        h_shape = jax.ShapeDtypeStruct((n, *ROW_SLAB), F32)
    else:
        h_spec = pl.BlockSpec((ROW_TILE, D_MODEL), row)
        h_shape = jax.ShapeDtypeStruct((n, D_MODEL), BF16)
    in_specs = [
        pl.BlockSpec((ROW_TILE, WIDTH_A), row),
        pl.BlockSpec((ROW_TILE, WIDTH_B), row),
        _resident((WIDTH_A + WIDTH_B, D_MODEL)),
        pl.BlockSpec((ROW_TILE, D_MODEL), row),
        _resident((1, D_MODEL)),
    ]
    out_specs = [pl.BlockSpec((ROW_TILE, D_MODEL), row), h_spec]
    out_shape = [jax.ShapeDtypeStruct((n, D_MODEL), F32), h_shape]
    args = [oa, ob, w_bf16, x2d, g]
    if with_router:
        in_specs.append(_resident((D_MODEL, LANES)))
        out_specs.append(pl.BlockSpec((ROW_TILE, LANES), row))
        out_shape.append(jax.ShapeDtypeStruct((n, LANES), F32))
        args.append(w_router_packed)
    return pl.pallas_call(
        functools.partial(_out_proj_kernel, with_router=with_router),
        grid=(n // ROW_TILE,),
        in_specs=in_specs,
        out_specs=out_specs,
        out_shape=out_shape,
        compiler_params=_cparams(("parallel",)),
        name="out_proj_router" if with_router else "out_proj",
    )(*args)


def _swiglu_act(g, u):
    return (g / (1.0 + jnp.exp(-g))) * u


def _ffn_kernel(h_ref, x_ref, wg_ref, wu_ref, wd_ref, o_ref, a_ref):
    h = h_ref[...]
    c0 = 0
    for width in FFN_CHUNKS:
        g = jnp.dot(h, wg_ref[:, c0:c0 + width], preferred_element_type=F32)
        u = jnp.dot(h, wu_ref[:, c0:c0 + width], preferred_element_type=F32)
        a_ref[:, c0:c0 + width] = _swiglu_act(g, u).astype(BF16)
        c0 += width
    o_ref[...] = x_ref[...] + jnp.dot(a_ref[...], wd_ref[...], preferred_element_type=F32)


def _ffn_call(h, x2d, wg, wu, wd):
    n = x2d.shape[0]
    row = lambda i: (i, 0)
    return pl.pallas_call(
        _ffn_kernel,
        grid=(n // ROW_TILE,),
        in_specs=[
            pl.BlockSpec((ROW_TILE, D_MODEL), row),
            pl.BlockSpec((ROW_TILE, D_MODEL), row),
            _resident((D_MODEL, D_FF)),
            _resident((D_MODEL, D_FF)),
            _resident((D_FF, D_MODEL)),
        ],
        out_specs=pl.BlockSpec((ROW_TILE, D_MODEL), row),
        out_shape=jax.ShapeDtypeStruct((n, D_MODEL), F32),
        scratch_shapes=[pltpu.VMEM((ROW_TILE, D_FF), BF16)],
        compiler_params=_cparams(("parallel",)),
        name="dense_swiglu",
    )(h, x2d, wg, wu, wd)


ROW_SLAB = (8, LANES)
assert ROW_SLAB[0] * ROW_SLAB[1] == D_MODEL


def _to_slabs(x):
    r = x.shape[0]
    xs = jnp.stack([x[:, LANES * j:LANES * (j + 1)] for j in range(ROW_SLAB[0])], axis=0)
    xs = xs.reshape(ROW_SLAB[0], r // 8, 8, LANES)
    xs = jnp.swapaxes(jnp.swapaxes(xs, 0, 1), 1, 2)
    return xs.reshape(r, *ROW_SLAB)


def _from_slabs(t):
    r = t.shape[0]
    y = jnp.swapaxes(t.reshape(r // 8, 8, *ROW_SLAB), 1, 2)
    return jnp.concatenate([y[:, j].reshape(r, LANES) for j in range(ROW_SLAB[0])], axis=1)


def _row_copy(src_ref, src_row, dst_ref, dst_row, sem):
    return pltpu.make_async_copy(src_ref.at[src_row], dst_ref.at[dst_row], sem)


def _dispatch_kernel(fill_ref, dest_ref, h_ref, o_hbm, zero_ref, sem):
    i = pl.program_id(0)

    def start(t, c):
        for k in range(TOP_K):
            _row_copy(h_ref, t, o_hbm, dest_ref[0, 0, TOP_K * t + k], sem).start()
        return c

    def wait(t, c):
        for k in range(TOP_K):
            _row_copy(h_ref, t, o_hbm, 0, sem).wait()
        return c

    lax.fori_loop(0, DISPATCH_TILE, start, 0, unroll=4)
    lax.fori_loop(0, DISPATCH_TILE, wait, 0, unroll=4)

    @pl.when(i == pl.num_programs(0) - 1)
    def _():
        zero_ref[...] = jnp.zeros_like(zero_ref)
        for e in range(N_EXPERTS):
            lo, hi = fill_ref[e], fill_ref[N_EXPERTS + e]
            lax.fori_loop(lo, hi, lambda r, c: (_row_copy(zero_ref, 0, o_hbm, r, sem).start(), c)[1], 0)
            lax.fori_loop(lo, hi, lambda r, c: (_row_copy(zero_ref, 0, o_hbm, 0, sem).wait(), c)[1], 0)


def _dispatch_call(fill, dest3d, h, n_rows):
    n = h.shape[0]
    grid_spec = pltpu.PrefetchScalarGridSpec(
        num_scalar_prefetch=1,
        grid=(n // DISPATCH_TILE,),
        in_specs=[
            pl.BlockSpec((1, 1, TOP_K * DISPATCH_TILE), lambda i, f: (i, 0, 0), memory_space=pltpu.SMEM),
            pl.BlockSpec((DISPATCH_TILE, *ROW_SLAB), lambda i, f: (i, 0, 0)),
        ],
        out_specs=pl.BlockSpec(memory_space=pl.ANY),
        scratch_shapes=[pltpu.VMEM((1, *ROW_SLAB), h.dtype), pltpu.SemaphoreType.DMA(())],
    )
    return pl.pallas_call(
        _dispatch_kernel,
        grid_spec=grid_spec,
        out_shape=jax.ShapeDtypeStruct((n_rows, *ROW_SLAB), h.dtype),
        compiler_params=_cparams(("arbitrary",)),
        name="moe_dispatch",
    )(fill, dest3d, h)


def _moe_kernel(be_ref, nact_ref, x_ref, wg_ref, wu_ref, wd_ref, o_ref, xb_ref, acc_ref):
    b = pl.program_id(0)
    c = pl.program_id(1)
    last = pl.num_programs(1) - 1
    active = b < nact_ref[0]

    @pl.when(active)
    def _():
        @pl.when(c == 0)
        def _():
            xb_ref[...] = _from_slabs(x_ref[...]).astype(BF16)

        xb = xb_ref[...]
        g = jnp.dot(xb, wg_ref[0], preferred_element_type=F32)
        u = jnp.dot(xb, wu_ref[0], preferred_element_type=F32)
        contrib = jnp.dot(_swiglu_act(g, u).astype(BF16), wd_ref[0], preferred_element_type=F32)

        @pl.when(c == 0)
        def _():
            acc_ref[...] = contrib

        @pl.when(c > 0)
        def _():
            acc_ref[...] += contrib

        @pl.when(c == last)
        def _():
            o_ref[...] = _to_slabs(acc_ref[...])

    @pl.when(jnp.logical_not(active) & (c == last))
    def _():
        o_ref[...] = jnp.zeros_like(o_ref)


def _moe_call(block_expert, n_active, x_pad, wg, wu, wd):
    n_rows = x_pad.shape[0]
    n_blocks = n_rows // MOE_BLOCK
    n_chunks = D_FF_EXPERT // MOE_FF_TILE
    grid_spec = pltpu.PrefetchScalarGridSpec(
        num_scalar_prefetch=2,
        grid=(n_blocks, n_chunks),
        in_specs=[
            pl.BlockSpec((MOE_BLOCK, *ROW_SLAB), lambda b, c, be, na: (b, 0, 0)),
            pl.BlockSpec((1, D_MODEL, MOE_FF_TILE), lambda b, c, be, na: (be[b], 0, c)),
            pl.BlockSpec((1, D_MODEL, MOE_FF_TILE), lambda b, c, be, na: (be[b], 0, c)),
            pl.BlockSpec((1, MOE_FF_TILE, D_MODEL), lambda b, c, be, na: (be[b], c, 0)),
        ],
        out_specs=pl.BlockSpec((MOE_BLOCK, *ROW_SLAB), lambda b, c, be, na: (b, 0, 0)),
        scratch_shapes=[pltpu.VMEM((MOE_BLOCK, D_MODEL), BF16), pltpu.VMEM((MOE_BLOCK, D_MODEL), F32)],
    )
    return pl.pallas_call(
        _moe_kernel,
        grid_spec=grid_spec,
        out_shape=jax.ShapeDtypeStruct((n_rows, *ROW_SLAB), F32),
        compiler_params=_cparams(("arbitrary", "arbitrary")),
        name="moe_swiglu",
    )(block_expert, n_active, x_pad, wg, wu, wd)


def _combine_kernel(pos_ref, y_hbm, x_ref, route_ref, o_ref, buf, sem):
    def start(t, c):
        for k in range(TOP_K):
            _row_copy(y_hbm, pos_ref[0, 0, TOP_K * t + k], buf.at[k], t, sem).start()
        return c

    def wait(t, c):
        for k in range(TOP_K):
            _row_copy(y_hbm, 0, buf.at[k], t, sem).wait()
        return c

    lax.fori_loop(0, COMBINE_TILE, start, 0, unroll=4)
    lax.fori_loop(0, COMBINE_TILE, wait, 0, unroll=4)
    route = route_ref[...]
    g0 = route[:, TOP_K:TOP_K + 1]
    g1 = route[:, TOP_K + 1:TOP_K + 2]
    o_ref[...] = x_ref[...] + g0 * _from_slabs(buf[0]) + g1 * _from_slabs(buf[1])


def _combine_call(pos3d, y_pad, x2d, route):
    n = x2d.shape[0]
    row = lambda i: (i, 0)
    return pl.pallas_call(
        _combine_kernel,
        grid=(n // COMBINE_TILE,),
        in_specs=[
            pl.BlockSpec((1, 1, TOP_K * COMBINE_TILE), lambda i: (i, 0, 0), memory_space=pltpu.SMEM),
            pl.BlockSpec(memory_space=pl.ANY),
            pl.BlockSpec((COMBINE_TILE, D_MODEL), row),
            pl.BlockSpec((COMBINE_TILE, LANES), row),
        ],
        out_specs=pl.BlockSpec((COMBINE_TILE, D_MODEL), row),
        out_shape=jax.ShapeDtypeStruct((n, D_MODEL), F32),
        scratch_shapes=[pltpu.VMEM((TOP_K, COMBINE_TILE, *ROW_SLAB), F32), pltpu.SemaphoreType.DMA(())],
        compiler_params=_cparams(("arbitrary",)),
        name="moe_combine",
    )(pos3d, y_pad, x2d, route)


def _moe_layer(x1, hn, route, wg, wu, wd):
    n = x1.shape[0]
    e_flat = route[:, :TOP_K].astype(jnp.int32).reshape(-1)
    onehot = (e_flat[:, None] == jnp.arange(N_EXPERTS, dtype=jnp.int32)[None, :]).astype(jnp.int32)
    csum = jnp.cumsum(onehot, axis=0)
    counts = csum[-1]
    padded = (counts + MOE_BLOCK - 1) // MOE_BLOCK * MOE_BLOCK
    pad_ends = jnp.cumsum(padded)
    pad_starts = pad_ends - padded
    dest = jnp.sum((csum - onehot + pad_starts[None, :]) * onehot, axis=1)
    n_blocks = -(-(n * TOP_K) // MOE_BLOCK) + N_EXPERTS
    n_rows = n_blocks * MOE_BLOCK
    block_start = jnp.arange(n_blocks, dtype=jnp.int32) * MOE_BLOCK
    block_expert = jnp.minimum(
        jnp.sum((pad_ends[None, :] <= block_start[:, None]).astype(jnp.int32), axis=1), N_EXPERTS - 1)
    n_active = (pad_ends[-1:] // MOE_BLOCK).astype(jnp.int32)
    fill = jnp.concatenate([pad_starts + counts, pad_starts[1:], jnp.full((1,), n_rows, jnp.int32)])

    x_pad = _dispatch_call(fill.astype(jnp.int32), dest.reshape(n // DISPATCH_TILE, 1, TOP_K * DISPATCH_TILE),
                           hn, n_rows)
    y_pad = _moe_call(block_expert.astype(jnp.int32), n_active, x_pad, wg, wu, wd)
    pos3d = dest.reshape(n // COMBINE_TILE, 1, TOP_K * COMBINE_TILE)
    return _combine_call(pos3d, y_pad, x1, route)


def _head_group_matrix():
    idx = np.arange(MXU_DIM) // HEAD_DIM
    return jnp.asarray((idx[:, None] == idx[None, :]).astype(np.float32), dtype=BF16)


def _pack_router(w_router):
    hi = w_router.astype(BF16)
    lo = (w_router - hi.astype(F32)).astype(BF16)
    pad = jnp.zeros((D_MODEL, LANES - 2 * N_EXPERTS), BF16)
    return jnp.concatenate([hi, lo, pad], axis=1)


def kernel(x, norm_mix_g, norm_ffn_g, w_in, w_out, q_norm_a, k_norm_a, q_norm_b, k_norm_b,
           lambda_q1, lambda_k1, lambda_q2, lambda_k2, subln_g, rel_bias,
           w_gate_dense, w_up_dense, w_down_dense, w_router, w_gate_moe, w_up_moe, w_down_moe):
    b, s, d = x.shape
    assert (s, d) == (SEQ, D_MODEL)
    depth = w_in.shape[0]
    n = b * s
    x2d = x.reshape(n, d)
    gmat = _head_group_matrix()
    bias_a = _bias_tables(rel_bias, _dilated_buckets(), 0, N_HEADS_A)
    bias_b = _bias_tables(rel_bias, _diff_buckets(), N_HEADS_A, N_HEADS_B)
    far_b = rel_bias[_far_bucket(), N_HEADS_A:]
    ones = jnp.ones((PROJ_COL_TILE,), F32)
    rep = PROJ_COL_TILE // HEAD_DIM

    for layer in range(depth):
        gains = jnp.concatenate([
            jnp.tile(q_norm_a[layer], rep) * (ATTN_SCALE * LOG2E), jnp.tile(k_norm_a[layer], rep), ones,
            jnp.tile(q_norm_b[layer], rep) * (ATTN_SCALE * LOG2E), jnp.tile(k_norm_b[layer], rep), ones,
        ]).reshape(1, PROJ_WIDTH)
        proj = _proj_call(x2d, norm_mix_g[layer].reshape(1, d), w_in[layer].astype(BF16), gains, gmat)
        proj3d = proj.reshape(b, s, PROJ_WIDTH)

        oa = _mixer_a_call(proj3d, bias_a)
        lam_init = 0.8 - 0.6 * math.exp(-0.3 * layer)
        lam_rows = jnp.zeros((8, LANES), F32).at[:4, :HEAD_DIM].set(
            jnp.stack([lambda_q1[layer], lambda_k1[layer], lambda_q2[layer], lambda_k2[layer]]))
        ob = _mixer_b_call(proj3d, bias_b, far_b, lam_rows, subln_g[layer].reshape(1, LANES), lam_init)

        is_moe = layer % 2 == 1
        i = layer // 2
        outs = _out_proj_call(
            oa.reshape(n, WIDTH_A), ob.reshape(n, WIDTH_B), w_out[layer].astype(BF16), x2d,
            norm_ffn_g[layer].reshape(1, d), _pack_router(w_router[i]) if is_moe else None)
        if is_moe:
            x1, hn, route = outs
            x2d = _moe_layer(x1, hn, route, w_gate_moe[i].astype(BF16), w_up_moe[i].astype(BF16),
                             w_down_moe[i].astype(BF16))
        else:
            x1, hn = outs
            x2d = _ffn_call(hn, x1, w_gate_dense[i].astype(BF16), w_up_dense[i].astype(BF16),
                            w_down_dense[i].astype(BF16))
    return x2d.reshape(b, s, d)
```

```python
import functools
import math

import numpy as np
import jax
import jax.numpy as jnp
from jax import lax
from jax.experimental import pallas as pl
from jax.experimental.pallas import tpu as pltpu

F32 = jnp.float32
BF16 = jnp.bfloat16

D_MODEL = 1024
SEQ = 2048
HEAD_DIM = 64
ATTN_SCALE = HEAD_DIM ** -0.5
LOG2E = math.log2(math.e)
N_HEADS_A = 8
DIL_PATTERNS = ((128, 1), (512, 4), (2048, 16))
DIL_BLOCK = 128
N_HEADS_B = 4
WIDTH_A = N_HEADS_A * HEAD_DIM
WIDTH_B = N_HEADS_B * 2 * HEAD_DIM
PROJ_WIDTH = 3 * (WIDTH_A + WIDTH_B)
N_BUCKETS = 32
MAX_DISTANCE = 128
D_FF = 2816
N_EXPERTS = 8
TOP_K = 2
D_FF_EXPERT = 3584
EPS = 1e-6
NEG = -1e30

LANES = 128
MXU_DIM = 256
VMEM_LIMIT = 56 * 1024 * 1024

ROW_TILE = 512
PROJ_COL_TILE = 512
Q_TILE_B = 512
DIL_GROUP = 16
MOE_BLOCK = 512
MOE_FF_TILE = 1792
FFN_CHUNKS = (512, 512, 512, 512, 512, 256)
DISPATCH_TILE = 256
COMBINE_TILE = 256


def _cparams(sem):
    return pltpu.CompilerParams(dimension_semantics=sem, vmem_limit_bytes=VMEM_LIMIT)


def _resident(shape):
    nd = len(shape)
    return pl.BlockSpec(shape, lambda *_: (0,) * nd, pipeline_mode=pl.Buffered(1))


def _t5_bucket_np(dist):
    n = np.maximum(dist, 0)
    max_exact = N_BUCKETS // 2
    large = max_exact + (np.log(np.maximum(n, 1).astype(np.float32) / max_exact)
                         / math.log(MAX_DISTANCE / max_exact) * (N_BUCKETS - max_exact)).astype(np.int32)
    large = np.minimum(large, N_BUCKETS - 1)
    return np.where(n < max_exact, n, large).astype(np.int32)


def _dilated_buckets():
    i = np.arange(DIL_BLOCK)[:, None]
    j = np.arange(2 * DIL_BLOCK)[None, :]
    steps = DIL_BLOCK + i - j
    valid = (steps >= 0) & (steps <= DIL_BLOCK)
    tabs = []
    for _, dil in DIL_PATTERNS:
        bucket = np.where(valid, _t5_bucket_np(steps * dil), -1)
        tabs.append(bucket)
        tabs.append(np.where(j >= DIL_BLOCK, bucket, -1))
    return np.stack(tabs).astype(np.int32)


def _diff_buckets():
    t = Q_TILE_B
    qi = np.arange(t)[:, None]
    kj = np.arange(t)[None, :]
    tabs = []
    for off in (0, t):
        dist = qi - kj + off
        tabs.append(np.where(dist >= 0, _t5_bucket_np(dist), -1))
    return np.stack(tabs).astype(np.int32)


def _far_bucket():
    b = _t5_bucket_np(np.arange(Q_TILE_B + 1, SEQ + 1))
    assert (b == b[0]).all()
    return int(b[0])


def _bias_table_kernel(rel_ref, bucket_ref, o_ref, *, head0):
    col = head0 + pl.program_id(1)
    bucket = bucket_ref[0]
    val = jnp.full(bucket.shape, NEG, F32)
    for b in range(N_BUCKETS):
        val = jnp.where(bucket == b, rel_ref[b, col] * LOG2E, val)
    o_ref[0, 0] = val


def _bias_tables(rel_bias, buckets, head0, n_heads):
    n_tab, rows, cols = buckets.shape
    return pl.pallas_call(
        functools.partial(_bias_table_kernel, head0=head0),
        grid=(n_tab, n_heads),
        in_specs=[
            pl.BlockSpec(memory_space=pltpu.SMEM),
            pl.BlockSpec((1, rows, cols), lambda t, h: (t, 0, 0)),
        ],
        out_specs=pl.BlockSpec((1, 1, rows, cols), lambda t, h: (t, h, 0, 0)),
        out_shape=jax.ShapeDtypeStruct((n_tab, n_heads, rows, cols), F32),
        compiler_params=_cparams(("parallel", "parallel")),
        name="bias_tables",
    )(rel_bias, jnp.asarray(buckets))


def _proj_kernel(x_ref, g_ref, w_ref, gain_ref, gmat_ref, o_ref):
    x = x_ref[...]
    ms = jnp.mean(x * x, axis=-1, keepdims=True)
    xn = (x * lax.rsqrt(ms + EPS) * g_ref[...]).astype(BF16)
    for j in range(PROJ_WIDTH // PROJ_COL_TILE):
        cols = slice(j * PROJ_COL_TILE, (j + 1) * PROJ_COL_TILE)
        acc = jnp.dot(xn, w_ref[:, cols], preferred_element_type=F32)
        if j % 3 == 2:
            o_ref[:, cols] = acc.astype(BF16)
        else:
            sq = (acc * acc).astype(BF16)
            ss = jnp.concatenate(
                [jnp.dot(sq[:, c:c + MXU_DIM], gmat_ref[...], preferred_element_type=F32)
                 for c in range(0, PROJ_COL_TILE, MXU_DIM)], axis=1)
            o_ref[:, cols] = (acc * lax.rsqrt(ss * (1.0 / HEAD_DIM) + EPS) * gain_ref[:, cols]).astype(BF16)


def _proj_call(x2d, g, w_bf16, gains, gmat):
    n = x2d.shape[0]
    return pl.pallas_call(
        _proj_kernel,
        grid=(n // ROW_TILE,),
        in_specs=[
            pl.BlockSpec((ROW_TILE, D_MODEL), lambda i: (i, 0)),
            _resident((1, D_MODEL)),
            _resident((D_MODEL, PROJ_WIDTH)),
            _resident((1, PROJ_WIDTH)),
            _resident((MXU_DIM, MXU_DIM)),
        ],
        out_specs=pl.BlockSpec((ROW_TILE, PROJ_WIDTH), lambda i: (i, 0)),
        out_shape=jax.ShapeDtypeStruct((n, PROJ_WIDTH), BF16),
        compiler_params=_cparams(("parallel",)),
        name="norm_in_proj",
    )(x2d, g, w_bf16, gains, gmat)


def _deinterleave(x, dil):
    if dil == 1:
        return x
    return jnp.swapaxes(x.reshape(SEQ // dil, dil, LANES), 0, 1).reshape(SEQ, LANES)


def _interleave(x, dil):
    if dil == 1:
        return x
    return jnp.swapaxes(x.reshape(dil, SEQ // dil, LANES), 0, 1).reshape(SEQ, LANES)


def _mixer_a_kernel(q_ref, k_ref, v_ref, bias_ref, o_ref, qp, kp, vp, out_s, lse_s):
    n_tiles = SEQ // DIL_BLOCK
    n_branch = len(DIL_PATTERNS)
    lane_full = lax.broadcasted_iota(jnp.int32, (SEQ, LANES), 1)
    lane = lax.broadcasted_iota(jnp.int32, (DIL_BLOCK, LANES), 1)
    head0 = lane < HEAD_DIM

    q = q_ref[0].astype(F32)
    k = k_ref[0].astype(F32)
    v = v_ref[0].astype(F32)
    pad = jnp.zeros((DIL_BLOCK, LANES), BF16)
    for bi, (_, dil) in enumerate(DIL_PATTERNS):
        qd = _deinterleave(q, dil)
        qp[bi, 0] = jnp.where(lane_full < HEAD_DIM, qd, 0.0).astype(BF16)
        qp[bi, 1] = jnp.where(lane_full >= HEAD_DIM, qd, 0.0).astype(BF16)
        kp[bi, :DIL_BLOCK, :] = pad
        vp[bi, :DIL_BLOCK, :] = pad
        kp[bi, DIL_BLOCK:, :] = _deinterleave(k, dil).astype(BF16)
        vp[bi, DIL_BLOCK:, :] = _deinterleave(v, dil).astype(BF16)

    for bi, (_, dil) in enumerate(DIL_PATTERNS):
        nb = SEQ // dil // DIL_BLOCK

        def tile(t, bi=bi, nb=nb):
            row0 = t * DIL_BLOCK if isinstance(t, int) else pl.multiple_of(t * DIL_BLOCK, DIL_BLOCK)
            rows = pl.ds(row0, DIL_BLOCK)
            keys = pl.ds(row0, 2 * DIL_BLOCK)
            tab = 2 * bi + (t % nb == 0)
            kt = kp[bi, keys, :]
            vt = jnp.concatenate([vp[bi, keys, :], jnp.ones((2 * DIL_BLOCK, LANES), BF16)], axis=1)
            m_h, pv_h = [], []
            for hh in range(2):
                s = lax.dot_general(qp[bi, hh, rows, :], kt, (((1,), (1,)), ((), ())),
                                    preferred_element_type=F32) + bias_ref[tab, hh]
                m = jnp.broadcast_to(jnp.max(s, axis=-1, keepdims=True), (DIL_BLOCK, LANES))
                p = jnp.exp2(s - jnp.concatenate([m, m], axis=1))
                m_h.append(m)
                pv_h.append(jnp.dot(p.astype(BF16), vt, preferred_element_type=F32))
            l = jnp.where(head0, pv_h[0][:, LANES:], pv_h[1][:, LANES:])
            out_s[bi, rows, :] = jnp.where(head0, pv_h[0][:, :LANES], pv_h[1][:, :LANES]) / l
            lse_s[bi, rows, :] = jnp.where(head0, m_h[0], m_h[1]) + jnp.log2(l)

        if DIL_GROUP == n_tiles:
            for t in range(n_tiles):
                tile(t)
        else:
            def group(g, carry, tile=tile):
                for u in range(DIL_GROUP):
                    tile(g * DIL_GROUP + u)
                return carry

            lax.fori_loop(0, n_tiles // DIL_GROUP, group, 0)

    lse_b = [_interleave(lse_s[bi], dil) for bi, (_, dil) in enumerate(DIL_PATTERNS)]
    lse_max = functools.reduce(jnp.maximum, lse_b)
    w_b = [jnp.exp2(lse - lse_max) for lse in lse_b]
    num = sum(w * _interleave(out_s[bi], dil) for w, (bi, (_, dil)) in zip(w_b, enumerate(DIL_PATTERNS)))
    o_ref[0] = (num / sum(w_b)).astype(o_ref.dtype)


def _mixer_a_call(proj3d, bias_a):
    b = proj3d.shape[0]
    n_pairs = N_HEADS_A // 2
    n_branch = len(DIL_PATTERNS)
    q_blk0, k_blk0, v_blk0 = 0, WIDTH_A // LANES, 2 * WIDTH_A // LANES
    return pl.pallas_call(
        _mixer_a_kernel,
        grid=(b, n_pairs),
        in_specs=[
            pl.BlockSpec((1, SEQ, LANES), lambda i, p: (i, 0, q_blk0 + p)),
            pl.BlockSpec((1, SEQ, LANES), lambda i, p: (i, 0, k_blk0 + p)),
            pl.BlockSpec((1, SEQ, LANES), lambda i, p: (i, 0, v_blk0 + p)),
            pl.BlockSpec((bias_a.shape[0], 2, DIL_BLOCK, 2 * DIL_BLOCK), lambda i, p: (0, p, 0, 0)),
        ],
        out_specs=pl.BlockSpec((1, SEQ, LANES), lambda i, p: (i, 0, p)),
        out_shape=jax.ShapeDtypeStruct((b, SEQ, WIDTH_A), BF16),
        scratch_shapes=[
            pltpu.VMEM((n_branch, 2, SEQ, LANES), BF16),
            pltpu.VMEM((n_branch, SEQ + DIL_BLOCK, LANES), BF16),
            pltpu.VMEM((n_branch, SEQ + DIL_BLOCK, LANES), BF16),
            pltpu.VMEM((n_branch, SEQ, LANES), F32),
            pltpu.VMEM((n_branch, SEQ, LANES), F32),
        ],
        compiler_params=_cparams(("parallel", "parallel")),
        name="dilated_attention",
    )(proj3d, proj3d, proj3d, bias_a)


def _mixer_b_kernel(far_ref, lam_ref, q_ref, k_ref, v_ref, bias_ref, subg_ref, o_ref, m_s, acc_s,
                    *, lam_init):
    pair = pl.program_id(1)
    i = pl.program_id(2)
    t = Q_TILE_B
    reps = t // LANES
    lane = lax.broadcasted_iota(jnp.int32, (t, LANES), 1)
    q_maps = []
    for hd in range(2):
        q = q_ref[0, :, hd * LANES:(hd + 1) * LANES]
        zero = jnp.zeros_like(q)
        q_maps.append((jnp.where(lane < HEAD_DIM, q, zero), jnp.where(lane >= HEAD_DIM, q, zero)))

    m_s[...] = jnp.full(m_s.shape, NEG, F32)
    acc_s[...] = jnp.zeros(acc_s.shape, F32)
    ones = jnp.ones((t, LANES), BF16)

    def step(j, bias_of_head, bias_is_scalar):
        koff = pl.multiple_of(j * t, t)
        for hd in range(2):
            kt = k_ref[0, pl.ds(koff, t), hd * LANES:(hd + 1) * LANES]
            vt = jnp.concatenate([v_ref[0, pl.ds(koff, t), hd * LANES:(hd + 1) * LANES], ones], axis=1)
            bias = bias_of_head(hd)
            for half in range(2):
                c = 2 * hd + half
                s = lax.dot_general(q_maps[hd][half], kt, (((1,), (1,)), ((), ())),
                                    preferred_element_type=F32)
                m_old = m_s[c]
                if bias_is_scalar:
                    m_new = jnp.maximum(m_old, jnp.max(s, axis=-1, keepdims=True) + bias)
                    shift = m_new - bias
                else:
                    s = s + bias
                    m_new = jnp.maximum(m_old, jnp.max(s, axis=-1, keepdims=True))
                    shift = m_new
                p = jnp.exp2(s - jnp.concatenate([shift] * reps, axis=1))
                alpha = jnp.exp2(m_old - m_new)
                acc_s[c] = (acc_s[c] * jnp.concatenate([alpha, alpha], axis=1)
                            + jnp.dot(p.astype(BF16), vt, preferred_element_type=F32))
                m_s[c] = m_new

    def far_step(j, carry):
        step(j, lambda hd: far_ref[2 * pair + hd] * LOG2E, True)
        return carry

    lax.fori_loop(0, jnp.maximum(i - 1, 0), far_step, 0)

    @pl.when(i > 0)
    def _():
        step(i - 1, lambda hd: bias_ref[1, hd], False)

    step(i, lambda hd: bias_ref[0, hd], False)

    lam_rows = lam_ref[...]
    e1 = jnp.exp(jnp.sum(lam_rows[0:1, :] * lam_rows[1:2, :], axis=-1, keepdims=True))
    e2 = jnp.exp(jnp.sum(lam_rows[2:3, :] * lam_rows[3:4, :], axis=-1, keepdims=True))
    lam = e1 - e2 + lam_init
    for hd in range(2):
        a1, a2 = acc_s[2 * hd], acc_s[2 * hd + 1]
        o = a1[:, :LANES] / a1[:, LANES:] - lam * (a2[:, :LANES] / a2[:, LANES:])
        ms = jnp.mean(o * o, axis=-1, keepdims=True)
        o_ref[0, :, hd * LANES:(hd + 1) * LANES] = (
            o * lax.rsqrt(ms + EPS) * subg_ref[...] * (1.0 - lam_init)).astype(o_ref.dtype)


def _mixer_b_call(proj3d, bias_b, far_b, lam_rows, subg, lam_init):
    b = proj3d.shape[0]
    width = 2 * LANES
    base = 3 * WIDTH_A // width
    q_blk0, k_blk0, v_blk0 = base, base + WIDTH_B // width, base + 2 * WIDTH_B // width
    t = Q_TILE_B
    grid_spec = pltpu.PrefetchScalarGridSpec(
        num_scalar_prefetch=1,
        grid=(b, N_HEADS_B // 2, SEQ // t),
        in_specs=[
            pl.BlockSpec((8, LANES), lambda bi, p, i, far: (0, 0)),
            pl.BlockSpec((1, t, width), lambda bi, p, i, far: (bi, i, q_blk0 + p)),
            pl.BlockSpec((1, SEQ, width), lambda bi, p, i, far: (bi, 0, k_blk0 + p)),
            pl.BlockSpec((1, SEQ, width), lambda bi, p, i, far: (bi, 0, v_blk0 + p)),
            pl.BlockSpec((2, 2, t, t), lambda bi, p, i, far: (0, p, 0, 0)),
            pl.BlockSpec((1, LANES), lambda bi, p, i, far: (0, 0)),
        ],
        out_specs=pl.BlockSpec((1, t, width), lambda bi, p, i, far: (bi, i, p)),
        scratch_shapes=[pltpu.VMEM((4, t, LANES), F32), pltpu.VMEM((4, t, 2 * LANES), F32)],
    )
    return pl.pallas_call(
        functools.partial(_mixer_b_kernel, lam_init=lam_init),
        grid_spec=grid_spec,
        out_shape=jax.ShapeDtypeStruct((b, SEQ, WIDTH_B), BF16),
        compiler_params=_cparams(("parallel", "parallel", "arbitrary")),
        name="diff_attention",
    )(far_b, lam_rows, proj3d, proj3d, proj3d, bias_b, subg)


def _out_proj_kernel(oa_ref, ob_ref, w_ref, x_ref, g_ref, *rest, with_router):
    if with_router:
        wr_ref, x1_ref, h_ref, route_ref = rest
    else:
        x1_ref, h_ref = rest
    acc = jnp.dot(oa_ref[...], w_ref[:WIDTH_A, :], preferred_element_type=F32)
    acc = acc + jnp.dot(ob_ref[...], w_ref[WIDTH_A:, :], preferred_element_type=F32)
    x1 = x_ref[...] + acc
    x1_ref[...] = x1
    ms = jnp.mean(x1 * x1, axis=-1, keepdims=True)
    hn = x1 * lax.rsqrt(ms + EPS) * g_ref[...]
    h_ref[...] = _to_slabs(hn) if with_router else hn.astype(h_ref.dtype)
    if with_router:
        hi = hn.astype(BF16)
        lo = (hn - hi.astype(F32)).astype(BF16)
        lg = (jnp.dot(hi, wr_ref[...], preferred_element_type=F32)
              + jnp.dot(lo, wr_ref[...], preferred_element_type=F32))
        lg = lg + pltpu.roll(lg, LANES - N_EXPERTS, 1)
        lane = lax.broadcasted_iota(jnp.int32, lg.shape, 1)
        lane_f = lane.astype(F32)
        lg = jnp.where(lane < N_EXPERTS, lg, -jnp.inf)
        v1 = jnp.max(lg, axis=-1, keepdims=True)
        i1 = jnp.min(jnp.where(lg == v1, lane_f, float(LANES)), axis=-1, keepdims=True)
        lg2 = jnp.where(lane_f == i1, -jnp.inf, lg)
        v2 = jnp.max(lg2, axis=-1, keepdims=True)
        i2 = jnp.min(jnp.where(lg2 == v2, lane_f, float(LANES)), axis=-1, keepdims=True)
        e = jnp.exp(v2 - v1)
        g1 = 1.0 / (1.0 + e)
        g2 = e / (1.0 + e)
        route_ref[...] = jnp.where(lane == 0, i1, jnp.where(lane == 1, i2,
                                   jnp.where(lane == 2, g1, jnp.where(lane == 3, g2, 0.0))))


def _out_proj_call(oa, ob, w_bf16, x2d, g, w_router_packed):
    n = x2d.shape[0]
    with_router = w_router_packed is not None
    row = lambda i: (i, 0)
    if with_router:
        h_spec = pl.BlockSpec((ROW_TILE, *ROW_SLAB), lambda i: (i, 0, 0))
        h_shape = jax.ShapeDtypeStruct((n, *ROW_SLAB), F32)
    else:
        h_spec = pl.BlockSpec((ROW_TILE, D_MODEL), row)
        h_shape = jax.ShapeDtypeStruct((n, D_MODEL), BF16)
    in_specs = [
        pl.BlockSpec((ROW_TILE, WIDTH_A), row),
        pl.BlockSpec((ROW_TILE, WIDTH_B), row),
        _resident((WIDTH_A + WIDTH_B, D_MODEL)),
        pl.BlockSpec((ROW_TILE, D_MODEL), row),
        _resident((1, D_MODEL)),
    ]
    out_specs = [pl.BlockSpec((ROW_TILE, D_MODEL), row), h_spec]
    out_shape = [jax.ShapeDtypeStruct((n, D_MODEL), F32), h_shape]
    args = [oa, ob, w_bf16, x2d, g]
    if with_router:
        in_specs.append(_resident((D_MODEL, LANES)))
        out_specs.append(pl.BlockSpec((ROW_TILE, LANES), row))
        out_shape.append(jax.ShapeDtypeStruct((n, LANES), F32))
        args.append(w_router_packed)
    return pl.pallas_call(
        functools.partial(_out_proj_kernel, with_router=with_router),
        grid=(n // ROW_TILE,),
        in_specs=in_specs,
        out_specs=out_specs,
        out_shape=out_shape,
        compiler_params=_cparams(("parallel",)),
        name="out_proj_router" if with_router else "out_proj",
    )(*args)


def _swiglu_act(g, u):
    return (g / (1.0 + jnp.exp(-g))) * u


def _ffn_kernel(h_ref, x_ref, wg_ref, wu_ref, wd_ref, o_ref, a_ref):
    h = h_ref[...]
    c0 = 0
    for width in FFN_CHUNKS:
        g = jnp.dot(h, wg_ref[:, c0:c0 + width], preferred_element_type=F32)
        u = jnp.dot(h, wu_ref[:, c0:c0 + width], preferred_element_type=F32)
        a_ref[:, c0:c0 + width] = _swiglu_act(g, u).astype(BF16)
        c0 += width
    o_ref[...] = x_ref[...] + jnp.dot(a_ref[...], wd_ref[...], preferred_element_type=F32)


def _ffn_call(h, x2d, wg, wu, wd):
    n = x2d.shape[0]
    row = lambda i: (i, 0)
    return pl.pallas_call(
        _ffn_kernel,
        grid=(n // ROW_TILE,),
        in_specs=[
            pl.BlockSpec((ROW_TILE, D_MODEL), row),
            pl.BlockSpec((ROW_TILE, D_MODEL), row),
            _resident((D_MODEL, D_FF)),
            _resident((D_MODEL, D_FF)),
            _resident((D_FF, D_MODEL)),
        ],
        out_specs=pl.BlockSpec((ROW_TILE, D_MODEL), row),
        out_shape=jax.ShapeDtypeStruct((n, D_MODEL), F32),
        scratch_shapes=[pltpu.VMEM((ROW_TILE, D_FF), BF16)],
        compiler_params=_cparams(("parallel",)),
        name="dense_swiglu",
    )(h, x2d, wg, wu, wd)


ROW_SLAB = (8, LANES)
assert ROW_SLAB[0] * ROW_SLAB[1] == D_MODEL


def _to_slabs(x):
    r = x.shape[0]
    xs = jnp.stack([x[:, LANES * j:LANES * (j + 1)] for j in range(ROW_SLAB[0])], axis=0)
    xs = xs.reshape(ROW_SLAB[0], r // 8, 8, LANES)
    xs = jnp.swapaxes(jnp.swapaxes(xs, 0, 1), 1, 2)
    return xs.reshape(r, *ROW_SLAB)


def _from_slabs(t):
    r = t.shape[0]
    y = jnp.swapaxes(t.reshape(r // 8, 8, *ROW_SLAB), 1, 2)
    return jnp.concatenate([y[:, j].reshape(r, LANES) for j in range(ROW_SLAB[0])], axis=1)


def _row_copy(src_ref, src_row, dst_ref, dst_row, sem):
    return pltpu.make_async_copy(src_ref.at[src_row], dst_ref.at[dst_row], sem)


def _dispatch_kernel(fill_ref, dest_ref, h_ref, o_hbm, zero_ref, sem):
    i = pl.program_id(0)

    def start(t, c):
        for k in range(TOP_K):
            _row_copy(h_ref, t, o_hbm, dest_ref[0, 0, TOP_K * t + k], sem).start(priority=k)
        return c

    def wait(t, c):
        for k in range(TOP_K):
            _row_copy(h_ref, t, o_hbm, 0, sem).wait()
        return c

    lax.fori_loop(0, DISPATCH_TILE, start, 0, unroll=4)
    lax.fori_loop(0, DISPATCH_TILE, wait, 0, unroll=4)

    @pl.when(i == pl.num_programs(0) - 1)
    def _():
        zero_ref[...] = jnp.zeros_like(zero_ref)
        for e in range(N_EXPERTS):
            lo, hi = fill_ref[e], fill_ref[N_EXPERTS + e]
            lax.fori_loop(lo, hi, lambda r, c: (_row_copy(zero_ref, 0, o_hbm, r, sem).start(), c)[1], 0)
            lax.fori_loop(lo, hi, lambda r, c: (_row_copy(zero_ref, 0, o_hbm, 0, sem).wait(), c)[1], 0)


def _dispatch_call(fill, dest3d, h, n_rows):
    n = h.shape[0]
    grid_spec = pltpu.PrefetchScalarGridSpec(
        num_scalar_prefetch=1,
        grid=(n // DISPATCH_TILE,),
        in_specs=[
            pl.BlockSpec((1, 1, TOP_K * DISPATCH_TILE), lambda i, f: (i, 0, 0), memory_space=pltpu.SMEM),
            pl.BlockSpec((DISPATCH_TILE, *ROW_SLAB), lambda i, f: (i, 0, 0)),
        ],
        out_specs=pl.BlockSpec(memory_space=pl.ANY),
        scratch_shapes=[pltpu.VMEM((1, *ROW_SLAB), h.dtype), pltpu.SemaphoreType.DMA(())],
    )
    return pl.pallas_call(
        _dispatch_kernel,
        grid_spec=grid_spec,
        out_shape=jax.ShapeDtypeStruct((n_rows, *ROW_SLAB), h.dtype),
        compiler_params=_cparams(("arbitrary",)),
        name="moe_dispatch",
    )(fill, dest3d, h)


def _moe_kernel(be_ref, nact_ref, x_ref, wg_ref, wu_ref, wd_ref, o_ref, xb_ref, acc_ref):
    b = pl.program_id(0)
    c = pl.program_id(1)
    last = pl.num_programs(1) - 1
    active = b < nact_ref[0]

    @pl.when(active)
    def _():
        @pl.when(c == 0)
        def _():
            xb_ref[...] = _from_slabs(x_ref[...]).astype(BF16)
            acc_ref[...] = jnp.zeros_like(acc_ref)

        xb = xb_ref[...]
        g = jnp.dot(xb, wg_ref[0], preferred_element_type=F32)
        u = jnp.dot(xb, wu_ref[0], preferred_element_type=F32)
        a = _swiglu_act(g, u).astype(BF16)
        for n0 in range(0, D_MODEL, MXU_DIM):
            cols = slice(n0, n0 + MXU_DIM)
            contrib = jnp.dot(a, wd_ref[0, :, cols], preferred_element_type=F32)
            acc_ref[:, cols] += contrib

        @pl.when(c == last)
        def _():
            o_ref[...] = _to_slabs(acc_ref[...])

    @pl.when(jnp.logical_not(active) & (c == last))
    def _():
        o_ref[...] = jnp.zeros_like(o_ref)


def _moe_call(block_expert, n_active, x_pad, wg, wu, wd):
    n_rows = x_pad.shape[0]
    n_blocks = n_rows // MOE_BLOCK
    n_chunks = D_FF_EXPERT // MOE_FF_TILE
    grid_spec = pltpu.PrefetchScalarGridSpec(
        num_scalar_prefetch=2,
        grid=(n_blocks, n_chunks),
        in_specs=[
            pl.BlockSpec((MOE_BLOCK, *ROW_SLAB), lambda b, c, be, na: (b, 0, 0)),
            pl.BlockSpec((1, D_MODEL, MOE_FF_TILE), lambda b, c, be, na: (be[b], 0, c)),
            pl.BlockSpec((1, D_MODEL, MOE_FF_TILE), lambda b, c, be, na: (be[b], 0, c)),
            pl.BlockSpec((1, MOE_FF_TILE, D_MODEL), lambda b, c, be, na: (be[b], c, 0)),
        ],
        out_specs=pl.BlockSpec((MOE_BLOCK, *ROW_SLAB), lambda b, c, be, na: (b, 0, 0)),
        scratch_shapes=[pltpu.VMEM((MOE_BLOCK, D_MODEL), BF16), pltpu.VMEM((MOE_BLOCK, D_MODEL), F32)],
    )
    return pl.pallas_call(
        _moe_kernel,
        grid_spec=grid_spec,
        out_shape=jax.ShapeDtypeStruct((n_rows, *ROW_SLAB), F32),
        compiler_params=_cparams(("arbitrary", "arbitrary")),
        name="moe_swiglu",
    )(block_expert, n_active, x_pad, wg, wu, wd)


def _combine_kernel(pos_ref, y_hbm, x_ref, route_ref, o_ref, buf, sem):
    def start(t, c):
        for k in range(TOP_K):
            _row_copy(y_hbm, pos_ref[0, 0, TOP_K * t + k], buf.at[k], t, sem).start(priority=k)
        return c

    def wait(t, c):
        for k in range(TOP_K):
            _row_copy(y_hbm, 0, buf.at[k], t, sem).wait()
        return c

    lax.fori_loop(0, COMBINE_TILE, start, 0, unroll=4)
    lax.fori_loop(0, COMBINE_TILE, wait, 0, unroll=4)
    route = route_ref[...]
    g0 = route[:, TOP_K:TOP_K + 1]
    g1 = route[:, TOP_K + 1:TOP_K + 2]
    o_ref[...] = x_ref[...] + g0 * _from_slabs(buf[0]) + g1 * _from_slabs(buf[1])


def _combine_call(pos3d, y_pad, x2d, route):
    n = x2d.shape[0]
    row = lambda i: (i, 0)
    return pl.pallas_call(
        _combine_kernel,
        grid=(n // COMBINE_TILE,),
        in_specs=[
            pl.BlockSpec((1, 1, TOP_K * COMBINE_TILE), lambda i: (i, 0, 0), memory_space=pltpu.SMEM),
            pl.BlockSpec(memory_space=pl.ANY),
            pl.BlockSpec((COMBINE_TILE, D_MODEL), row),
            pl.BlockSpec((COMBINE_TILE, LANES), row),
        ],
        out_specs=pl.BlockSpec((COMBINE_TILE, D_MODEL), row),
        out_shape=jax.ShapeDtypeStruct((n, D_MODEL), F32),
        scratch_shapes=[pltpu.VMEM((TOP_K, COMBINE_TILE, *ROW_SLAB), F32), pltpu.SemaphoreType.DMA(())],
        compiler_params=_cparams(("arbitrary",)),
        name="moe_combine",
    )(pos3d, y_pad, x2d, route)


def _moe_layer(x1, hn, route, wg, wu, wd):
    n = x1.shape[0]
    e_flat = route[:, :TOP_K].astype(jnp.int32).reshape(-1)
    onehot = (e_flat[:, None] == jnp.arange(N_EXPERTS, dtype=jnp.int32)[None, :]).astype(jnp.int32)
    csum = jnp.cumsum(onehot, axis=0)
    counts = csum[-1]
    padded = (counts + MOE_BLOCK - 1) // MOE_BLOCK * MOE_BLOCK
    pad_ends = jnp.cumsum(padded)
    pad_starts = pad_ends - padded
    dest = jnp.sum((csum - onehot + pad_starts[None, :]) * onehot, axis=1)
    n_blocks = -(-(n * TOP_K) // MOE_BLOCK) + N_EXPERTS
    n_rows = n_blocks * MOE_BLOCK
    block_start = jnp.arange(n_blocks, dtype=jnp.int32) * MOE_BLOCK
    block_expert = jnp.minimum(
        jnp.sum((pad_ends[None, :] <= block_start[:, None]).astype(jnp.int32), axis=1), N_EXPERTS - 1)
    n_active = (pad_ends[-1:] // MOE_BLOCK).astype(jnp.int32)
    fill = jnp.concatenate([pad_starts + counts, pad_starts[1:], jnp.full((1,), n_rows, jnp.int32)])

    x_pad = _dispatch_call(fill.astype(jnp.int32), dest.reshape(n // DISPATCH_TILE, 1, TOP_K * DISPATCH_TILE),
                           hn, n_rows)
    y_pad = _moe_call(block_expert.astype(jnp.int32), n_active, x_pad, wg, wu, wd)
    pos3d = dest.reshape(n // COMBINE_TILE, 1, TOP_K * COMBINE_TILE)
    return _combine_call(pos3d, y_pad, x1, route)


def _head_group_matrix():
    idx = np.arange(MXU_DIM) // HEAD_DIM
    return jnp.asarray((idx[:, None] == idx[None, :]).astype(np.float32), dtype=BF16)


def _pack_router(w_router):
    hi = w_router.astype(BF16)
    lo = (w_router - hi.astype(F32)).astype(BF16)
    pad = jnp.zeros((D_MODEL, LANES - 2 * N_EXPERTS), BF16)
    return jnp.concatenate([hi, lo, pad], axis=1)


def kernel(x, norm_mix_g, norm_ffn_g, w_in, w_out, q_norm_a, k_norm_a, q_norm_b, k_norm_b,
           lambda_q1, lambda_k1, lambda_q2, lambda_k2, subln_g, rel_bias,
           w_gate_dense, w_up_dense, w_down_dense, w_router, w_gate_moe, w_up_moe, w_down_moe):
    b, s, d = x.shape
    assert (s, d) == (SEQ, D_MODEL)
    depth = w_in.shape[0]
    n = b * s
    x2d = x.reshape(n, d)
    gmat = _head_group_matrix()
    bias_a = _bias_tables(rel_bias, _dilated_buckets(), 0, N_HEADS_A)
    bias_b = _bias_tables(rel_bias, _diff_buckets(), N_HEADS_A, N_HEADS_B)
    far_b = rel_bias[_far_bucket(), N_HEADS_A:]
    ones = jnp.ones((PROJ_COL_TILE,), F32)
    rep = PROJ_COL_TILE // HEAD_DIM

    for layer in range(depth):
        gains = jnp.concatenate([
            jnp.tile(q_norm_a[layer], rep) * (ATTN_SCALE * LOG2E), jnp.tile(k_norm_a[layer], rep), ones,
            jnp.tile(q_norm_b[layer], rep) * (ATTN_SCALE * LOG2E), jnp.tile(k_norm_b[layer], rep), ones,
        ]).reshape(1, PROJ_WIDTH)
        proj = _proj_call(x2d, norm_mix_g[layer].reshape(1, d), w_in[layer].astype(BF16), gains, gmat)
        proj3d = proj.reshape(b, s, PROJ_WIDTH)

        oa = _mixer_a_call(proj3d, bias_a)
        lam_init = 0.8 - 0.6 * math.exp(-0.3 * layer)
        lam_rows = jnp.zeros((8, LANES), F32).at[:4, :HEAD_DIM].set(
            jnp.stack([lambda_q1[layer], lambda_k1[layer], lambda_q2[layer], lambda_k2[layer]]))
        ob = _mixer_b_call(proj3d, bias_b, far_b, lam_rows, subln_g[layer].reshape(1, LANES), lam_init)

        is_moe = layer % 2 == 1
        i = layer // 2
        outs = _out_proj_call(
            oa.reshape(n, WIDTH_A), ob.reshape(n, WIDTH_B), w_out[layer].astype(BF16), x2d,
            norm_ffn_g[layer].reshape(1, d), _pack_router(w_router[i]) if is_moe else None)
        if is_moe:
            x1, hn, route = outs
            x2d = _moe_layer(x1, hn, route, w_gate_moe[i].astype(BF16), w_up_moe[i].astype(BF16),
                             w_down_moe[i].astype(BF16))
        else:
            x1, hn = outs
            x2d = _ffn_call(hn, x1, w_gate_dense[i].astype(BF16), w_up_dense[i].astype(BF16),
                            w_down_dense[i].astype(BF16))
    return x2d.reshape(b, s, d)
```

```python
import functools
import math

import numpy as np
import jax
import jax.numpy as jnp
from jax import lax
from jax.experimental import pallas as pl
from jax.experimental.pallas import tpu as pltpu

F32 = jnp.float32
BF16 = jnp.bfloat16

D_MODEL = 1024
SEQ = 2048
HEAD_DIM = 64
ATTN_SCALE = HEAD_DIM ** -0.5
LOG2E = math.log2(math.e)
N_HEADS_A = 8
DIL_PATTERNS = ((128, 1), (512, 4), (2048, 16))
DIL_BLOCK = 128
N_HEADS_B = 4
WIDTH_A = N_HEADS_A * HEAD_DIM
WIDTH_B = N_HEADS_B * 2 * HEAD_DIM
PROJ_WIDTH = 3 * (WIDTH_A + WIDTH_B)
N_BUCKETS = 32
MAX_DISTANCE = 128
D_FF = 2816
N_EXPERTS = 8
TOP_K = 2
D_FF_EXPERT = 3584
EPS = 1e-6
NEG = -1e30

LANES = 128
MXU_DIM = 256
VMEM_LIMIT = 56 * 1024 * 1024

ROW_TILE = 512
PROJ_COL_TILE = 512
Q_TILE_B = 512
DIL_GROUP = 16
MOE_BLOCK = 512
MOE_FF_TILE = 1792
FFN_CHUNKS = (512, 512, 512, 512, 512, 256)
DISPATCH_TILE = 256
COMBINE_TILE = 256


def _cparams(sem):
    return pltpu.CompilerParams(dimension_semantics=sem, vmem_limit_bytes=VMEM_LIMIT)


def _resident(shape):
    nd = len(shape)
    return pl.BlockSpec(shape, lambda *_: (0,) * nd, pipeline_mode=pl.Buffered(1))


def _t5_bucket_np(dist):
    n = np.maximum(dist, 0)
    max_exact = N_BUCKETS // 2
    large = max_exact + (np.log(np.maximum(n, 1).astype(np.float32) / max_exact)
                         / math.log(MAX_DISTANCE / max_exact) * (N_BUCKETS - max_exact)).astype(np.int32)
    large = np.minimum(large, N_BUCKETS - 1)
    return np.where(n < max_exact, n, large).astype(np.int32)


def _dilated_buckets():
    i = np.arange(DIL_BLOCK)[:, None]
    j = np.arange(2 * DIL_BLOCK)[None, :]
    steps = DIL_BLOCK + i - j
    valid = (steps >= 0) & (steps <= DIL_BLOCK)
    tabs = []
    for _, dil in DIL_PATTERNS:
        bucket = np.where(valid, _t5_bucket_np(steps * dil), -1)
        tabs.append(bucket)
        tabs.append(np.where(j >= DIL_BLOCK, bucket, -1))
    return np.stack(tabs).astype(np.int32)


def _diff_buckets():
    t = Q_TILE_B
    qi = np.arange(t)[:, None]
    kj = np.arange(t)[None, :]
    tabs = []
    for off in (0, t):
        dist = qi - kj + off
        tabs.append(np.where(dist >= 0, _t5_bucket_np(dist), -1))
    return np.stack(tabs).astype(np.int32)


def _far_bucket():
    b = _t5_bucket_np(np.arange(Q_TILE_B + 1, SEQ + 1))
    assert (b == b[0]).all()
    return int(b[0])


def _bias_table_kernel(rel_ref, bucket_ref, o_ref, *, head0):
    col = head0 + pl.program_id(1)
    bucket = bucket_ref[0]
    val = jnp.full(bucket.shape, NEG, F32)
    for b in range(N_BUCKETS):
        val = jnp.where(bucket == b, rel_ref[b, col] * LOG2E, val)
    o_ref[0, 0] = val


def _bias_tables(rel_bias, buckets, head0, n_heads):
    n_tab, rows, cols = buckets.shape
    return pl.pallas_call(
        functools.partial(_bias_table_kernel, head0=head0),
        grid=(n_tab, n_heads),
        in_specs=[
            pl.BlockSpec(memory_space=pltpu.SMEM),
            pl.BlockSpec((1, rows, cols), lambda t, h: (t, 0, 0)),
        ],
        out_specs=pl.BlockSpec((1, 1, rows, cols), lambda t, h: (t, h, 0, 0)),
        out_shape=jax.ShapeDtypeStruct((n_tab, n_heads, rows, cols), F32),
        compiler_params=_cparams(("parallel", "parallel")),
        name="bias_tables",
    )(rel_bias, jnp.asarray(buckets))


def _proj_kernel(x_ref, g_ref, w_ref, gain_ref, gmat_ref, o_ref):
    x = x_ref[...]
    ms = jnp.mean(x * x, axis=-1, keepdims=True)
    xn = (x * lax.rsqrt(ms + EPS) * g_ref[...]).astype(BF16)
    for j in range(PROJ_WIDTH // PROJ_COL_TILE):
        cols = slice(j * PROJ_COL_TILE, (j + 1) * PROJ_COL_TILE)
        acc = jnp.dot(xn, w_ref[:, cols], preferred_element_type=F32)
        if j % 3 == 2:
            o_ref[:, cols] = acc.astype(BF16)
        else:
            sq = (acc * acc).astype(BF16)
            ss = jnp.concatenate(
                [jnp.dot(sq[:, c:c + MXU_DIM], gmat_ref[...], preferred_element_type=F32)
                 for c in range(0, PROJ_COL_TILE, MXU_DIM)], axis=1)
            o_ref[:, cols] = (acc * lax.rsqrt(ss * (1.0 / HEAD_DIM) + EPS) * gain_ref[:, cols]).astype(BF16)


def _proj_call(x2d, g, w_bf16, gains, gmat):
    n = x2d.shape[0]
    return pl.pallas_call(
        _proj_kernel,
        grid=(n // ROW_TILE,),
        in_specs=[
            pl.BlockSpec((ROW_TILE, D_MODEL), lambda i: (i, 0)),
            _resident((1, D_MODEL)),
            _resident((D_MODEL, PROJ_WIDTH)),
            _resident((1, PROJ_WIDTH)),
            _resident((MXU_DIM, MXU_DIM)),
        ],
        out_specs=pl.BlockSpec((ROW_TILE, PROJ_WIDTH), lambda i: (i, 0)),
        out_shape=jax.ShapeDtypeStruct((n, PROJ_WIDTH), BF16),
        compiler_params=_cparams(("parallel",)),
        name="norm_in_proj",
    )(x2d, g, w_bf16, gains, gmat)


def _deinterleave(x, dil):
    if dil == 1:
        return x
    return jnp.swapaxes(x.reshape(SEQ // dil, dil, LANES), 0, 1).reshape(SEQ, LANES)


def _interleave(x, dil):
    if dil == 1:
        return x
    return jnp.swapaxes(x.reshape(dil, SEQ // dil, LANES), 0, 1).reshape(SEQ, LANES)


def _mixer_a_kernel(q_ref, k_ref, v_ref, bias_ref, o_ref, qp, kp, vp, out_s, lse_s):
    n_tiles = SEQ // DIL_BLOCK
    n_branch = len(DIL_PATTERNS)
    lane_full = lax.broadcasted_iota(jnp.int32, (SEQ, LANES), 1)
    lane = lax.broadcasted_iota(jnp.int32, (DIL_BLOCK, LANES), 1)
    head0 = lane < HEAD_DIM

    q = q_ref[0].astype(F32)
    k = k_ref[0].astype(F32)
    v = v_ref[0].astype(F32)
    pad = jnp.zeros((DIL_BLOCK, LANES), BF16)
    for bi, (_, dil) in enumerate(DIL_PATTERNS):
        qd = _deinterleave(q, dil)
        qp[bi, 0] = jnp.where(lane_full < HEAD_DIM, qd, 0.0).astype(BF16)
        qp[bi, 1] = jnp.where(lane_full >= HEAD_DIM, qd, 0.0).astype(BF16)
        kp[bi, :DIL_BLOCK, :] = pad
        vp[bi, :DIL_BLOCK, :] = pad
        kp[bi, DIL_BLOCK:, :] = _deinterleave(k, dil).astype(BF16)
        vp[bi, DIL_BLOCK:, :] = _deinterleave(v, dil).astype(BF16)

    for bi, (_, dil) in enumerate(DIL_PATTERNS):
        nb = SEQ // dil // DIL_BLOCK

        def tile(t, bi=bi, nb=nb):
            row0 = t * DIL_BLOCK if isinstance(t, int) else pl.multiple_of(t * DIL_BLOCK, DIL_BLOCK)
            rows = pl.ds(row0, DIL_BLOCK)
            keys = pl.ds(row0, 2 * DIL_BLOCK)
            tab = 2 * bi + (t % nb == 0)
            kt = kp[bi, keys, :]
            vt = jnp.concatenate([vp[bi, keys, :], jnp.ones((2 * DIL_BLOCK, LANES), BF16)], axis=1)
            m_h, pv_h = [], []
            for hh in range(2):
                s = lax.dot_general(qp[bi, hh, rows, :], kt, (((1,), (1,)), ((), ())),
                                    preferred_element_type=F32) + bias_ref[tab, hh]
                m = jnp.broadcast_to(jnp.max(s, axis=-1, keepdims=True), (DIL_BLOCK, LANES))
                p = jnp.exp2(s - jnp.concatenate([m, m], axis=1))
                m_h.append(m)
                pv_h.append(jnp.dot(p.astype(BF16), vt, preferred_element_type=F32))
            l = jnp.where(head0, pv_h[0][:, LANES:], pv_h[1][:, LANES:])
            out_s[bi, rows, :] = jnp.where(head0, pv_h[0][:, :LANES], pv_h[1][:, :LANES]) / l
            lse_s[bi, rows, :] = jnp.where(head0, m_h[0], m_h[1]) + jnp.log2(l)

        if DIL_GROUP == n_tiles:
            for t in range(n_tiles):
                tile(t)
        else:
            def group(g, carry, tile=tile):
                for u in range(DIL_GROUP):
                    tile(g * DIL_GROUP + u)
                return carry

            lax.fori_loop(0, n_tiles // DIL_GROUP, group, 0)

    lse_b = [_interleave(lse_s[bi], dil) for bi, (_, dil) in enumerate(DIL_PATTERNS)]
    lse_max = functools.reduce(jnp.maximum, lse_b)
    w_b = [jnp.exp2(lse - lse_max) for lse in lse_b]
    num = sum(w * _interleave(out_s[bi], dil) for w, (bi, (_, dil)) in zip(w_b, enumerate(DIL_PATTERNS)))
    o_ref[0] = (num / sum(w_b)).astype(o_ref.dtype)


def _mixer_a_call(proj3d, bias_a):
    b = proj3d.shape[0]
    n_pairs = N_HEADS_A // 2
    n_branch = len(DIL_PATTERNS)
    q_blk0, k_blk0, v_blk0 = 0, WIDTH_A // LANES, 2 * WIDTH_A // LANES
    return pl.pallas_call(
        _mixer_a_kernel,
        grid=(b, n_pairs),
        in_specs=[
            pl.BlockSpec((1, SEQ, LANES), lambda i, p: (i, 0, q_blk0 + p)),
            pl.BlockSpec((1, SEQ, LANES), lambda i, p: (i, 0, k_blk0 + p)),
            pl.BlockSpec((1, SEQ, LANES), lambda i, p: (i, 0, v_blk0 + p)),
            pl.BlockSpec((bias_a.shape[0], 2, DIL_BLOCK, 2 * DIL_BLOCK), lambda i, p: (0, p, 0, 0)),
        ],
        out_specs=pl.BlockSpec((1, SEQ, LANES), lambda i, p: (i, 0, p)),
        out_shape=jax.ShapeDtypeStruct((b, SEQ, WIDTH_A), BF16),
        scratch_shapes=[
            pltpu.VMEM((n_branch, 2, SEQ, LANES), BF16),
            pltpu.VMEM((n_branch, SEQ + DIL_BLOCK, LANES), BF16),
            pltpu.VMEM((n_branch, SEQ + DIL_BLOCK, LANES), BF16),
            pltpu.VMEM((n_branch, SEQ, LANES), F32),
            pltpu.VMEM((n_branch, SEQ, LANES), F32),
        ],
        compiler_params=_cparams(("parallel", "parallel")),
        name="dilated_attention",
    )(proj3d, proj3d, proj3d, bias_a)


def _mixer_b_kernel(far_ref, lam_ref, q_ref, k_ref, v_ref, bias_ref, subg_ref, o_ref, *, lam_init):
    pair = pl.program_id(1)
    t = Q_TILE_B
    reps = t // LANES

    def run(n_q):
        lane = lax.broadcasted_iota(jnp.int32, (t, LANES), 1)
        q_maps = []
        for hd in range(2):
            q = q_ref[0, :, hd * LANES:(hd + 1) * LANES]
            zero = jnp.zeros_like(q)
            q_maps.append((jnp.where(lane < HEAD_DIM, q, zero), jnp.where(lane >= HEAD_DIM, q, zero)))
        ones = jnp.ones((t, LANES), BF16)
        state = [None] * 4

        def key_tile(j, kind):
            keys = slice(j * t, (j + 1) * t)
            for hd in range(2):
                kt = k_ref[0, keys, hd * LANES:(hd + 1) * LANES]
                vt = jnp.concatenate([v_ref[0, keys, hd * LANES:(hd + 1) * LANES], ones], axis=1)
                for half in range(2):
                    c = 2 * hd + half
                    s = lax.dot_general(q_maps[hd][half], kt, (((1,), (1,)), ((), ())),
                                        preferred_element_type=F32)
                    if kind == "far":
                        bias = far_ref[2 * pair + hd] * LOG2E
                        m_cur = jnp.broadcast_to(jnp.max(s, axis=-1, keepdims=True) + bias, (t, LANES))
                    else:
                        s = s + bias_ref[1 if kind == "sub" else 0, hd]
                        m_cur = jnp.broadcast_to(jnp.max(s, axis=-1, keepdims=True), (t, LANES))
                    m_new = m_cur if state[c] is None else jnp.maximum(state[c][0], m_cur)
                    shift = m_new - bias if kind == "far" else m_new
                    p = jnp.exp2(s - jnp.concatenate([shift] * reps, axis=1))
                    pv = jnp.dot(p.astype(BF16), vt, preferred_element_type=F32)
                    if state[c] is not None:
                        alpha = jnp.exp2(state[c][0] - m_new)
                        pv = state[c][1] * jnp.concatenate([alpha, alpha], axis=1) + pv
                    state[c] = (m_new, pv)

        for j in range(n_q - 1):
            key_tile(j, "far")
        if n_q > 0:
            key_tile(n_q - 1, "sub")
        key_tile(n_q, "diag")

        lam_rows = lam_ref[...]
        e1 = jnp.exp(jnp.sum(lam_rows[0:1, :] * lam_rows[1:2, :], axis=-1, keepdims=True))
        e2 = jnp.exp(jnp.sum(lam_rows[2:3, :] * lam_rows[3:4, :], axis=-1, keepdims=True))
        lam = e1 - e2 + lam_init
        for hd in range(2):
            a1, a2 = state[2 * hd][1], state[2 * hd + 1][1]
            o = a1[:, :LANES] / a1[:, LANES:] - lam * (a2[:, :LANES] / a2[:, LANES:])
            ms = jnp.mean(o * o, axis=-1, keepdims=True)
            o_ref[0, :, hd * LANES:(hd + 1) * LANES] = (
                o * lax.rsqrt(ms + EPS) * subg_ref[...] * (1.0 - lam_init)).astype(o_ref.dtype)

    for n_q in range(SEQ // t):
        pl.when(pl.program_id(2) == n_q)(functools.partial(run, n_q))


def _mixer_b_call(proj3d, bias_b, far_b, lam_rows, subg, lam_init):
    b = proj3d.shape[0]
    width = 2 * LANES
    base = 3 * WIDTH_A // width
    q_blk0, k_blk0, v_blk0 = base, base + WIDTH_B // width, base + 2 * WIDTH_B // width
    t = Q_TILE_B
    grid_spec = pltpu.PrefetchScalarGridSpec(
        num_scalar_prefetch=1,
        grid=(b, N_HEADS_B // 2, SEQ // t),
        in_specs=[
            pl.BlockSpec((8, LANES), lambda bi, p, i, far: (0, 0)),
            pl.BlockSpec((1, t, width), lambda bi, p, i, far: (bi, i, q_blk0 + p)),
            pl.BlockSpec((1, SEQ, width), lambda bi, p, i, far: (bi, 0, k_blk0 + p)),
            pl.BlockSpec((1, SEQ, width), lambda bi, p, i, far: (bi, 0, v_blk0 + p)),
            pl.BlockSpec((2, 2, t, t), lambda bi, p, i, far: (0, p, 0, 0)),
            pl.BlockSpec((1, LANES), lambda bi, p, i, far: (0, 0)),
        ],
        out_specs=pl.BlockSpec((1, t, width), lambda bi, p, i, far: (bi, i, p)),
    )
    return pl.pallas_call(
        functools.partial(_mixer_b_kernel, lam_init=lam_init),
        grid_spec=grid_spec,
        out_shape=jax.ShapeDtypeStruct((b, SEQ, WIDTH_B), BF16),
        compiler_params=_cparams(("parallel", "parallel", "arbitrary")),
        name="diff_attention",
    )(far_b, lam_rows, proj3d, proj3d, proj3d, bias_b, subg)


def _mixer_out_residual_norm(oa_ref, ob_ref, w_ref, x_ref, g_ref):
    acc = jnp.dot(oa_ref[...], w_ref[:WIDTH_A, :], preferred_element_type=F32)
    acc = acc + jnp.dot(ob_ref[...], w_ref[WIDTH_A:, :], preferred_element_type=F32)
    x1 = x_ref[...] + acc
    ms = jnp.mean(x1 * x1, axis=-1, keepdims=True)
    return x1, x1 * lax.rsqrt(ms + EPS) * g_ref[...]


def _out_proj_router_kernel(oa_ref, ob_ref, w_ref, x_ref, g_ref, wr_ref, x1_ref, h_ref, route_ref):
    x1, hn = _mixer_out_residual_norm(oa_ref, ob_ref, w_ref, x_ref, g_ref)
    x1_ref[...] = x1
    h_ref[...] = _to_slabs(hn)
    hi = hn.astype(BF16)
    lo = (hn - hi.astype(F32)).astype(BF16)
    lg = (jnp.dot(hi, wr_ref[...], preferred_element_type=F32)
          + jnp.dot(lo, wr_ref[...], preferred_element_type=F32))
    lg = lg + pltpu.roll(lg, LANES - N_EXPERTS, 1)
    lane = lax.broadcasted_iota(jnp.int32, lg.shape, 1)
    lane_f = lane.astype(F32)
    lg = jnp.where(lane < N_EXPERTS, lg, -jnp.inf)
    v1 = jnp.max(lg, axis=-1, keepdims=True)
    i1 = jnp.min(jnp.where(lg == v1, lane_f, float(LANES)), axis=-1, keepdims=True)
    lg2 = jnp.where(lane_f == i1, -jnp.inf, lg)
    v2 = jnp.max(lg2, axis=-1, keepdims=True)
    i2 = jnp.min(jnp.where(lg2 == v2, lane_f, float(LANES)), axis=-1, keepdims=True)
    e = jnp.exp(v2 - v1)
    g1 = 1.0 / (1.0 + e)
    g2 = e / (1.0 + e)
    route_ref[...] = jnp.where(lane == 0, i1, jnp.where(lane == 1, i2,
                               jnp.where(lane == 2, g1, jnp.where(lane == 3, g2, 0.0))))


def _mixer_out_specs(row):
    return [
        pl.BlockSpec((ROW_TILE, WIDTH_A), row),
        pl.BlockSpec((ROW_TILE, WIDTH_B), row),
        _resident((WIDTH_A + WIDTH_B, D_MODEL)),
        pl.BlockSpec((ROW_TILE, D_MODEL), row),
        _resident((1, D_MODEL)),
    ]


def _out_proj_router_call(oa, ob, w_bf16, x2d, g, w_router_packed):
    n = x2d.shape[0]
    row = lambda i: (i, 0)
    return pl.pallas_call(
        _out_proj_router_kernel,
        grid=(n // ROW_TILE,),
        in_specs=_mixer_out_specs(row) + [_resident((D_MODEL, LANES))],
        out_specs=[
            pl.BlockSpec((ROW_TILE, D_MODEL), row),
            pl.BlockSpec((ROW_TILE, *ROW_SLAB), lambda i: (i, 0, 0)),
            pl.BlockSpec((ROW_TILE, LANES), row),
        ],
        out_shape=[
            jax.ShapeDtypeStruct((n, D_MODEL), F32),
            jax.ShapeDtypeStruct((n, *ROW_SLAB), F32),
            jax.ShapeDtypeStruct((n, LANES), F32),
        ],
        compiler_params=_cparams(("parallel",)),
        name="out_proj_router",
    )(oa, ob, w_bf16, x2d, g, w_router_packed)


def _swiglu_act(g, u):
    return (g / (1.0 + jnp.exp(-g))) * u


def _out_proj_ffn_kernel(oa_ref, ob_ref, w_ref, x_ref, g_ref, wg_ref, wu_ref, wd_ref, o_ref, a_ref):
    x1, hn = _mixer_out_residual_norm(oa_ref, ob_ref, w_ref, x_ref, g_ref)
    h = hn.astype(BF16)
    c0 = 0
    for width in FFN_CHUNKS:
        g = jnp.dot(h, wg_ref[:, c0:c0 + width], preferred_element_type=F32)
        u = jnp.dot(h, wu_ref[:, c0:c0 + width], preferred_element_type=F32)
        a_ref[:, c0:c0 + width] = _swiglu_act(g, u).astype(BF16)
        c0 += width
    o_ref[...] = x1 + jnp.dot(a_ref[...], wd_ref[...], preferred_element_type=F32)


def _out_proj_ffn_call(oa, ob, w_bf16, x2d, g, wg, wu, wd):
    n = x2d.shape[0]
    row = lambda i: (i, 0)
    return pl.pallas_call(
        _out_proj_ffn_kernel,
        grid=(n // ROW_TILE,),
        in_specs=_mixer_out_specs(row) + [
            _resident((D_MODEL, D_FF)),
            _resident((D_MODEL, D_FF)),
            _resident((D_FF, D_MODEL)),
        ],
        out_specs=pl.BlockSpec((ROW_TILE, D_MODEL), row),
        out_shape=jax.ShapeDtypeStruct((n, D_MODEL), F32),
        scratch_shapes=[pltpu.VMEM((ROW_TILE, D_FF), BF16)],
        compiler_params=_cparams(("parallel",)),
        name="out_proj_dense_swiglu",
    )(oa, ob, w_bf16, x2d, g, wg, wu, wd)


ROW_SLAB = (8, LANES)
assert ROW_SLAB[0] * ROW_SLAB[1] == D_MODEL


def _to_slabs(x):
    r = x.shape[0]
    xs = jnp.stack([x[:, LANES * j:LANES * (j + 1)] for j in range(ROW_SLAB[0])], axis=0)
    xs = xs.reshape(ROW_SLAB[0], r // 8, 8, LANES)
    xs = jnp.swapaxes(jnp.swapaxes(xs, 0, 1), 1, 2)
    return xs.reshape(r, *ROW_SLAB)


def _from_slabs(t):
    r = t.shape[0]
    y = jnp.swapaxes(t.reshape(r // 8, 8, *ROW_SLAB), 1, 2)
    return jnp.concatenate([y[:, j].reshape(r, LANES) for j in range(ROW_SLAB[0])], axis=1)


def _row_copy(src_ref, src_row, dst_ref, dst_row, sem):
    return pltpu.make_async_copy(src_ref.at[src_row], dst_ref.at[dst_row], sem)


def _dispatch_kernel(fill_ref, dest_ref, h_ref, o_hbm, zero_ref, sem):
    i = pl.program_id(0)

    def start(t, c):
        for k in range(TOP_K):
            _row_copy(h_ref, t, o_hbm, dest_ref[0, 0, TOP_K * t + k], sem).start(priority=k)
        return c

    def wait(t, c):
        for k in range(TOP_K):
            _row_copy(h_ref, t, o_hbm, 0, sem).wait()
        return c

    lax.fori_loop(0, DISPATCH_TILE, start, 0, unroll=4)
    lax.fori_loop(0, DISPATCH_TILE, wait, 0, unroll=4)

    @pl.when(i == pl.num_programs(0) - 1)
    def _():
        zero_ref[...] = jnp.zeros_like(zero_ref)
        for e in range(N_EXPERTS):
            lo, hi = fill_ref[e], fill_ref[N_EXPERTS + e]
            lax.fori_loop(lo, hi, lambda r, c: (_row_copy(zero_ref, 0, o_hbm, r, sem).start(), c)[1], 0)
            lax.fori_loop(lo, hi, lambda r, c: (_row_copy(zero_ref, 0, o_hbm, 0, sem).wait(), c)[1], 0)


def _dispatch_call(fill, dest3d, h, n_rows):
    n = h.shape[0]
    grid_spec = pltpu.PrefetchScalarGridSpec(
        num_scalar_prefetch=1,
        grid=(n // DISPATCH_TILE,),
        in_specs=[
            pl.BlockSpec((1, 1, TOP_K * DISPATCH_TILE), lambda i, f: (i, 0, 0), memory_space=pltpu.SMEM),
            pl.BlockSpec((DISPATCH_TILE, *ROW_SLAB), lambda i, f: (i, 0, 0)),
        ],
        out_specs=pl.BlockSpec(memory_space=pl.ANY),
        scratch_shapes=[pltpu.VMEM((1, *ROW_SLAB), h.dtype), pltpu.SemaphoreType.DMA(())],
    )
    return pl.pallas_call(
        _dispatch_kernel,
        grid_spec=grid_spec,
        out_shape=jax.ShapeDtypeStruct((n_rows, *ROW_SLAB), h.dtype),
        compiler_params=_cparams(("arbitrary",)),
        name="moe_dispatch",
    )(fill, dest3d, h)


def _moe_kernel(be_ref, nact_ref, x_ref, wg_ref, wu_ref, wd_ref, o_ref, xb_ref, acc_ref):
    b = pl.program_id(0)
    c = pl.program_id(1)
    last = pl.num_programs(1) - 1
    active = b < nact_ref[0]

    @pl.when(active)
    def _():
        @pl.when(c == 0)
        def _():
            xb_ref[...] = _from_slabs(x_ref[...]).astype(BF16)
            acc_ref[...] = jnp.zeros_like(acc_ref)

        xb = xb_ref[...]
        g = jnp.dot(xb, wg_ref[0], preferred_element_type=F32)
        u = jnp.dot(xb, wu_ref[0], preferred_element_type=F32)
        a = _swiglu_act(g, u).astype(BF16)
        for n0 in range(0, D_MODEL, MXU_DIM):
            cols = slice(n0, n0 + MXU_DIM)
            contrib = jnp.dot(a, wd_ref[0, :, cols], preferred_element_type=F32)
            acc_ref[:, cols] += contrib

        @pl.when(c == last)
        def _():
            o_ref[...] = _to_slabs(acc_ref[...])

    @pl.when(jnp.logical_not(active) & (c == last))
    def _():
        o_ref[...] = jnp.zeros_like(o_ref)


def _moe_call(block_expert, n_active, x_pad, wg, wu, wd):
    n_rows = x_pad.shape[0]
    n_blocks = n_rows // MOE_BLOCK
    n_chunks = D_FF_EXPERT // MOE_FF_TILE
    grid_spec = pltpu.PrefetchScalarGridSpec(
        num_scalar_prefetch=2,
        grid=(n_blocks, n_chunks),
        in_specs=[
            pl.BlockSpec((MOE_BLOCK, *ROW_SLAB), lambda b, c, be, na: (b, 0, 0)),
            pl.BlockSpec((1, D_MODEL, MOE_FF_TILE), lambda b, c, be, na: (be[b], 0, c)),
            pl.BlockSpec((1, D_MODEL, MOE_FF_TILE), lambda b, c, be, na: (be[b], 0, c)),
            pl.BlockSpec((1, MOE_FF_TILE, D_MODEL), lambda b, c, be, na: (be[b], c, 0)),
        ],
        out_specs=pl.BlockSpec((MOE_BLOCK, *ROW_SLAB), lambda b, c, be, na: (b, 0, 0)),
        scratch_shapes=[pltpu.VMEM((MOE_BLOCK, D_MODEL), BF16), pltpu.VMEM((MOE_BLOCK, D_MODEL), F32)],
    )
    return pl.pallas_call(
        _moe_kernel,
        grid_spec=grid_spec,
        out_shape=jax.ShapeDtypeStruct((n_rows, *ROW_SLAB), F32),
        compiler_params=_cparams(("arbitrary", "arbitrary")),
        name="moe_swiglu",
    )(block_expert, n_active, x_pad, wg, wu, wd)


def _combine_kernel(pos_ref, y_hbm, x_ref, route_ref, o_ref, buf, sem):
    def start(t, c):
        for k in range(TOP_K):
            _row_copy(y_hbm, pos_ref[0, 0, TOP_K * t + k], buf.at[k], t, sem).start(priority=k)
        return c

    def wait(t, c):
        for k in range(TOP_K):
            _row_copy(y_hbm, 0, buf.at[k], t, sem).wait()
        return c

    lax.fori_loop(0, COMBINE_TILE, start, 0, unroll=4)
    lax.fori_loop(0, COMBINE_TILE, wait, 0, unroll=4)
    route = route_ref[...]
    g0 = route[:, TOP_K:TOP_K + 1]
    g1 = route[:, TOP_K + 1:TOP_K + 2]
    o_ref[...] = x_ref[...] + g0 * _from_slabs(buf[0]) + g1 * _from_slabs(buf[1])


def _combine_call(pos3d, y_pad, x2d, route):
    n = x2d.shape[0]
    row = lambda i: (i, 0)
    return pl.pallas_call(
        _combine_kernel,
        grid=(n // COMBINE_TILE,),
        in_specs=[
            pl.BlockSpec((1, 1, TOP_K * COMBINE_TILE), lambda i: (i, 0, 0), memory_space=pltpu.SMEM),
            pl.BlockSpec(memory_space=pl.ANY),
            pl.BlockSpec((COMBINE_TILE, D_MODEL), row),
            pl.BlockSpec((COMBINE_TILE, LANES), row),
        ],
        out_specs=pl.BlockSpec((COMBINE_TILE, D_MODEL), row),
        out_shape=jax.ShapeDtypeStruct((n, D_MODEL), F32),
        scratch_shapes=[pltpu.VMEM((TOP_K, COMBINE_TILE, *ROW_SLAB), F32), pltpu.SemaphoreType.DMA(())],
        compiler_params=_cparams(("arbitrary",)),
        name="moe_combine",
    )(pos3d, y_pad, x2d, route)


def _moe_layer(x1, hn, route, wg, wu, wd):
    n = x1.shape[0]
    e_flat = route[:, :TOP_K].astype(jnp.int32).reshape(-1)
    onehot = (e_flat[:, None] == jnp.arange(N_EXPERTS, dtype=jnp.int32)[None, :]).astype(jnp.int32)
    csum = jnp.cumsum(onehot, axis=0)
    counts = csum[-1]
    padded = (counts + MOE_BLOCK - 1) // MOE_BLOCK * MOE_BLOCK
    pad_ends = jnp.cumsum(padded)
    pad_starts = pad_ends - padded
    dest = jnp.sum((csum - onehot + pad_starts[None, :]) * onehot, axis=1)
    n_blocks = -(-(n * TOP_K) // MOE_BLOCK) + N_EXPERTS
    n_rows = n_blocks * MOE_BLOCK
    block_start = jnp.arange(n_blocks, dtype=jnp.int32) * MOE_BLOCK
    block_expert = jnp.minimum(
        jnp.sum((pad_ends[None, :] <= block_start[:, None]).astype(jnp.int32), axis=1), N_EXPERTS - 1)
    n_active = (pad_ends[-1:] // MOE_BLOCK).astype(jnp.int32)
    fill = jnp.concatenate([pad_starts + counts, pad_starts[1:], jnp.full((1,), n_rows, jnp.int32)])

    x_pad = _dispatch_call(fill.astype(jnp.int32), dest.reshape(n // DISPATCH_TILE, 1, TOP_K * DISPATCH_TILE),
                           hn, n_rows)
    y_pad = _moe_call(block_expert.astype(jnp.int32), n_active, x_pad, wg, wu, wd)
    pos3d = dest.reshape(n // COMBINE_TILE, 1, TOP_K * COMBINE_TILE)
    return _combine_call(pos3d, y_pad, x1, route)


def _head_group_matrix():
    idx = np.arange(MXU_DIM) // HEAD_DIM
    return jnp.asarray((idx[:, None] == idx[None, :]).astype(np.float32), dtype=BF16)


def _pack_router(w_router):
    hi = w_router.astype(BF16)
    lo = (w_router - hi.astype(F32)).astype(BF16)
    pad = jnp.zeros((D_MODEL, LANES - 2 * N_EXPERTS), BF16)
    return jnp.concatenate([hi, lo, pad], axis=1)


def kernel(x, norm_mix_g, norm_ffn_g, w_in, w_out, q_norm_a, k_norm_a, q_norm_b, k_norm_b,
           lambda_q1, lambda_k1, lambda_q2, lambda_k2, subln_g, rel_bias,
           w_gate_dense, w_up_dense, w_down_dense, w_router, w_gate_moe, w_up_moe, w_down_moe):
    b, s, d = x.shape
    assert (s, d) == (SEQ, D_MODEL)
    depth = w_in.shape[0]
    n = b * s
    x2d = x.reshape(n, d)
    gmat = _head_group_matrix()
    bias_a = _bias_tables(rel_bias, _dilated_buckets(), 0, N_HEADS_A)
    bias_b = _bias_tables(rel_bias, _diff_buckets(), N_HEADS_A, N_HEADS_B)
    far_b = rel_bias[_far_bucket(), N_HEADS_A:]
    ones = jnp.ones((PROJ_COL_TILE,), F32)
    rep = PROJ_COL_TILE // HEAD_DIM

    for layer in range(depth):
        gains = jnp.concatenate([
            jnp.tile(q_norm_a[layer], rep) * (ATTN_SCALE * LOG2E), jnp.tile(k_norm_a[layer], rep), ones,
            jnp.tile(q_norm_b[layer], rep) * (ATTN_SCALE * LOG2E), jnp.tile(k_norm_b[layer], rep), ones,
        ]).reshape(1, PROJ_WIDTH)
        proj = _proj_call(x2d, norm_mix_g[layer].reshape(1, d), w_in[layer].astype(BF16), gains, gmat)
        proj3d = proj.reshape(b, s, PROJ_WIDTH)

        oa = _mixer_a_call(proj3d, bias_a)
        lam_init = 0.8 - 0.6 * math.exp(-0.3 * layer)
        lam_rows = jnp.zeros((8, LANES), F32).at[:4, :HEAD_DIM].set(
            jnp.stack([lambda_q1[layer], lambda_k1[layer], lambda_q2[layer], lambda_k2[layer]]))
        ob = _mixer_b_call(proj3d, bias_b, far_b, lam_rows, subln_g[layer].reshape(1, LANES), lam_init)

        is_moe = layer % 2 == 1
        i = layer // 2
        mixed = (oa.reshape(n, WIDTH_A), ob.reshape(n, WIDTH_B), w_out[layer].astype(BF16), x2d,
                 norm_ffn_g[layer].reshape(1, d))
        if is_moe:
            x1, hn, route = _out_proj_router_call(*mixed, _pack_router(w_router[i]))
            x2d = _moe_layer(x1, hn, route, w_gate_moe[i].astype(BF16), w_up_moe[i].astype(BF16),
                             w_down_moe[i].astype(BF16))
        else:
            x2d = _out_proj_ffn_call(*mixed, w_gate_dense[i].astype(BF16), w_up_dense[i].astype(BF16),
                                     w_down_dense[i].astype(BF16))
    return x2d.reshape(b, s, d)
```

```python
import functools
import math

import numpy as np
import jax
import jax.numpy as jnp
from jax import lax
from jax.experimental import pallas as pl
from jax.experimental.pallas import tpu as pltpu

F32 = jnp.float32
BF16 = jnp.bfloat16

D_MODEL = 1024
SEQ = 2048
HEAD_DIM = 64
ATTN_SCALE = HEAD_DIM ** -0.5
LOG2E = math.log2(math.e)
N_HEADS_A = 8
DIL_PATTERNS = ((128, 1), (512, 4), (2048, 16))
DIL_BLOCK = 128
N_HEADS_B = 4
WIDTH_A = N_HEADS_A * HEAD_DIM
WIDTH_B = N_HEADS_B * 2 * HEAD_DIM
PROJ_WIDTH = 3 * (WIDTH_A + WIDTH_B)
N_BUCKETS = 32
MAX_DISTANCE = 128
D_FF = 2816
N_EXPERTS = 8
TOP_K = 2
D_FF_EXPERT = 3584
EPS = 1e-6
NEG = -1e30

LANES = 128
MXU_DIM = 256
VMEM_LIMIT = 56 * 1024 * 1024

ROW_TILE = 512
PROJ_COL_TILE = 512
ROUTER_ROWS = 128
Q_TILE_B = 512
DIL_GROUP = 16
MOE_BLOCK = 512
MOE_FF_TILE = 512
FFN_CHUNKS = (512, 512, 512, 512, 512, 256)
DISPATCH_TILE = 256
COMBINE_TILE = 256


def _cparams(sem):
    return pltpu.CompilerParams(dimension_semantics=sem, vmem_limit_bytes=VMEM_LIMIT)


def _resident(shape):
    nd = len(shape)
    return pl.BlockSpec(shape, lambda *_: (0,) * nd, pipeline_mode=pl.Buffered(1))


def _t5_bucket_np(dist):
    n = np.maximum(dist, 0)
    max_exact = N_BUCKETS // 2
    large = max_exact + (np.log(np.maximum(n, 1).astype(np.float32) / max_exact)
                         / math.log(MAX_DISTANCE / max_exact) * (N_BUCKETS - max_exact)).astype(np.int32)
    large = np.minimum(large, N_BUCKETS - 1)
    return np.where(n < max_exact, n, large).astype(np.int32)


def _dilated_buckets():
    i = np.arange(DIL_BLOCK)[:, None]
    j = np.arange(2 * DIL_BLOCK)[None, :]
    steps = DIL_BLOCK + i - j
    valid = (steps >= 0) & (steps <= DIL_BLOCK)
    tabs = []
    for _, dil in DIL_PATTERNS:
        bucket = np.where(valid, _t5_bucket_np(steps * dil), -1)
        tabs.append(bucket)
        tabs.append(np.where(j >= DIL_BLOCK, bucket, -1))
    return np.stack(tabs).astype(np.int32)


def _diff_buckets():
    t = Q_TILE_B
    qi = np.arange(t)[:, None]
    kj = np.arange(t)[None, :]
    tabs = []
    for off in (0, t):
        dist = qi - kj + off
        tabs.append(np.where(dist >= 0, _t5_bucket_np(dist), -1))
    return np.stack(tabs).astype(np.int32)


def _far_bucket():
    b = _t5_bucket_np(np.arange(Q_TILE_B + 1, SEQ + 1))
    assert (b == b[0]).all()
    return int(b[0])


def _bias_table_kernel(rel_ref, bucket_ref, o_ref, *, head0):
    col = head0 + pl.program_id(1)
    bucket = bucket_ref[0]
    val = jnp.full(bucket.shape, NEG, F32)
    for b in range(N_BUCKETS):
        val = jnp.where(bucket == b, rel_ref[b, col] * LOG2E, val)
    o_ref[0, 0] = val


def _bias_tables(rel_bias, buckets, head0, n_heads):
    n_tab, rows, cols = buckets.shape
    return pl.pallas_call(
        functools.partial(_bias_table_kernel, head0=head0),
        grid=(n_tab, n_heads),
        in_specs=[
            pl.BlockSpec(memory_space=pltpu.SMEM),
            pl.BlockSpec((1, rows, cols), lambda t, h: (t, 0, 0)),
        ],
        out_specs=pl.BlockSpec((1, 1, rows, cols), lambda t, h: (t, h, 0, 0)),
        out_shape=jax.ShapeDtypeStruct((n_tab, n_heads, rows, cols), F32),
        compiler_params=_cparams(("parallel", "parallel")),
        name="bias_tables",
    )(rel_bias, jnp.asarray(buckets))


def _proj_kernel(x_ref, g_ref, w_ref, gain_ref, gmat_ref, o_ref):
    x = x_ref[...]
    ms = jnp.mean(x * x, axis=-1, keepdims=True)
    xn = (x * lax.rsqrt(ms + EPS) * g_ref[...]).astype(BF16)
    for j in range(PROJ_WIDTH // PROJ_COL_TILE):
        cols = slice(j * PROJ_COL_TILE, (j + 1) * PROJ_COL_TILE)
        acc = jnp.dot(xn, w_ref[:, cols], preferred_element_type=F32)
        if j % 3 == 2:
            o_ref[:, cols] = acc.astype(BF16)
        else:
            sq = (acc * acc).astype(BF16)
            ss = jnp.concatenate(
                [jnp.dot(sq[:, c:c + MXU_DIM], gmat_ref[...], preferred_element_type=F32)
                 for c in range(0, PROJ_COL_TILE, MXU_DIM)], axis=1)
            o_ref[:, cols] = (acc * lax.rsqrt(ss * (1.0 / HEAD_DIM) + EPS) * gain_ref[:, cols]).astype(BF16)


def _proj_call(x2d, g, w_bf16, gains, gmat):
    n = x2d.shape[0]
    return pl.pallas_call(
        _proj_kernel,
        grid=(n // ROW_TILE,),
        in_specs=[
            pl.BlockSpec((ROW_TILE, D_MODEL), lambda i: (i, 0)),
            _resident((1, D_MODEL)),
            _resident((D_MODEL, PROJ_WIDTH)),
            _resident((1, PROJ_WIDTH)),
            _resident((MXU_DIM, MXU_DIM)),
        ],
        out_specs=pl.BlockSpec((ROW_TILE, PROJ_WIDTH), lambda i: (i, 0)),
        out_shape=jax.ShapeDtypeStruct((n, PROJ_WIDTH), BF16),
        compiler_params=_cparams(("parallel",)),
        name="norm_in_proj",
    )(x2d, g, w_bf16, gains, gmat)


def _deinterleave(x, dil):
    if dil == 1:
        return x
    return jnp.swapaxes(x.reshape(SEQ // dil, dil, LANES), 0, 1).reshape(SEQ, LANES)


def _interleave(x, dil):
    if dil == 1:
        return x
    return jnp.swapaxes(x.reshape(dil, SEQ // dil, LANES), 0, 1).reshape(SEQ, LANES)


def _mixer_a_kernel(q_ref, k_ref, v_ref, bias_ref, o_ref, qp, kp, vp, out_s, lse_s):
    n_tiles = SEQ // DIL_BLOCK
    n_branch = len(DIL_PATTERNS)
    lane_full = lax.broadcasted_iota(jnp.int32, (SEQ, LANES), 1)
    lane = lax.broadcasted_iota(jnp.int32, (DIL_BLOCK, LANES), 1)
    head0 = lane < HEAD_DIM

    q = q_ref[0].astype(F32)
    k = k_ref[0].astype(F32)
    v = v_ref[0].astype(F32)
    pad = jnp.zeros((DIL_BLOCK, LANES), BF16)
    for bi, (_, dil) in enumerate(DIL_PATTERNS):
        qd = _deinterleave(q, dil)
        qp[bi, 0] = jnp.where(lane_full < HEAD_DIM, qd, 0.0).astype(BF16)
        qp[bi, 1] = jnp.where(lane_full >= HEAD_DIM, qd, 0.0).astype(BF16)
        kp[bi, :DIL_BLOCK, :] = pad
        vp[bi, :DIL_BLOCK, :] = pad
        kp[bi, DIL_BLOCK:, :] = _deinterleave(k, dil).astype(BF16)
        vp[bi, DIL_BLOCK:, :] = _deinterleave(v, dil).astype(BF16)

    for bi, (_, dil) in enumerate(DIL_PATTERNS):
        nb = SEQ // dil // DIL_BLOCK

        def tile(t, bi=bi, nb=nb):
            row0 = t * DIL_BLOCK if isinstance(t, int) else pl.multiple_of(t * DIL_BLOCK, DIL_BLOCK)
            rows = pl.ds(row0, DIL_BLOCK)
            keys = pl.ds(row0, 2 * DIL_BLOCK)
            tab = 2 * bi + (t % nb == 0)
            kt = kp[bi, keys, :]
            vt = jnp.concatenate([vp[bi, keys, :], jnp.ones((2 * DIL_BLOCK, LANES), BF16)], axis=1)
            m_h, pv_h = [], []
            for hh in range(2):
                s = lax.dot_general(qp[bi, hh, rows, :], kt, (((1,), (1,)), ((), ())),
                                    preferred_element_type=F32) + bias_ref[tab, hh]
                m = jnp.broadcast_to(jnp.max(s, axis=-1, keepdims=True), (DIL_BLOCK, LANES))
                p = jnp.exp2(s - jnp.concatenate([m, m], axis=1))
                m_h.append(m)
                pv_h.append(jnp.dot(p.astype(BF16), vt, preferred_element_type=F32))
            l = jnp.where(head0, pv_h[0][:, LANES:], pv_h[1][:, LANES:])
            out_s[bi, rows, :] = jnp.where(head0, pv_h[0][:, :LANES], pv_h[1][:, :LANES]) / l
            lse_s[bi, rows, :] = jnp.where(head0, m_h[0], m_h[1]) + jnp.log2(l)

        if DIL_GROUP == n_tiles:
            for t in range(n_tiles):
                tile(t)
        else:
            def group(g, carry, tile=tile):
                for u in range(DIL_GROUP):
                    tile(g * DIL_GROUP + u)
                return carry

            lax.fori_loop(0, n_tiles // DIL_GROUP, group, 0)

    lse_b = [_interleave(lse_s[bi], dil) for bi, (_, dil) in enumerate(DIL_PATTERNS)]
    lse_max = functools.reduce(jnp.maximum, lse_b)
    w_b = [jnp.exp2(lse - lse_max) for lse in lse_b]
    num = sum(w * _interleave(out_s[bi], dil) for w, (bi, (_, dil)) in zip(w_b, enumerate(DIL_PATTERNS)))
    o_ref[0] = (num / sum(w_b)).astype(o_ref.dtype)


def _mixer_a_call(proj3d, bias_a):
    b = proj3d.shape[0]
    n_pairs = N_HEADS_A // 2
    n_branch = len(DIL_PATTERNS)
    q_blk0, k_blk0, v_blk0 = 0, WIDTH_A // LANES, 2 * WIDTH_A // LANES
    return pl.pallas_call(
        _mixer_a_kernel,
        grid=(b, n_pairs),
        in_specs=[
            pl.BlockSpec((1, SEQ, LANES), lambda i, p: (i, 0, q_blk0 + p)),
            pl.BlockSpec((1, SEQ, LANES), lambda i, p: (i, 0, k_blk0 + p)),
            pl.BlockSpec((1, SEQ, LANES), lambda i, p: (i, 0, v_blk0 + p)),
            pl.BlockSpec((bias_a.shape[0], 2, DIL_BLOCK, 2 * DIL_BLOCK), lambda i, p: (0, p, 0, 0)),
        ],
        out_specs=pl.BlockSpec((1, SEQ, LANES), lambda i, p: (i, 0, p)),
        out_shape=jax.ShapeDtypeStruct((b, SEQ, WIDTH_A), BF16),
        scratch_shapes=[
            pltpu.VMEM((n_branch, 2, SEQ, LANES), BF16),
            pltpu.VMEM((n_branch, SEQ + DIL_BLOCK, LANES), BF16),
            pltpu.VMEM((n_branch, SEQ + DIL_BLOCK, LANES), BF16),
            pltpu.VMEM((n_branch, SEQ, LANES), F32),
            pltpu.VMEM((n_branch, SEQ, LANES), F32),
        ],
        compiler_params=_cparams(("parallel", "parallel")),
        name="dilated_attention",
    )(proj3d, proj3d, proj3d, bias_a)


def _mixer_b_kernel(far_ref, lam_ref, q_ref, k_ref, v_ref, bias_ref, subg_ref, o_ref, *, lam_init):
    pair = pl.program_id(1)
    t = Q_TILE_B

    def run(n_q):
        lane = lax.broadcasted_iota(jnp.int32, (t, LANES), 1)
        q_maps = []
        for hd in range(2):
            q = q_ref[0, :, hd * LANES:(hd + 1) * LANES]
            zero = jnp.zeros_like(q)
            q_maps.append((jnp.where(lane < HEAD_DIM, q, zero), jnp.where(lane >= HEAD_DIM, q, zero)))
        ones = jnp.ones((t, LANES), BF16)
        state = [None] * 4

        def key_tile(j, kind):
            keys = slice(j * t, (j + 1) * t)
            for hd in range(2):
                kt = k_ref[0, keys, hd * LANES:(hd + 1) * LANES]
                vt = jnp.concatenate([v_ref[0, keys, hd * LANES:(hd + 1) * LANES], ones], axis=1)
                for half in range(2):
                    c = 2 * hd + half
                    s = lax.dot_general(q_maps[hd][half], kt, (((1,), (1,)), ((), ())),
                                        preferred_element_type=F32)
                    if kind == "far":
                        bias = far_ref[2 * pair + hd] * LOG2E
                        m_cur = jnp.broadcast_to(jnp.max(s, axis=-1, keepdims=True) + bias, (t, LANES))
                    else:
                        s = s + bias_ref[1 if kind == "sub" else 0, hd]
                        m_cur = jnp.broadcast_to(jnp.max(s, axis=-1, keepdims=True), (t, LANES))
                    m_new = m_cur if state[c] is None else jnp.maximum(state[c][0], m_cur)
                    shift = m_new - bias if kind == "far" else m_new
                    p = jnp.exp2(s - jnp.concatenate([shift] * (t // LANES), axis=1))
                    pv = jnp.dot(p.astype(BF16), vt, preferred_element_type=F32)
                    if state[c] is not None:
                        alpha = jnp.exp2(state[c][0] - m_new)
                        pv = state[c][1] * jnp.concatenate([alpha, alpha], axis=1) + pv
                    state[c] = (m_new, pv)

        for j in range(n_q - 1):
            key_tile(j, "far")
        if n_q > 0:
            key_tile(n_q - 1, "sub")
        key_tile(n_q, "diag")

        lam_rows = lam_ref[...]
        e1 = jnp.exp(jnp.sum(lam_rows[0:1, :] * lam_rows[1:2, :], axis=-1, keepdims=True))
        e2 = jnp.exp(jnp.sum(lam_rows[2:3, :] * lam_rows[3:4, :], axis=-1, keepdims=True))
        lam = e1 - e2 + lam_init
        for hd in range(2):
            a1, a2 = state[2 * hd][1], state[2 * hd + 1][1]
            o = a1[:, :LANES] / a1[:, LANES:] - lam * (a2[:, :LANES] / a2[:, LANES:])
            ms = jnp.mean(o * o, axis=-1, keepdims=True)
            o_ref[0, :, hd * LANES:(hd + 1) * LANES] = (
                o * lax.rsqrt(ms + EPS) * subg_ref[...] * (1.0 - lam_init)).astype(o_ref.dtype)

    for n_q in range(SEQ // t):
        pl.when(pl.program_id(2) == n_q)(functools.partial(run, n_q))


def _mixer_b_call(proj3d, bias_b, far_b, lam_rows, subg, lam_init):
    b = proj3d.shape[0]
    width = 2 * LANES
    base = 3 * WIDTH_A // width
    q_blk0, k_blk0, v_blk0 = base, base + WIDTH_B // width, base + 2 * WIDTH_B // width
    t = Q_TILE_B
    grid_spec = pltpu.PrefetchScalarGridSpec(
        num_scalar_prefetch=1,
        grid=(b, N_HEADS_B // 2, SEQ // t),
        in_specs=[
            pl.BlockSpec((8, LANES), lambda bi, p, i, far: (0, 0)),
            pl.BlockSpec((1, t, width), lambda bi, p, i, far: (bi, i, q_blk0 + p)),
            pl.BlockSpec((1, SEQ, width), lambda bi, p, i, far: (bi, 0, k_blk0 + p)),
            pl.BlockSpec((1, SEQ, width), lambda bi, p, i, far: (bi, 0, v_blk0 + p)),
            pl.BlockSpec((2, 2, t, t), lambda bi, p, i, far: (0, p, 0, 0)),
            pl.BlockSpec((1, LANES), lambda bi, p, i, far: (0, 0)),
        ],
        out_specs=pl.BlockSpec((1, t, width), lambda bi, p, i, far: (bi, i, p)),
    )
    return pl.pallas_call(
        functools.partial(_mixer_b_kernel, lam_init=lam_init),
        grid_spec=grid_spec,
        out_shape=jax.ShapeDtypeStruct((b, SEQ, WIDTH_B), BF16),
        compiler_params=_cparams(("parallel", "parallel", "arbitrary")),
        name="diff_attention",
    )(far_b, lam_rows, proj3d, proj3d, proj3d, bias_b, subg)


def _mixer_out_residual_norm(oa_ref, ob_ref, w_ref, x_ref, g_ref, rows=slice(None)):
    acc = jnp.dot(oa_ref[rows, :], w_ref[:WIDTH_A, :], preferred_element_type=F32)
    acc = acc + jnp.dot(ob_ref[rows, :], w_ref[WIDTH_A:, :], preferred_element_type=F32)
    x1 = x_ref[rows, :] + acc
    ms = jnp.mean(x1 * x1, axis=-1, keepdims=True)
    return x1, x1 * lax.rsqrt(ms + EPS) * g_ref[...]


def _out_proj_router_kernel(oa_ref, ob_ref, w_ref, x_ref, g_ref, wr_ref, x1_ref, h_ref, route_ref):
    for r0 in range(0, ROW_TILE, ROUTER_ROWS):
        rows = slice(r0, r0 + ROUTER_ROWS)
        x1, hn = _mixer_out_residual_norm(oa_ref, ob_ref, w_ref, x_ref, g_ref, rows)
        x1_ref[rows, :] = x1
        h_ref[rows] = _to_slabs(hn)
        hi = hn.astype(BF16)
        lo = (hn - hi.astype(F32)).astype(BF16)
        lg = (jnp.dot(hi, wr_ref[...], preferred_element_type=F32)
              + jnp.dot(lo, wr_ref[...], preferred_element_type=F32))
        lg = lg + pltpu.roll(lg, LANES - N_EXPERTS, 1)
        lane = lax.broadcasted_iota(jnp.int32, lg.shape, 1)
        lane_f = lane.astype(F32)
        lg = jnp.where(lane < N_EXPERTS, lg, -jnp.inf)
        v1 = jnp.max(lg, axis=-1, keepdims=True)
        i1 = jnp.min(jnp.where(lg == v1, lane_f, float(LANES)), axis=-1, keepdims=True)
        lg2 = jnp.where(lane_f == i1, -jnp.inf, lg)
        v2 = jnp.max(lg2, axis=-1, keepdims=True)
        i2 = jnp.min(jnp.where(lg2 == v2, lane_f, float(LANES)), axis=-1, keepdims=True)
        e = jnp.exp(v2 - v1)
        g1 = 1.0 / (1.0 + e)
        g2 = e / (1.0 + e)
        route_ref[rows, :] = jnp.where(lane == 0, i1, jnp.where(lane == 1, i2,
                                       jnp.where(lane == 2, g1, jnp.where(lane == 3, g2, 0.0))))


def _mixer_out_specs(row):
    return [
        pl.BlockSpec((ROW_TILE, WIDTH_A), row),
        pl.BlockSpec((ROW_TILE, WIDTH_B), row),
        _resident((WIDTH_A + WIDTH_B, D_MODEL)),
        pl.BlockSpec((ROW_TILE, D_MODEL), row),
        _resident((1, D_MODEL)),
    ]


def _out_proj_router_call(oa, ob, w_bf16, x2d, g, w_router_packed):
    n = x2d.shape[0]
    row = lambda i: (i, 0)
    return pl.pallas_call(
        _out_proj_router_kernel,
        grid=(n // ROW_TILE,),
        in_specs=_mixer_out_specs(row) + [_resident((D_MODEL, LANES))],
        out_specs=[
            pl.BlockSpec((ROW_TILE, D_MODEL), row),
            pl.BlockSpec((ROW_TILE, *ROW_SLAB), lambda i: (i, 0, 0)),
            pl.BlockSpec((ROW_TILE, LANES), row),
        ],
        out_shape=[
            jax.ShapeDtypeStruct((n, D_MODEL), F32),
            jax.ShapeDtypeStruct((n, *ROW_SLAB), F32),
            jax.ShapeDtypeStruct((n, LANES), F32),
        ],
        compiler_params=_cparams(("parallel",)),
        name="out_proj_router",
    )(oa, ob, w_bf16, x2d, g, w_router_packed)


def _swiglu_act(g, u):
    return (g / (1.0 + jnp.exp(-g))) * u


def _out_proj_ffn_kernel(oa_ref, ob_ref, w_ref, x_ref, g_ref, wg_ref, wu_ref, wd_ref, o_ref, a_ref):
    x1, hn = _mixer_out_residual_norm(oa_ref, ob_ref, w_ref, x_ref, g_ref)
    h = hn.astype(BF16)
    c0 = 0
    for width in FFN_CHUNKS:
        g = jnp.dot(h, wg_ref[:, c0:c0 + width], preferred_element_type=F32)
        u = jnp.dot(h, wu_ref[:, c0:c0 + width], preferred_element_type=F32)
        a_ref[:, c0:c0 + width] = _swiglu_act(g, u).astype(BF16)
        c0 += width
    o_ref[...] = x1 + jnp.dot(a_ref[...], wd_ref[...], preferred_element_type=F32)


def _out_proj_ffn_call(oa, ob, w_bf16, x2d, g, wg, wu, wd):
    n = x2d.shape[0]
    row = lambda i: (i, 0)
    return pl.pallas_call(
        _out_proj_ffn_kernel,
        grid=(n // ROW_TILE,),
        in_specs=_mixer_out_specs(row) + [
            _resident((D_MODEL, D_FF)),
            _resident((D_MODEL, D_FF)),
            _resident((D_FF, D_MODEL)),
        ],
        out_specs=pl.BlockSpec((ROW_TILE, D_MODEL), row),
        out_shape=jax.ShapeDtypeStruct((n, D_MODEL), F32),
        scratch_shapes=[pltpu.VMEM((ROW_TILE, D_FF), BF16)],
        compiler_params=_cparams(("parallel",)),
        name="out_proj_dense_swiglu",
    )(oa, ob, w_bf16, x2d, g, wg, wu, wd)


ROW_SLAB = (8, LANES)
assert ROW_SLAB[0] * ROW_SLAB[1] == D_MODEL


def _to_slabs(x):
    r = x.shape[0]
    xs = jnp.stack([x[:, LANES * j:LANES * (j + 1)] for j in range(ROW_SLAB[0])], axis=0)
    xs = xs.reshape(ROW_SLAB[0], r // 8, 8, LANES)
    xs = jnp.swapaxes(jnp.swapaxes(xs, 0, 1), 1, 2)
    return xs.reshape(r, *ROW_SLAB)


def _from_slabs(t):
    r = t.shape[0]
    y = jnp.swapaxes(t.reshape(r // 8, 8, *ROW_SLAB), 1, 2)
    return jnp.concatenate([y[:, j].reshape(r, LANES) for j in range(ROW_SLAB[0])], axis=1)


def _row_copy(src_ref, src_row, dst_ref, dst_row, sem):
    return pltpu.make_async_copy(src_ref.at[src_row], dst_ref.at[dst_row], sem)


def _dispatch_kernel(fill_ref, dest_ref, h_ref, o_hbm, zero_ref, sem):
    i = pl.program_id(0)

    def start(t, c):
        for k in range(TOP_K):
            _row_copy(h_ref, t, o_hbm, dest_ref[0, 0, TOP_K * t + k], sem).start(priority=k)
        return c

    def wait(t, c):
        for k in range(TOP_K):
            _row_copy(h_ref, t, o_hbm, 0, sem).wait()
        return c

    lax.fori_loop(0, DISPATCH_TILE, start, 0, unroll=4)
    lax.fori_loop(0, DISPATCH_TILE, wait, 0, unroll=4)

    @pl.when(i == pl.num_programs(0) - 1)
    def _():
        zero_ref[...] = jnp.zeros_like(zero_ref)
        for e in range(N_EXPERTS):
            lo, hi = fill_ref[e], fill_ref[N_EXPERTS + e]
            lax.fori_loop(lo, hi, lambda r, c: (_row_copy(zero_ref, 0, o_hbm, r, sem).start(), c)[1], 0)
            lax.fori_loop(lo, hi, lambda r, c: (_row_copy(zero_ref, 0, o_hbm, 0, sem).wait(), c)[1], 0)


def _dispatch_call(fill, dest3d, h, n_rows):
    n = h.shape[0]
    grid_spec = pltpu.PrefetchScalarGridSpec(
        num_scalar_prefetch=1,
        grid=(n // DISPATCH_TILE,),
        in_specs=[
            pl.BlockSpec((1, 1, TOP_K * DISPATCH_TILE), lambda i, f: (i, 0, 0), memory_space=pltpu.SMEM),
            pl.BlockSpec((DISPATCH_TILE, *ROW_SLAB), lambda i, f: (i, 0, 0)),
        ],
        out_specs=pl.BlockSpec(memory_space=pl.ANY),
        scratch_shapes=[pltpu.VMEM((1, *ROW_SLAB), h.dtype), pltpu.SemaphoreType.DMA(())],
    )
    return pl.pallas_call(
        _dispatch_kernel,
        grid_spec=grid_spec,
        out_shape=jax.ShapeDtypeStruct((n_rows, *ROW_SLAB), h.dtype),
        compiler_params=_cparams(("arbitrary",)),
        name="moe_dispatch",
    )(fill, dest3d, h)


def _moe_kernel(be_ref, nact_ref, x_ref, wg_ref, wu_ref, wd_ref, o_ref, a_ref):
    active = pl.program_id(0) < nact_ref[0]

    @pl.when(active)
    def _():
        xb = _from_slabs(x_ref[...]).astype(BF16)
        for c0 in range(0, D_FF_EXPERT, MOE_FF_TILE):
            cols = slice(c0, c0 + MOE_FF_TILE)
            g = jnp.dot(xb, wg_ref[0, :, cols], preferred_element_type=F32)
            u = jnp.dot(xb, wu_ref[0, :, cols], preferred_element_type=F32)
            a_ref[:, cols] = _swiglu_act(g, u).astype(BF16)
        o_ref[...] = _to_slabs(jnp.dot(a_ref[...], wd_ref[0], preferred_element_type=F32))

    @pl.when(jnp.logical_not(active))
    def _():
        o_ref[...] = jnp.zeros_like(o_ref)


def _moe_call(block_expert, n_active, x_pad, wg, wu, wd):
    n_rows = x_pad.shape[0]
    expert = lambda b, be, na: (be[b], 0, 0)
    grid_spec = pltpu.PrefetchScalarGridSpec(
        num_scalar_prefetch=2,
        grid=(n_rows // MOE_BLOCK,),
        in_specs=[
            pl.BlockSpec((MOE_BLOCK, *ROW_SLAB), lambda b, be, na: (b, 0, 0)),
            pl.BlockSpec((1, D_MODEL, D_FF_EXPERT), expert, pipeline_mode=pl.Buffered(1)),
            pl.BlockSpec((1, D_MODEL, D_FF_EXPERT), expert, pipeline_mode=pl.Buffered(1)),
            pl.BlockSpec((1, D_FF_EXPERT, D_MODEL), expert),
        ],
        out_specs=pl.BlockSpec((MOE_BLOCK, *ROW_SLAB), lambda b, be, na: (b, 0, 0)),
        scratch_shapes=[pltpu.VMEM((MOE_BLOCK, D_FF_EXPERT), BF16)],
    )
    return pl.pallas_call(
        _moe_kernel,
        grid_spec=grid_spec,
        out_shape=jax.ShapeDtypeStruct((n_rows, *ROW_SLAB), F32),
        compiler_params=_cparams(("arbitrary",)),
        name="moe_swiglu",
    )(block_expert, n_active, x_pad, wg, wu, wd)


def _combine_kernel(pos_ref, pos_next_ref, y_hbm, x_ref, route_ref, o_ref, buf, sems):
    i = pl.program_id(0)
    slot = i % 2

    def gather(idx_ref, into):
        def start(t, c):
            for k in range(TOP_K):
                _row_copy(y_hbm, idx_ref[0, 0, TOP_K * t + k], buf.at[into, k], t,
                          sems.at[into]).start(priority=k)
            return c
        lax.fori_loop(0, COMBINE_TILE, start, 0, unroll=4)

    @pl.when(i == 0)
    def _():
        gather(pos_ref, 0)

    @pl.when(i + 1 < pl.num_programs(0))
    def _():
        gather(pos_next_ref, 1 - slot)

    def wait(t, c):
        for k in range(TOP_K):
            _row_copy(y_hbm, 0, buf.at[slot, k], t, sems.at[slot]).wait()
        return c

    lax.fori_loop(0, COMBINE_TILE, wait, 0, unroll=4)
    route = route_ref[...]
    g0 = route[:, TOP_K:TOP_K + 1]
    g1 = route[:, TOP_K + 1:TOP_K + 2]
    o_ref[...] = x_ref[...] + g0 * _from_slabs(buf[slot, 0]) + g1 * _from_slabs(buf[slot, 1])


def _combine_call(pos3d, y_pad, x2d, route):
    n = x2d.shape[0]
    n_tiles = n // COMBINE_TILE
    row = lambda i: (i, 0)
    idx_block = (1, 1, TOP_K * COMBINE_TILE)
    return pl.pallas_call(
        _combine_kernel,
        grid=(n_tiles,),
        in_specs=[
            pl.BlockSpec(idx_block, lambda i: (i, 0, 0), memory_space=pltpu.SMEM),
            pl.BlockSpec(idx_block, lambda i: (jnp.minimum(i + 1, n_tiles - 1), 0, 0), memory_space=pltpu.SMEM),
            pl.BlockSpec(memory_space=pl.ANY),
            pl.BlockSpec((COMBINE_TILE, D_MODEL), row),
            pl.BlockSpec((COMBINE_TILE, LANES), row),
        ],
        out_specs=pl.BlockSpec((COMBINE_TILE, D_MODEL), row),
        out_shape=jax.ShapeDtypeStruct((n, D_MODEL), F32),
        scratch_shapes=[pltpu.VMEM((2, TOP_K, COMBINE_TILE, *ROW_SLAB), F32), pltpu.SemaphoreType.DMA((2,))],
        compiler_params=_cparams(("arbitrary",)),
        name="moe_combine",
    )(pos3d, pos3d, y_pad, x2d, route)


def _moe_layer(x1, hn, route, wg, wu, wd):
    n = x1.shape[0]
    e_flat = route[:, :TOP_K].astype(jnp.int32).reshape(-1)
    onehot = (e_flat[:, None] == jnp.arange(N_EXPERTS, dtype=jnp.int32)[None, :]).astype(jnp.int32)
    csum = jnp.cumsum(onehot, axis=0)
    counts = csum[-1]
    padded = (counts + MOE_BLOCK - 1) // MOE_BLOCK * MOE_BLOCK
    pad_ends = jnp.cumsum(padded)
    pad_starts = pad_ends - padded
    dest = jnp.sum((csum - onehot + pad_starts[None, :]) * onehot, axis=1)
    n_blocks = -(-(n * TOP_K) // MOE_BLOCK) + N_EXPERTS
    n_rows = n_blocks * MOE_BLOCK
    block_start = jnp.arange(n_blocks, dtype=jnp.int32) * MOE_BLOCK
    block_expert = jnp.minimum(
        jnp.sum((pad_ends[None, :] <= block_start[:, None]).astype(jnp.int32), axis=1), N_EXPERTS - 1)
    n_active = (pad_ends[-1:] // MOE_BLOCK).astype(jnp.int32)
    fill = jnp.concatenate([pad_starts + counts, pad_starts[1:], jnp.full((1,), n_rows, jnp.int32)])

    x_pad = _dispatch_call(fill.astype(jnp.int32), dest.reshape(n // DISPATCH_TILE, 1, TOP_K * DISPATCH_TILE),
                           hn, n_rows)
    y_pad = _moe_call(block_expert.astype(jnp.int32), n_active, x_pad, wg, wu, wd)
    pos3d = dest.reshape(n // COMBINE_TILE, 1, TOP_K * COMBINE_TILE)
    return _combine_call(pos3d, y_pad, x1, route)


def _head_group_matrix():
    idx = np.arange(MXU_DIM) // HEAD_DIM
    return jnp.asarray((idx[:, None] == idx[None, :]).astype(np.float32), dtype=BF16)


def _pack_router(w_router):
    hi = w_router.astype(BF16)
    lo = (w_router - hi.astype(F32)).astype(BF16)
    pad = jnp.zeros((D_MODEL, LANES - 2 * N_EXPERTS), BF16)
    return jnp.concatenate([hi, lo, pad], axis=1)


def kernel(x, norm_mix_g, norm_ffn_g, w_in, w_out, q_norm_a, k_norm_a, q_norm_b, k_norm_b,
           lambda_q1, lambda_k1, lambda_q2, lambda_k2, subln_g, rel_bias,
           w_gate_dense, w_up_dense, w_down_dense, w_router, w_gate_moe, w_up_moe, w_down_moe):
    b, s, d = x.shape
    assert (s, d) == (SEQ, D_MODEL)
    depth = w_in.shape[0]
    n = b * s
    x2d = x.reshape(n, d)
    gmat = _head_group_matrix()
    bias_a = _bias_tables(rel_bias, _dilated_buckets(), 0, N_HEADS_A)
    bias_b = _bias_tables(rel_bias, _diff_buckets(), N_HEADS_A, N_HEADS_B)
    far_b = rel_bias[_far_bucket(), N_HEADS_A:]
    ones = jnp.ones((PROJ_COL_TILE,), F32)
    rep = PROJ_COL_TILE // HEAD_DIM

    for layer in range(depth):
        gains = jnp.concatenate([
            jnp.tile(q_norm_a[layer], rep) * (ATTN_SCALE * LOG2E), jnp.tile(k_norm_a[layer], rep), ones,
            jnp.tile(q_norm_b[layer], rep) * (ATTN_SCALE * LOG2E), jnp.tile(k_norm_b[layer], rep), ones,
        ]).reshape(1, PROJ_WIDTH)
        proj = _proj_call(x2d, norm_mix_g[layer].reshape(1, d), w_in[layer].astype(BF16), gains, gmat)
        proj3d = proj.reshape(b, s, PROJ_WIDTH)

        oa = _mixer_a_call(proj3d, bias_a)
        lam_init = 0.8 - 0.6 * math.exp(-0.3 * layer)
        lam_rows = jnp.zeros((8, LANES), F32).at[:4, :HEAD_DIM].set(
            jnp.stack([lambda_q1[layer], lambda_k1[layer], lambda_q2[layer], lambda_k2[layer]]))
        ob = _mixer_b_call(proj3d, bias_b, far_b, lam_rows, subln_g[layer].reshape(1, LANES), lam_init)

        is_moe = layer % 2 == 1
        i = layer // 2
        mixed = (oa.reshape(n, WIDTH_A), ob.reshape(n, WIDTH_B), w_out[layer].astype(BF16), x2d,
                 norm_ffn_g[layer].reshape(1, d))
        if is_moe:
            x1, hn, route = _out_proj_router_call(*mixed, _pack_router(w_router[i]))
            x2d = _moe_layer(x1, hn, route, w_gate_moe[i].astype(BF16), w_up_moe[i].astype(BF16),
                             w_down_moe[i].astype(BF16))
        else:
            x2d = _out_proj_ffn_call(*mixed, w_gate_dense[i].astype(BF16), w_up_dense[i].astype(BF16),
                                     w_down_dense[i].astype(BF16))
    return x2d.reshape(b, s, d)
```

```python
import functools
import math

import numpy as np
import jax
import jax.numpy as jnp
from jax import lax
from jax.experimental import pallas as pl
from jax.experimental.pallas import tpu as pltpu

F32 = jnp.float32
BF16 = jnp.bfloat16

D_MODEL = 1024
SEQ = 2048
HEAD_DIM = 64
ATTN_SCALE = HEAD_DIM ** -0.5
LOG2E = math.log2(math.e)
N_HEADS_A = 8
DIL_PATTERNS = ((128, 1), (512, 4), (2048, 16))
DIL_BLOCK = 128
N_HEADS_B = 4
WIDTH_A = N_HEADS_A * HEAD_DIM
WIDTH_B = N_HEADS_B * 2 * HEAD_DIM
PROJ_WIDTH = 3 * (WIDTH_A + WIDTH_B)
N_BUCKETS = 32
MAX_DISTANCE = 128
D_FF = 2816
N_EXPERTS = 8
TOP_K = 2
D_FF_EXPERT = 3584
EPS = 1e-6
NEG = -1e30

LANES = 128
MXU_DIM = 256
VMEM_LIMIT = 56 * 1024 * 1024

ROW_TILE = 512
PROJ_COL_TILE = 512
ROUTER_ROWS = 128
Q_TILE_B = 512
DIL_GROUP = 16
MOE_BLOCK = 512
MOE_FF_TILE = 512
FFN_CHUNKS = (512, 512, 512, 512, 512, 256)
DISPATCH_TILE = 256
COMBINE_TILE = 256


def _cparams(sem):
    return pltpu.CompilerParams(dimension_semantics=sem, vmem_limit_bytes=VMEM_LIMIT)


def _resident(shape):
    nd = len(shape)
    return pl.BlockSpec(shape, lambda *_: (0,) * nd, pipeline_mode=pl.Buffered(1))


def _t5_bucket_np(dist):
    n = np.maximum(dist, 0)
    max_exact = N_BUCKETS // 2
    large = max_exact + (np.log(np.maximum(n, 1).astype(np.float32) / max_exact)
                         / math.log(MAX_DISTANCE / max_exact) * (N_BUCKETS - max_exact)).astype(np.int32)
    large = np.minimum(large, N_BUCKETS - 1)
    return np.where(n < max_exact, n, large).astype(np.int32)


def _dilated_buckets():
    i = np.arange(DIL_BLOCK)[:, None]
    j = np.arange(2 * DIL_BLOCK)[None, :]
    steps = DIL_BLOCK + i - j
    valid = (steps >= 0) & (steps <= DIL_BLOCK)
    tabs = []
    for _, dil in DIL_PATTERNS:
        bucket = np.where(valid, _t5_bucket_np(steps * dil), -1)
        tabs.append(bucket)
        tabs.append(np.where(j >= DIL_BLOCK, bucket, -1))
    return np.stack(tabs).astype(np.int32)


def _diff_buckets():
    t = Q_TILE_B
    qi = np.arange(t)[:, None]
    kj = np.arange(t)[None, :]
    tabs = []
    for off in (0, t):
        dist = qi - kj + off
        tabs.append(np.where(dist >= 0, _t5_bucket_np(dist), -1))
    return np.stack(tabs).astype(np.int32)


def _far_bucket():
    b = _t5_bucket_np(np.arange(Q_TILE_B + 1, SEQ + 1))
    assert (b == b[0]).all()
    return int(b[0])


def _bias_table_kernel(rel_ref, bucket_ref, o_ref, *, head0):
    col = head0 + pl.program_id(1)
    bucket = bucket_ref[0]
    val = jnp.full(bucket.shape, NEG, F32)
    for b in range(N_BUCKETS):
        val = jnp.where(bucket == b, rel_ref[b, col] * LOG2E, val)
    o_ref[0, 0] = val


def _bias_tables(rel_bias, buckets, head0, n_heads):
    n_tab, rows, cols = buckets.shape
    return pl.pallas_call(
        functools.partial(_bias_table_kernel, head0=head0),
        grid=(n_tab, n_heads),
        in_specs=[
            pl.BlockSpec(memory_space=pltpu.SMEM),
            pl.BlockSpec((1, rows, cols), lambda t, h: (t, 0, 0)),
        ],
        out_specs=pl.BlockSpec((1, 1, rows, cols), lambda t, h: (t, h, 0, 0)),
        out_shape=jax.ShapeDtypeStruct((n_tab, n_heads, rows, cols), F32),
        compiler_params=_cparams(("parallel", "parallel")),
        name="bias_tables",
    )(rel_bias, jnp.asarray(buckets))


def _proj_kernel(x_ref, g_ref, w_ref, gain_ref, gmat_ref, o_ref):
    x = x_ref[...]
    ms = jnp.mean(x * x, axis=-1, keepdims=True)
    xn = (x * lax.rsqrt(ms + EPS) * g_ref[...]).astype(BF16)
    for j in range(PROJ_WIDTH // PROJ_COL_TILE):
        cols = slice(j * PROJ_COL_TILE, (j + 1) * PROJ_COL_TILE)
        acc = jnp.dot(xn, w_ref[:, cols], preferred_element_type=F32)
        if j % 3 == 2:
            o_ref[:, cols] = acc.astype(BF16)
        else:
            sq = (acc * acc).astype(BF16)
            ss = jnp.concatenate(
                [jnp.dot(sq[:, c:c + MXU_DIM], gmat_ref[...], preferred_element_type=F32)
                 for c in range(0, PROJ_COL_TILE, MXU_DIM)], axis=1)
            o_ref[:, cols] = (acc * lax.rsqrt(ss * (1.0 / HEAD_DIM) + EPS) * gain_ref[:, cols]).astype(BF16)


def _proj_call(x2d, g, w_bf16, gains, gmat):
    n = x2d.shape[0]
    return pl.pallas_call(
        _proj_kernel,
        grid=(n // ROW_TILE,),
        in_specs=[
            pl.BlockSpec((ROW_TILE, D_MODEL), lambda i: (i, 0)),
            _resident((1, D_MODEL)),
            _resident((D_MODEL, PROJ_WIDTH)),
            _resident((1, PROJ_WIDTH)),
            _resident((MXU_DIM, MXU_DIM)),
        ],
        out_specs=pl.BlockSpec((ROW_TILE, PROJ_WIDTH), lambda i: (i, 0)),
        out_shape=jax.ShapeDtypeStruct((n, PROJ_WIDTH), BF16),
        compiler_params=_cparams(("parallel",)),
        name="norm_in_proj",
    )(x2d, g, w_bf16, gains, gmat)


def _deinterleave(x, dil):
    if dil == 1:
        return x
    return jnp.swapaxes(x.reshape(SEQ // dil, dil, LANES), 0, 1).reshape(SEQ, LANES)


def _interleave(x, dil):
    if dil == 1:
        return x
    return jnp.swapaxes(x.reshape(dil, SEQ // dil, LANES), 0, 1).reshape(SEQ, LANES)


def _mixer_a_kernel(q_ref, k_ref, v_ref, bias_ref, o_ref, qp, kp, vp, out_s, lse_s, nat_s):
    n_tiles = SEQ // DIL_BLOCK
    n_branch = len(DIL_PATTERNS)
    lane_full = lax.broadcasted_iota(jnp.int32, (SEQ, LANES), 1)
    lane = lax.broadcasted_iota(jnp.int32, (DIL_BLOCK, LANES), 1)
    head0 = lane < HEAD_DIM

    pad = jnp.zeros((DIL_BLOCK, LANES), BF16)
    for bi, (_, dil) in enumerate(DIL_PATTERNS):
        qd = _deinterleave(q_ref[0].astype(F32), dil)
        qp[bi, 0] = jnp.where(lane_full < HEAD_DIM, qd, 0.0).astype(BF16)
        qp[bi, 1] = jnp.where(lane_full >= HEAD_DIM, qd, 0.0).astype(BF16)
        kp[bi, :DIL_BLOCK, :] = pad
        vp[bi, :DIL_BLOCK, :] = pad
        kp[bi, DIL_BLOCK:, :] = _deinterleave(k_ref[0].astype(F32), dil).astype(BF16)
        vp[bi, DIL_BLOCK:, :] = _deinterleave(v_ref[0].astype(F32), dil).astype(BF16)

    for bi, (_, dil) in enumerate(DIL_PATTERNS):
        nb = SEQ // dil // DIL_BLOCK

        def tile(t, bi=bi, nb=nb):
            row0 = t * DIL_BLOCK if isinstance(t, int) else pl.multiple_of(t * DIL_BLOCK, DIL_BLOCK)
            rows = pl.ds(row0, DIL_BLOCK)
            keys = pl.ds(row0, 2 * DIL_BLOCK)
            tab = 2 * bi + (t % nb == 0)
            kt = kp[bi, keys, :]
            vt = jnp.concatenate([vp[bi, keys, :], jnp.ones((2 * DIL_BLOCK, LANES), BF16)], axis=1)
            m_h, pv_h = [], []
            for hh in range(2):
                s = lax.dot_general(qp[bi, hh, rows, :], kt, (((1,), (1,)), ((), ())),
                                    preferred_element_type=F32) + bias_ref[tab, hh]
                m = jnp.broadcast_to(jnp.max(s, axis=-1, keepdims=True), (DIL_BLOCK, LANES))
                p = jnp.exp2(s - jnp.concatenate([m, m], axis=1))
                m_h.append(m)
                pv_h.append(jnp.dot(p.astype(BF16), vt, preferred_element_type=F32))
            l = jnp.where(head0, pv_h[0][:, LANES:], pv_h[1][:, LANES:])
            out_s[bi, rows, :] = jnp.where(head0, pv_h[0][:, :LANES], pv_h[1][:, :LANES]) / l
            lse_s[bi, rows, :] = jnp.where(head0, m_h[0], m_h[1]) + jnp.log2(l)

        if DIL_GROUP == n_tiles:
            for t in range(n_tiles):
                tile(t)
        else:
            def group(g, carry, tile=tile):
                for u in range(DIL_GROUP):
                    tile(g * DIL_GROUP + u)
                return carry

            lax.fori_loop(0, n_tiles // DIL_GROUP, group, 0)

    for bi, (_, dil) in enumerate(DIL_PATTERNS):
        if dil > 1:
            nat_s[2 * bi - 2] = _interleave(lse_s[bi], dil)
            nat_s[2 * bi - 1] = _interleave(out_s[bi], dil)

    def combine(c, carry):
        rows = pl.ds(pl.multiple_of(c * DIL_BLOCK, DIL_BLOCK), DIL_BLOCK)
        lse_b = [lse_s[0, rows, :]] + [nat_s[2 * bi - 2, rows, :] for bi in range(1, n_branch)]
        out_b = [out_s[0, rows, :]] + [nat_s[2 * bi - 1, rows, :] for bi in range(1, n_branch)]
        lse_max = functools.reduce(jnp.maximum, lse_b)
        w_b = [jnp.exp2(lse - lse_max) for lse in lse_b]
        num = sum(w * o for w, o in zip(w_b, out_b))
        o_ref[0, rows, :] = (num / sum(w_b)).astype(o_ref.dtype)
        return carry

    lax.fori_loop(0, n_tiles, combine, 0, unroll=2)


def _mixer_a_call(proj3d, bias_a):
    b = proj3d.shape[0]
    n_pairs = N_HEADS_A // 2
    n_branch = len(DIL_PATTERNS)
    q_blk0, k_blk0, v_blk0 = 0, WIDTH_A // LANES, 2 * WIDTH_A // LANES
    return pl.pallas_call(
        _mixer_a_kernel,
        grid=(b, n_pairs),
        in_specs=[
            pl.BlockSpec((1, SEQ, LANES), lambda i, p: (i, 0, q_blk0 + p)),
            pl.BlockSpec((1, SEQ, LANES), lambda i, p: (i, 0, k_blk0 + p)),
            pl.BlockSpec((1, SEQ, LANES), lambda i, p: (i, 0, v_blk0 + p)),
            pl.BlockSpec((bias_a.shape[0], 2, DIL_BLOCK, 2 * DIL_BLOCK), lambda i, p: (0, p, 0, 0)),
        ],
        out_specs=pl.BlockSpec((1, SEQ, LANES), lambda i, p: (i, 0, p)),
        out_shape=jax.ShapeDtypeStruct((b, SEQ, WIDTH_A), BF16),
        scratch_shapes=[
            pltpu.VMEM((n_branch, 2, SEQ, LANES), BF16),
            pltpu.VMEM((n_branch, SEQ + DIL_BLOCK, LANES), BF16),
            pltpu.VMEM((n_branch, SEQ + DIL_BLOCK, LANES), BF16),
            pltpu.VMEM((n_branch, SEQ, LANES), F32),
            pltpu.VMEM((n_branch, SEQ, LANES), F32),
            pltpu.VMEM((2 * (n_branch - 1), SEQ, LANES), F32),
        ],
        compiler_params=_cparams(("parallel", "parallel")),
        name="dilated_attention",
    )(proj3d, proj3d, proj3d, bias_a)


def _mixer_b_kernel(far_ref, lam_ref, q_ref, k_ref, v_ref, bias_ref, subg_ref, o_ref, *, lam_init):
    pair = pl.program_id(1)
    t = Q_TILE_B

    def run(n_q):
        lane = lax.broadcasted_iota(jnp.int32, (t, LANES), 1)
        q_maps = []
        for hd in range(2):
            q = q_ref[0, :, hd * LANES:(hd + 1) * LANES]
            zero = jnp.zeros_like(q)
            q_maps.append((jnp.where(lane < HEAD_DIM, q, zero), jnp.where(lane >= HEAD_DIM, q, zero)))
        ones = jnp.ones((t, LANES), BF16)
        state = [None] * 4

        def key_tile(j, kind):
            keys = slice(j * t, (j + 1) * t)
            for hd in range(2):
                kt = k_ref[0, keys, hd * LANES:(hd + 1) * LANES]
                vt = jnp.concatenate([v_ref[0, keys, hd * LANES:(hd + 1) * LANES], ones], axis=1)
                for half in range(2):
                    c = 2 * hd + half
                    s = lax.dot_general(q_maps[hd][half], kt, (((1,), (1,)), ((), ())),
                                        preferred_element_type=F32)
                    if kind == "far":
                        bias = far_ref[2 * pair + hd] * LOG2E
                        m_cur = jnp.broadcast_to(jnp.max(s, axis=-1, keepdims=True) + bias, (t, LANES))
                    else:
                        s = s + bias_ref[1 if kind == "sub" else 0, hd]
                        m_cur = jnp.broadcast_to(jnp.max(s, axis=-1, keepdims=True), (t, LANES))
                    m_new = m_cur if state[c] is None else jnp.maximum(state[c][0], m_cur)
                    shift = m_new - bias if kind == "far" else m_new
                    p = jnp.exp2(s - jnp.concatenate([shift] * (t // LANES), axis=1))
                    pv = jnp.dot(p.astype(BF16), vt, preferred_element_type=F32)
                    if state[c] is not None:
                        alpha = jnp.exp2(state[c][0] - m_new)
                        pv = state[c][1] * jnp.concatenate([alpha, alpha], axis=1) + pv
                    state[c] = (m_new, pv)

        for j in range(n_q - 1):
            key_tile(j, "far")
        if n_q > 0:
            key_tile(n_q - 1, "sub")
        key_tile(n_q, "diag")

        lam_rows = lam_ref[...]
        e1 = jnp.exp(jnp.sum(lam_rows[0:1, :] * lam_rows[1:2, :], axis=-1, keepdims=True))
        e2 = jnp.exp(jnp.sum(lam_rows[2:3, :] * lam_rows[3:4, :], axis=-1, keepdims=True))
        lam = e1 - e2 + lam_init
        for hd in range(2):
            a1, a2 = state[2 * hd][1], state[2 * hd + 1][1]
            o = a1[:, :LANES] / a1[:, LANES:] - lam * (a2[:, :LANES] / a2[:, LANES:])
            ms = jnp.mean(o * o, axis=-1, keepdims=True)
            o_ref[0, :, hd * LANES:(hd + 1) * LANES] = (
                o * lax.rsqrt(ms + EPS) * subg_ref[...] * (1.0 - lam_init)).astype(o_ref.dtype)

    for n_q in range(SEQ // t):
        pl.when(pl.program_id(2) == n_q)(functools.partial(run, n_q))


def _mixer_b_call(proj3d, bias_b, far_b, lam_rows, subg, lam_init):
    b = proj3d.shape[0]
    width = 2 * LANES
    base = 3 * WIDTH_A // width
    q_blk0, k_blk0, v_blk0 = base, base + WIDTH_B // width, base + 2 * WIDTH_B // width
    t = Q_TILE_B
    grid_spec = pltpu.PrefetchScalarGridSpec(
        num_scalar_prefetch=1,
        grid=(b, N_HEADS_B // 2, SEQ // t),
        in_specs=[
            pl.BlockSpec((8, LANES), lambda bi, p, i, far: (0, 0)),
            pl.BlockSpec((1, t, width), lambda bi, p, i, far: (bi, i, q_blk0 + p)),
            pl.BlockSpec((1, SEQ, width), lambda bi, p, i, far: (bi, 0, k_blk0 + p)),
            pl.BlockSpec((1, SEQ, width), lambda bi, p, i, far: (bi, 0, v_blk0 + p)),
            pl.BlockSpec((2, 2, t, t), lambda bi, p, i, far: (0, p, 0, 0)),
            pl.BlockSpec((1, LANES), lambda bi, p, i, far: (0, 0)),
        ],
        out_specs=pl.BlockSpec((1, t, width), lambda bi, p, i, far: (bi, i, p)),
    )
    return pl.pallas_call(
        functools.partial(_mixer_b_kernel, lam_init=lam_init),
        grid_spec=grid_spec,
        out_shape=jax.ShapeDtypeStruct((b, SEQ, WIDTH_B), BF16),
        compiler_params=_cparams(("parallel", "parallel", "arbitrary")),
        name="diff_attention",
    )(far_b, lam_rows, proj3d, proj3d, proj3d, bias_b, subg)


def _mixer_out_residual_norm(oa_ref, ob_ref, w_ref, x_ref, g_ref, rows=slice(None)):
    acc = jnp.dot(oa_ref[rows, :], w_ref[:WIDTH_A, :], preferred_element_type=F32)
    acc = acc + jnp.dot(ob_ref[rows, :], w_ref[WIDTH_A:, :], preferred_element_type=F32)
    x1 = x_ref[rows, :] + acc
    ms = jnp.mean(x1 * x1, axis=-1, keepdims=True)
    return x1, x1 * lax.rsqrt(ms + EPS) * g_ref[...]


def _out_proj_router_kernel(oa_ref, ob_ref, w_ref, x_ref, g_ref, wr_ref, x1_ref, h_ref, route_ref):
    for r0 in range(0, ROW_TILE, ROUTER_ROWS):
        rows = slice(r0, r0 + ROUTER_ROWS)
        x1, hn = _mixer_out_residual_norm(oa_ref, ob_ref, w_ref, x_ref, g_ref, rows)
        x1_ref[rows, :] = x1
        h_ref[rows] = _to_slabs(hn)
        hi = hn.astype(BF16)
        lo = (hn - hi.astype(F32)).astype(BF16)
        lg = (jnp.dot(hi, wr_ref[...], preferred_element_type=F32)
              + jnp.dot(lo, wr_ref[...], preferred_element_type=F32))
        lg = lg + pltpu.roll(lg, LANES - N_EXPERTS, 1)
        lane = lax.broadcasted_iota(jnp.int32, lg.shape, 1)
        lane_f = lane.astype(F32)
        lg = jnp.where(lane < N_EXPERTS, lg, -jnp.inf)
        v1 = jnp.max(lg, axis=-1, keepdims=True)
        i1 = jnp.min(jnp.where(lg == v1, lane_f, float(LANES)), axis=-1, keepdims=True)
        lg2 = jnp.where(lane_f == i1, -jnp.inf, lg)
        v2 = jnp.max(lg2, axis=-1, keepdims=True)
        i2 = jnp.min(jnp.where(lg2 == v2, lane_f, float(LANES)), axis=-1, keepdims=True)
        e = jnp.exp(v2 - v1)
        g1 = 1.0 / (1.0 + e)
        g2 = e / (1.0 + e)
        route_ref[rows, :] = jnp.where(lane == 0, i1, jnp.where(lane == 1, i2,
                                       jnp.where(lane == 2, g1, jnp.where(lane == 3, g2, 0.0))))


def _mixer_out_specs(row):
    return [
        pl.BlockSpec((ROW_TILE, WIDTH_A), row),
        pl.BlockSpec((ROW_TILE, WIDTH_B), row),
        _resident((WIDTH_A + WIDTH_B, D_MODEL)),
        pl.BlockSpec((ROW_TILE, D_MODEL), row),
        _resident((1, D_MODEL)),
    ]


def _out_proj_router_call(oa, ob, w_bf16, x2d, g, w_router_packed):
    n = x2d.shape[0]
    row = lambda i: (i, 0)
    return pl.pallas_call(
        _out_proj_router_kernel,
        grid=(n // ROW_TILE,),
        in_specs=_mixer_out_specs(row) + [_resident((D_MODEL, LANES))],
        out_specs=[
            pl.BlockSpec((ROW_TILE, D_MODEL), row),
            pl.BlockSpec((ROW_TILE, *ROW_SLAB), lambda i: (i, 0, 0)),
            pl.BlockSpec((ROW_TILE, LANES), row),
        ],
        out_shape=[
            jax.ShapeDtypeStruct((n, D_MODEL), F32),
            jax.ShapeDtypeStruct((n, *ROW_SLAB), F32),
            jax.ShapeDtypeStruct((n, LANES), F32),
        ],
        compiler_params=_cparams(("parallel",)),
        name="out_proj_router",
    )(oa, ob, w_bf16, x2d, g, w_router_packed)


def _swiglu_act(g, u):
    return (g / (1.0 + jnp.exp(-g))) * u


def _out_proj_ffn_kernel(oa_ref, ob_ref, w_ref, x_ref, g_ref, wg_ref, wu_ref, wd_ref, o_ref, a_ref):
    x1, hn = _mixer_out_residual_norm(oa_ref, ob_ref, w_ref, x_ref, g_ref)
    h = hn.astype(BF16)
    c0 = 0
    for width in FFN_CHUNKS:
        g = jnp.dot(h, wg_ref[:, c0:c0 + width], preferred_element_type=F32)
        u = jnp.dot(h, wu_ref[:, c0:c0 + width], preferred_element_type=F32)
        a_ref[:, c0:c0 + width] = _swiglu_act(g, u).astype(BF16)
        c0 += width
    o_ref[...] = x1 + jnp.dot(a_ref[...], wd_ref[...], preferred_element_type=F32)


def _out_proj_ffn_call(oa, ob, w_bf16, x2d, g, wg, wu, wd):
    n = x2d.shape[0]
    row = lambda i: (i, 0)
    return pl.pallas_call(
        _out_proj_ffn_kernel,
        grid=(n // ROW_TILE,),
        in_specs=_mixer_out_specs(row) + [
            _resident((D_MODEL, D_FF)),
            _resident((D_MODEL, D_FF)),
            _resident((D_FF, D_MODEL)),
        ],
        out_specs=pl.BlockSpec((ROW_TILE, D_MODEL), row),
        out_shape=jax.ShapeDtypeStruct((n, D_MODEL), F32),
        scratch_shapes=[pltpu.VMEM((ROW_TILE, D_FF), BF16)],
        compiler_params=_cparams(("parallel",)),
        name="out_proj_dense_swiglu",
    )(oa, ob, w_bf16, x2d, g, wg, wu, wd)


ROW_SLAB = (8, LANES)
assert ROW_SLAB[0] * ROW_SLAB[1] == D_MODEL


def _to_slabs(x):
    r = x.shape[0]
    xs = jnp.stack([x[:, LANES * j:LANES * (j + 1)] for j in range(ROW_SLAB[0])], axis=0)
    xs = xs.reshape(ROW_SLAB[0], r // 8, 8, LANES)
    xs = jnp.swapaxes(jnp.swapaxes(xs, 0, 1), 1, 2)
    return xs.reshape(r, *ROW_SLAB)


def _from_slabs(t):
    r = t.shape[0]
    y = jnp.swapaxes(t.reshape(r // 8, 8, *ROW_SLAB), 1, 2)
    return jnp.concatenate([y[:, j].reshape(r, LANES) for j in range(ROW_SLAB[0])], axis=1)


def _row_copy(src_ref, src_row, dst_ref, dst_row, sem):
    return pltpu.make_async_copy(src_ref.at[src_row], dst_ref.at[dst_row], sem)


def _dispatch_kernel(fill_ref, dest_ref, h_ref, o_hbm, stage, zero_ref, sems, zero_sem):
    i = pl.program_id(0)
    last = pl.num_programs(0) - 1
    slot = i % 2

    def drain(s):
        def wait(t, c):
            for k in range(TOP_K):
                _row_copy(stage.at[s], t, o_hbm, 0, sems.at[s]).wait()
            return c
        lax.fori_loop(0, DISPATCH_TILE, wait, 0, unroll=4)

    @pl.when(i >= 2)
    def _():
        drain(slot)

    stage[slot] = h_ref[...]

    def start(t, c):
        for k in range(TOP_K):
            _row_copy(stage.at[slot], t, o_hbm, dest_ref[0, 0, TOP_K * t + k], sems.at[slot]).start(priority=k)
        return c

    lax.fori_loop(0, DISPATCH_TILE, start, 0, unroll=4)

    @pl.when(i == last)
    def _():
        drain(slot)

        @pl.when(last >= 1)
        def _():
            drain(1 - slot)

        zero_ref[...] = jnp.zeros_like(zero_ref)
        for e in range(N_EXPERTS):
            lo, hi = fill_ref[e], fill_ref[N_EXPERTS + e]
            lax.fori_loop(lo, hi, lambda r, c: (_row_copy(zero_ref, 0, o_hbm, r, zero_sem).start(), c)[1], 0)
            lax.fori_loop(lo, hi, lambda r, c: (_row_copy(zero_ref, 0, o_hbm, 0, zero_sem).wait(), c)[1], 0)


def _dispatch_call(fill, dest3d, h, n_rows):
    n = h.shape[0]
    grid_spec = pltpu.PrefetchScalarGridSpec(
        num_scalar_prefetch=1,
        grid=(n // DISPATCH_TILE,),
        in_specs=[
            pl.BlockSpec((1, 1, TOP_K * DISPATCH_TILE), lambda i, f: (i, 0, 0), memory_space=pltpu.SMEM),
            pl.BlockSpec((DISPATCH_TILE, *ROW_SLAB), lambda i, f: (i, 0, 0)),
        ],
        out_specs=pl.BlockSpec(memory_space=pl.ANY),
        scratch_shapes=[pltpu.VMEM((2, DISPATCH_TILE, *ROW_SLAB), h.dtype), pltpu.VMEM((1, *ROW_SLAB), h.dtype),
                        pltpu.SemaphoreType.DMA((2,)), pltpu.SemaphoreType.DMA(())],
    )
    return pl.pallas_call(
        _dispatch_kernel,
        grid_spec=grid_spec,
        out_shape=jax.ShapeDtypeStruct((n_rows, *ROW_SLAB), h.dtype),
        compiler_params=_cparams(("arbitrary",)),
        name="moe_dispatch",
    )(fill, dest3d, h)


def _moe_kernel(be_ref, nact_ref, x_ref, wg_ref, wu_ref, wd_ref, o_ref, a_ref):
    active = pl.program_id(0) < nact_ref[0]

    @pl.when(active)
    def _():
        xb = _from_slabs(x_ref[...]).astype(BF16)
        for c0 in range(0, D_FF_EXPERT, MOE_FF_TILE):
            cols = slice(c0, c0 + MOE_FF_TILE)
            g = jnp.dot(xb, wg_ref[0, :, cols], preferred_element_type=F32)
            u = jnp.dot(xb, wu_ref[0, :, cols], preferred_element_type=F32)
            a_ref[:, cols] = _swiglu_act(g, u).astype(BF16)
        o_ref[...] = _to_slabs(jnp.dot(a_ref[...], wd_ref[0], preferred_element_type=F32))

    @pl.when(jnp.logical_not(active))
    def _():
        o_ref[...] = jnp.zeros_like(o_ref)


def _moe_call(block_expert, n_active, x_pad, wg, wu, wd):
    n_rows = x_pad.shape[0]
    expert = lambda b, be, na: (be[b], 0, 0)
    grid_spec = pltpu.PrefetchScalarGridSpec(
        num_scalar_prefetch=2,
        grid=(n_rows // MOE_BLOCK,),
        in_specs=[
            pl.BlockSpec((MOE_BLOCK, *ROW_SLAB), lambda b, be, na: (b, 0, 0)),
            pl.BlockSpec((1, D_MODEL, D_FF_EXPERT), expert, pipeline_mode=pl.Buffered(1)),
            pl.BlockSpec((1, D_MODEL, D_FF_EXPERT), expert, pipeline_mode=pl.Buffered(1)),
            pl.BlockSpec((1, D_FF_EXPERT, D_MODEL), expert),
        ],
        out_specs=pl.BlockSpec((MOE_BLOCK, *ROW_SLAB), lambda b, be, na: (b, 0, 0)),
        scratch_shapes=[pltpu.VMEM((MOE_BLOCK, D_FF_EXPERT), BF16)],
    )
    return pl.pallas_call(
        _moe_kernel,
        grid_spec=grid_spec,
        out_shape=jax.ShapeDtypeStruct((n_rows, *ROW_SLAB), F32),
        compiler_params=_cparams(("arbitrary",)),
        name="moe_swiglu",
    )(block_expert, n_active, x_pad, wg, wu, wd)


def _combine_kernel(pos_ref, pos_next_ref, y_hbm, x_ref, route_ref, o_ref, buf, sems):
    i = pl.program_id(0)
    slot = i % 2

    def gather(idx_ref, into):
        def start(t, c):
            for k in range(TOP_K):
                _row_copy(y_hbm, idx_ref[0, 0, TOP_K * t + k], buf.at[into, k], t,
                          sems.at[into]).start(priority=k)
            return c
        lax.fori_loop(0, COMBINE_TILE, start, 0, unroll=4)

    @pl.when(i == 0)
    def _():
        gather(pos_ref, 0)

    @pl.when(i + 1 < pl.num_programs(0))
    def _():
        gather(pos_next_ref, 1 - slot)

    def wait(t, c):
        for k in range(TOP_K):
            _row_copy(y_hbm, 0, buf.at[slot, k], t, sems.at[slot]).wait()
        return c

    lax.fori_loop(0, COMBINE_TILE, wait, 0, unroll=4)
    route = route_ref[...]
    g0 = route[:, TOP_K:TOP_K + 1]
    g1 = route[:, TOP_K + 1:TOP_K + 2]
    o_ref[...] = x_ref[...] + g0 * _from_slabs(buf[slot, 0]) + g1 * _from_slabs(buf[slot, 1])


def _combine_call(pos3d, y_pad, x2d, route):
    n = x2d.shape[0]
    n_tiles = n // COMBINE_TILE
    row = lambda i: (i, 0)
    idx_block = (1, 1, TOP_K * COMBINE_TILE)
    return pl.pallas_call(
        _combine_kernel,
        grid=(n_tiles,),
        in_specs=[
            pl.BlockSpec(idx_block, lambda i: (i, 0, 0), memory_space=pltpu.SMEM),
            pl.BlockSpec(idx_block, lambda i: (jnp.minimum(i + 1, n_tiles - 1), 0, 0), memory_space=pltpu.SMEM),
            pl.BlockSpec(memory_space=pl.ANY),
            pl.BlockSpec((COMBINE_TILE, D_MODEL), row),
            pl.BlockSpec((COMBINE_TILE, LANES), row),
        ],
        out_specs=pl.BlockSpec((COMBINE_TILE, D_MODEL), row),
        out_shape=jax.ShapeDtypeStruct((n, D_MODEL), F32),
        scratch_shapes=[pltpu.VMEM((2, TOP_K, COMBINE_TILE, *ROW_SLAB), F32), pltpu.SemaphoreType.DMA((2,))],
        compiler_params=_cparams(("arbitrary",)),
        name="moe_combine",
    )(pos3d, pos3d, y_pad, x2d, route)


def _moe_layer(x1, hn, route, wg, wu, wd):
    n = x1.shape[0]
    e_flat = route[:, :TOP_K].astype(jnp.int32).reshape(-1)
    onehot = (e_flat[:, None] == jnp.arange(N_EXPERTS, dtype=jnp.int32)[None, :]).astype(jnp.int32)
    csum = jnp.cumsum(onehot, axis=0)
    counts = csum[-1]
    padded = (counts + MOE_BLOCK - 1) // MOE_BLOCK * MOE_BLOCK
    pad_ends = jnp.cumsum(padded)
    pad_starts = pad_ends - padded
    dest = jnp.sum((csum - onehot + pad_starts[None, :]) * onehot, axis=1)
    n_blocks = -(-(n * TOP_K) // MOE_BLOCK) + N_EXPERTS
    n_rows = n_blocks * MOE_BLOCK
    block_start = jnp.arange(n_blocks, dtype=jnp.int32) * MOE_BLOCK
    block_expert = jnp.minimum(
        jnp.sum((pad_ends[None, :] <= block_start[:, None]).astype(jnp.int32), axis=1), N_EXPERTS - 1)
    n_active = (pad_ends[-1:] // MOE_BLOCK).astype(jnp.int32)
    fill = jnp.concatenate([pad_starts + counts, pad_starts[1:], jnp.full((1,), n_rows, jnp.int32)])

    x_pad = _dispatch_call(fill.astype(jnp.int32), dest.reshape(n // DISPATCH_TILE, 1, TOP_K * DISPATCH_TILE),
                           hn, n_rows)
    y_pad = _moe_call(block_expert.astype(jnp.int32), n_active, x_pad, wg, wu, wd)
    pos3d = dest.reshape(n // COMBINE_TILE, 1, TOP_K * COMBINE_TILE)
    return _combine_call(pos3d, y_pad, x1, route)


def _head_group_matrix():
    idx = np.arange(MXU_DIM) // HEAD_DIM
    return jnp.asarray((idx[:, None] == idx[None, :]).astype(np.float32), dtype=BF16)


def _pack_router(w_router):
    hi = w_router.astype(BF16)
    lo = (w_router - hi.astype(F32)).astype(BF16)
    pad = jnp.zeros((D_MODEL, LANES - 2 * N_EXPERTS), BF16)
    return jnp.concatenate([hi, lo, pad], axis=1)


def kernel(x, norm_mix_g, norm_ffn_g, w_in, w_out, q_norm_a, k_norm_a, q_norm_b, k_norm_b,
           lambda_q1, lambda_k1, lambda_q2, lambda_k2, subln_g, rel_bias,
           w_gate_dense, w_up_dense, w_down_dense, w_router, w_gate_moe, w_up_moe, w_down_moe):
    b, s, d = x.shape
    assert (s, d) == (SEQ, D_MODEL)
    depth = w_in.shape[0]
    n = b * s
    x2d = x.reshape(n, d)
    gmat = _head_group_matrix()
    bias_a = _bias_tables(rel_bias, _dilated_buckets(), 0, N_HEADS_A)
    bias_b = _bias_tables(rel_bias, _diff_buckets(), N_HEADS_A, N_HEADS_B)
    far_b = rel_bias[_far_bucket(), N_HEADS_A:]
    ones = jnp.ones((PROJ_COL_TILE,), F32)
    rep = PROJ_COL_TILE // HEAD_DIM

    for layer in range(depth):
        gains = jnp.concatenate([
            jnp.tile(q_norm_a[layer], rep) * (ATTN_SCALE * LOG2E), jnp.tile(k_norm_a[layer], rep), ones,
            jnp.tile(q_norm_b[layer], rep) * (ATTN_SCALE * LOG2E), jnp.tile(k_norm_b[layer], rep), ones,
        ]).reshape(1, PROJ_WIDTH)
        proj = _proj_call(x2d, norm_mix_g[layer].reshape(1, d), w_in[layer].astype(BF16), gains, gmat)
        proj3d = proj.reshape(b, s, PROJ_WIDTH)

        oa = _mixer_a_call(proj3d, bias_a)
        lam_init = 0.8 - 0.6 * math.exp(-0.3 * layer)
        lam_rows = jnp.zeros((8, LANES), F32).at[:4, :HEAD_DIM].set(
            jnp.stack([lambda_q1[layer], lambda_k1[layer], lambda_q2[layer], lambda_k2[layer]]))
        ob = _mixer_b_call(proj3d, bias_b, far_b, lam_rows, subln_g[layer].reshape(1, LANES), lam_init)

        is_moe = layer % 2 == 1
        i = layer // 2
        mixed = (oa.reshape(n, WIDTH_A), ob.reshape(n, WIDTH_B), w_out[layer].astype(BF16), x2d,
                 norm_ffn_g[layer].reshape(1, d))
        if is_moe:
            x1, hn, route = _out_proj_router_call(*mixed, _pack_router(w_router[i]))
            x2d = _moe_layer(x1, hn, route, w_gate_moe[i].astype(BF16), w_up_moe[i].astype(BF16),
                             w_down_moe[i].astype(BF16))
        else:
            x2d = _out_proj_ffn_call(*mixed, w_gate_dense[i].astype(BF16), w_up_dense[i].astype(BF16),
                                     w_down_dense[i].astype(BF16))
    return x2d.reshape(b, s, d)
```

```python
import functools
import math

import numpy as np
import jax
import jax.numpy as jnp
from jax import lax
from jax.experimental import pallas as pl
from jax.experimental.pallas import tpu as pltpu

F32 = jnp.float32
BF16 = jnp.bfloat16

D_MODEL = 1024
SEQ = 2048
HEAD_DIM = 64
ATTN_SCALE = HEAD_DIM ** -0.5
LOG2E = math.log2(math.e)
N_HEADS_A = 8
DIL_PATTERNS = ((128, 1), (512, 4), (2048, 16))
DIL_BLOCK = 128
N_HEADS_B = 4
WIDTH_A = N_HEADS_A * HEAD_DIM
WIDTH_B = N_HEADS_B * 2 * HEAD_DIM
PROJ_WIDTH = 3 * (WIDTH_A + WIDTH_B)
N_BUCKETS = 32
MAX_DISTANCE = 128
D_FF = 2816
N_EXPERTS = 8
TOP_K = 2
D_FF_EXPERT = 3584
EPS = 1e-6
NEG = -1e30

LANES = 128
MXU_DIM = 256
VMEM_LIMIT = 56 * 1024 * 1024

ROW_TILE = 512
PROJ_COL_TILE = 512
ROUTER_ROWS = 256
Q_TILE_B = 512
DIL_GROUP = 16
MOE_BLOCK = 512
MOE_FF_TILE = 512
FFN_CHUNKS = (512, 512, 512, 512, 512, 256)
DISPATCH_TILE = 512
COMBINE_TILE = 512


def _cparams(sem):
    return pltpu.CompilerParams(dimension_semantics=sem, vmem_limit_bytes=VMEM_LIMIT)


def _resident(shape):
    nd = len(shape)
    return pl.BlockSpec(shape, lambda *_: (0,) * nd, pipeline_mode=pl.Buffered(1))


def _t5_bucket_np(dist):
    n = np.maximum(dist, 0)
    max_exact = N_BUCKETS // 2
    large = max_exact + (np.log(np.maximum(n, 1).astype(np.float32) / max_exact)
                         / math.log(MAX_DISTANCE / max_exact) * (N_BUCKETS - max_exact)).astype(np.int32)
    large = np.minimum(large, N_BUCKETS - 1)
    return np.where(n < max_exact, n, large).astype(np.int32)


def _dilated_buckets():
    i = np.arange(DIL_BLOCK)[:, None]
    j = np.arange(2 * DIL_BLOCK)[None, :]
    steps = DIL_BLOCK + i - j
    valid = (steps >= 0) & (steps <= DIL_BLOCK)
    tabs = []
    for _, dil in DIL_PATTERNS:
        bucket = np.where(valid, _t5_bucket_np(steps * dil), -1)
        tabs.append(bucket)
        tabs.append(np.where(j >= DIL_BLOCK, bucket, -1))
    return np.stack(tabs).astype(np.int32)


def _diff_buckets():
    t = Q_TILE_B
    qi = np.arange(t)[:, None]
    kj = np.arange(t)[None, :]
    tabs = []
    for off in (0, t):
        dist = qi - kj + off
        tabs.append(np.where(dist >= 0, _t5_bucket_np(dist), -1))
    return np.stack(tabs).astype(np.int32)


def _far_bucket():
    b = _t5_bucket_np(np.arange(Q_TILE_B + 1, SEQ + 1))
    assert (b == b[0]).all()
    return int(b[0])


def _bias_table_kernel(rel_ref, bucket_ref, o_ref, *, head0):
    col = head0 + pl.program_id(1)
    bucket = bucket_ref[0]
    val = jnp.full(bucket.shape, NEG, F32)
    for b in range(N_BUCKETS):
        val = jnp.where(bucket == b, rel_ref[b, col] * LOG2E, val)
    o_ref[0, 0] = val


def _bias_tables(rel_bias, buckets, head0, n_heads):
    n_tab, rows, cols = buckets.shape
    return pl.pallas_call(
        functools.partial(_bias_table_kernel, head0=head0),
        grid=(n_tab, n_heads),
        in_specs=[
            pl.BlockSpec(memory_space=pltpu.SMEM),
            pl.BlockSpec((1, rows, cols), lambda t, h: (t, 0, 0)),
        ],
        out_specs=pl.BlockSpec((1, 1, rows, cols), lambda t, h: (t, h, 0, 0)),
        out_shape=jax.ShapeDtypeStruct((n_tab, n_heads, rows, cols), F32),
        compiler_params=_cparams(("parallel", "parallel")),
        name="bias_tables",
    )(rel_bias, jnp.asarray(buckets))


def _proj_kernel(x_ref, g_ref, w_ref, gain_ref, gmat_ref, o_ref):
    x = x_ref[...]
    ms = jnp.mean(x * x, axis=-1, keepdims=True)
    xn = (x * lax.rsqrt(ms + EPS) * g_ref[...]).astype(BF16)
    for j in range(PROJ_WIDTH // PROJ_COL_TILE):
        cols = slice(j * PROJ_COL_TILE, (j + 1) * PROJ_COL_TILE)
        acc = jnp.dot(xn, w_ref[:, cols], preferred_element_type=F32)
        if j % 3 == 2:
            o_ref[:, cols] = acc.astype(BF16)
        else:
            sq = (acc * acc).astype(BF16)
            ss = jnp.concatenate(
                [jnp.dot(sq[:, c:c + MXU_DIM], gmat_ref[...], preferred_element_type=F32)
                 for c in range(0, PROJ_COL_TILE, MXU_DIM)], axis=1)
            o_ref[:, cols] = (acc * lax.rsqrt(ss * (1.0 / HEAD_DIM) + EPS) * gain_ref[:, cols]).astype(BF16)


def _proj_call(x2d, g, w_bf16, gains, gmat):
    n = x2d.shape[0]
    return pl.pallas_call(
        _proj_kernel,
        grid=(n // ROW_TILE,),
        in_specs=[
            pl.BlockSpec((ROW_TILE, D_MODEL), lambda i: (i, 0)),
            _resident((1, D_MODEL)),
            _resident((D_MODEL, PROJ_WIDTH)),
            _resident((1, PROJ_WIDTH)),
            _resident((MXU_DIM, MXU_DIM)),
        ],
        out_specs=pl.BlockSpec((ROW_TILE, PROJ_WIDTH), lambda i: (i, 0)),
        out_shape=jax.ShapeDtypeStruct((n, PROJ_WIDTH), BF16),
        compiler_params=_cparams(("parallel",)),
        name="norm_in_proj",
    )(x2d, g, w_bf16, gains, gmat)


def _deinterleave(x, dil):
    if dil == 1:
        return x
    return jnp.swapaxes(x.reshape(SEQ // dil, dil, LANES), 0, 1).reshape(SEQ, LANES)


def _interleave(x, dil):
    if dil == 1:
        return x
    return jnp.swapaxes(x.reshape(dil, SEQ // dil, LANES), 0, 1).reshape(SEQ, LANES)


def _mixer_a_kernel(q_ref, k_ref, v_ref, bias_ref, o_ref, qp, kp, vp, out_s, lse_s, nat_s):
    n_tiles = SEQ // DIL_BLOCK
    n_branch = len(DIL_PATTERNS)
    lane_full = lax.broadcasted_iota(jnp.int32, (SEQ, LANES), 1)
    lane = lax.broadcasted_iota(jnp.int32, (DIL_BLOCK, LANES), 1)
    head0 = lane < HEAD_DIM

    pad = jnp.zeros((DIL_BLOCK, LANES), BF16)
    for bi, (_, dil) in enumerate(DIL_PATTERNS):
        qd = _deinterleave(q_ref[0].astype(F32), dil)
        qp[bi, 0] = jnp.where(lane_full < HEAD_DIM, qd, 0.0).astype(BF16)
        qp[bi, 1] = jnp.where(lane_full >= HEAD_DIM, qd, 0.0).astype(BF16)
        kp[bi, :DIL_BLOCK, :] = pad
        vp[bi, :DIL_BLOCK, :] = pad
        kp[bi, DIL_BLOCK:, :] = _deinterleave(k_ref[0].astype(F32), dil).astype(BF16)
        vp[bi, DIL_BLOCK:, :] = _deinterleave(v_ref[0].astype(F32), dil).astype(BF16)

    for bi, (_, dil) in enumerate(DIL_PATTERNS):
        nb = SEQ // dil // DIL_BLOCK

        def tile(t, bi=bi, nb=nb):
            row0 = t * DIL_BLOCK if isinstance(t, int) else pl.multiple_of(t * DIL_BLOCK, DIL_BLOCK)
            rows = pl.ds(row0, DIL_BLOCK)
            keys = pl.ds(row0, 2 * DIL_BLOCK)
            tab = 2 * bi + (t % nb == 0)
            kt = kp[bi, keys, :]
            vt = jnp.concatenate([vp[bi, keys, :], jnp.ones((2 * DIL_BLOCK, LANES), BF16)], axis=1)
            m_h, pv_h = [], []
            for hh in range(2):
                s = lax.dot_general(qp[bi, hh, rows, :], kt, (((1,), (1,)), ((), ())),
                                    preferred_element_type=F32) + bias_ref[tab, hh]
                m = jnp.broadcast_to(jnp.max(s, axis=-1, keepdims=True), (DIL_BLOCK, LANES))
                p = jnp.exp2(s - jnp.concatenate([m, m], axis=1))
                m_h.append(m)
                pv_h.append(jnp.dot(p.astype(BF16), vt, preferred_element_type=F32))
            l = jnp.where(head0, pv_h[0][:, LANES:], pv_h[1][:, LANES:])
            out_s[bi, rows, :] = jnp.where(head0, pv_h[0][:, :LANES], pv_h[1][:, :LANES]) / l
            lse_s[bi, rows, :] = jnp.where(head0, m_h[0], m_h[1]) + jnp.log2(l)

        if DIL_GROUP == n_tiles:
            for t in range(n_tiles):
                tile(t)
        else:
            def group(g, carry, tile=tile):
                for u in range(DIL_GROUP):
                    tile(g * DIL_GROUP + u)
                return carry

            lax.fori_loop(0, n_tiles // DIL_GROUP, group, 0)

    for bi, (_, dil) in enumerate(DIL_PATTERNS):
        if dil > 1:
            nat_s[2 * bi - 2] = _interleave(lse_s[bi], dil)
            nat_s[2 * bi - 1] = _interleave(out_s[bi], dil)

    def combine(c, carry):
        rows = pl.ds(pl.multiple_of(c * DIL_BLOCK, DIL_BLOCK), DIL_BLOCK)
        lse_b = [lse_s[0, rows, :]] + [nat_s[2 * bi - 2, rows, :] for bi in range(1, n_branch)]
        out_b = [out_s[0, rows, :]] + [nat_s[2 * bi - 1, rows, :] for bi in range(1, n_branch)]
        lse_max = functools.reduce(jnp.maximum, lse_b)
        w_b = [jnp.exp2(lse - lse_max) for lse in lse_b]
        num = sum(w * o for w, o in zip(w_b, out_b))
        o_ref[0, rows, :] = (num / sum(w_b)).astype(o_ref.dtype)
        return carry

    lax.fori_loop(0, n_tiles, combine, 0, unroll=2)


def _mixer_a_call(proj3d, bias_a):
    b = proj3d.shape[0]
    n_pairs = N_HEADS_A // 2
    n_branch = len(DIL_PATTERNS)
    q_blk0, k_blk0, v_blk0 = 0, WIDTH_A // LANES, 2 * WIDTH_A // LANES
    return pl.pallas_call(
        _mixer_a_kernel,
        grid=(b, n_pairs),
        in_specs=[
            pl.BlockSpec((1, SEQ, LANES), lambda i, p: (i, 0, q_blk0 + p)),
            pl.BlockSpec((1, SEQ, LANES), lambda i, p: (i, 0, k_blk0 + p)),
            pl.BlockSpec((1, SEQ, LANES), lambda i, p: (i, 0, v_blk0 + p)),
            pl.BlockSpec((bias_a.shape[0], 2, DIL_BLOCK, 2 * DIL_BLOCK), lambda i, p: (0, p, 0, 0)),
        ],
        out_specs=pl.BlockSpec((1, SEQ, LANES), lambda i, p: (i, 0, p)),
        out_shape=jax.ShapeDtypeStruct((b, SEQ, WIDTH_A), BF16),
        scratch_shapes=[
            pltpu.VMEM((n_branch, 2, SEQ, LANES), BF16),
            pltpu.VMEM((n_branch, SEQ + DIL_BLOCK, LANES), BF16),
            pltpu.VMEM((n_branch, SEQ + DIL_BLOCK, LANES), BF16),
            pltpu.VMEM((n_branch, SEQ, LANES), F32),
            pltpu.VMEM((n_branch, SEQ, LANES), F32),
            pltpu.VMEM((2 * (n_branch - 1), SEQ, LANES), F32),
        ],
        compiler_params=_cparams(("parallel", "parallel")),
        name="dilated_attention",
    )(proj3d, proj3d, proj3d, bias_a)


def _mixer_b_kernel(far_ref, lam_ref, q_ref, k_ref, v_ref, bias_ref, subg_ref, o_ref, *, lam_init):
    pair = pl.program_id(1)
    t = Q_TILE_B

    def run(n_q):
        lane = lax.broadcasted_iota(jnp.int32, (t, LANES), 1)
        q_maps = []
        for hd in range(2):
            q = q_ref[0, :, hd * LANES:(hd + 1) * LANES]
            zero = jnp.zeros_like(q)
            q_maps.append((jnp.where(lane < HEAD_DIM, q, zero), jnp.where(lane >= HEAD_DIM, q, zero)))
        ones = jnp.ones((t, LANES), BF16)
        state = [None] * 4

        def key_tile(j, kind):
            keys = slice(j * t, (j + 1) * t)
            for hd in range(2):
                kt = k_ref[0, keys, hd * LANES:(hd + 1) * LANES]
                vt = jnp.concatenate([v_ref[0, keys, hd * LANES:(hd + 1) * LANES], ones], axis=1)
                for half in range(2):
                    c = 2 * hd + half
                    s = lax.dot_general(q_maps[hd][half], kt, (((1,), (1,)), ((), ())),
                                        preferred_element_type=F32)
                    if kind == "far":
                        bias = far_ref[2 * pair + hd] * LOG2E
                        m_cur = jnp.broadcast_to(jnp.max(s, axis=-1, keepdims=True) + bias, (t, LANES))
                    else:
                        s = s + bias_ref[1 if kind == "sub" else 0, hd]
                        m_cur = jnp.broadcast_to(jnp.max(s, axis=-1, keepdims=True), (t, LANES))
                    m_new = m_cur if state[c] is None else jnp.maximum(state[c][0], m_cur)
                    shift = m_new - bias if kind == "far" else m_new
                    p = jnp.exp2(s - jnp.concatenate([shift] * (t // LANES), axis=1))
                    pv = jnp.dot(p.astype(BF16), vt, preferred_element_type=F32)
                    if state[c] is not None:
                        alpha = jnp.exp2(state[c][0] - m_new)
                        pv = state[c][1] * jnp.concatenate([alpha, alpha], axis=1) + pv
                    state[c] = (m_new, pv)

        for j in range(n_q - 1):
            key_tile(j, "far")
        if n_q > 0:
            key_tile(n_q - 1, "sub")
        key_tile(n_q, "diag")

        lam_rows = lam_ref[...]
        e1 = jnp.exp(jnp.sum(lam_rows[0:1, :] * lam_rows[1:2, :], axis=-1, keepdims=True))
        e2 = jnp.exp(jnp.sum(lam_rows[2:3, :] * lam_rows[3:4, :], axis=-1, keepdims=True))
        lam = e1 - e2 + lam_init
        for hd in range(2):
            a1, a2 = state[2 * hd][1], state[2 * hd + 1][1]
            o = a1[:, :LANES] / a1[:, LANES:] - lam * (a2[:, :LANES] / a2[:, LANES:])
            ms = jnp.mean(o * o, axis=-1, keepdims=True)
            o_ref[0, :, hd * LANES:(hd + 1) * LANES] = (
                o * lax.rsqrt(ms + EPS) * subg_ref[...] * (1.0 - lam_init)).astype(o_ref.dtype)

    for n_q in range(SEQ // t):
        pl.when(pl.program_id(2) == n_q)(functools.partial(run, n_q))


def _mixer_b_call(proj3d, bias_b, far_b, lam_rows, subg, lam_init):
    b = proj3d.shape[0]
    width = 2 * LANES
    base = 3 * WIDTH_A // width
    q_blk0, k_blk0, v_blk0 = base, base + WIDTH_B // width, base + 2 * WIDTH_B // width
    t = Q_TILE_B
    grid_spec = pltpu.PrefetchScalarGridSpec(
        num_scalar_prefetch=1,
        grid=(b, N_HEADS_B // 2, SEQ // t),
        in_specs=[
            pl.BlockSpec((8, LANES), lambda bi, p, i, far: (0, 0)),
            pl.BlockSpec((1, t, width), lambda bi, p, i, far: (bi, i, q_blk0 + p)),
            pl.BlockSpec((1, SEQ, width), lambda bi, p, i, far: (bi, 0, k_blk0 + p)),
            pl.BlockSpec((1, SEQ, width), lambda bi, p, i, far: (bi, 0, v_blk0 + p)),
            pl.BlockSpec((2, 2, t, t), lambda bi, p, i, far: (0, p, 0, 0)),
            pl.BlockSpec((1, LANES), lambda bi, p, i, far: (0, 0)),
        ],
        out_specs=pl.BlockSpec((1, t, width), lambda bi, p, i, far: (bi, i, p)),
    )
    return pl.pallas_call(
        functools.partial(_mixer_b_kernel, lam_init=lam_init),
        grid_spec=grid_spec,
        out_shape=jax.ShapeDtypeStruct((b, SEQ, WIDTH_B), BF16),
        compiler_params=_cparams(("parallel", "parallel", "arbitrary")),
        name="diff_attention",
    )(far_b, lam_rows, proj3d, proj3d, proj3d, bias_b, subg)


def _mixer_out_residual_norm(oa_ref, ob_ref, w_ref, x_ref, g_ref, rows=slice(None)):
    acc = jnp.dot(oa_ref[rows, :], w_ref[:WIDTH_A, :], preferred_element_type=F32)
    acc = acc + jnp.dot(ob_ref[rows, :], w_ref[WIDTH_A:, :], preferred_element_type=F32)
    x1 = x_ref[rows, :] + acc
    ms = jnp.mean(x1 * x1, axis=-1, keepdims=True)
    return x1, x1 * lax.rsqrt(ms + EPS) * g_ref[...]


def _out_proj_router_kernel(oa_ref, ob_ref, w_ref, x_ref, g_ref, wr_ref, x1_ref, h_ref, route_ref):
    for r0 in range(0, ROW_TILE, ROUTER_ROWS):
        rows = slice(r0, r0 + ROUTER_ROWS)
        x1, hn = _mixer_out_residual_norm(oa_ref, ob_ref, w_ref, x_ref, g_ref, rows)
        x1_ref[rows, :] = x1
        h_ref[rows] = _to_slabs(hn)
        hi = hn.astype(BF16)
        lo = (hn - hi.astype(F32)).astype(BF16)
        lg = (jnp.dot(hi, wr_ref[...], preferred_element_type=F32)
              + jnp.dot(lo, wr_ref[...], preferred_element_type=F32))
        lg = lg + pltpu.roll(lg, LANES - N_EXPERTS, 1)
        lane = lax.broadcasted_iota(jnp.int32, lg.shape, 1)
        lane_f = lane.astype(F32)
        lg = jnp.where(lane < N_EXPERTS, lg, -jnp.inf)
        v1 = jnp.max(lg, axis=-1, keepdims=True)
        i1 = jnp.min(jnp.where(lg == v1, lane_f, float(LANES)), axis=-1, keepdims=True)
        lg2 = jnp.where(lane_f == i1, -jnp.inf, lg)
        v2 = jnp.max(lg2, axis=-1, keepdims=True)
        i2 = jnp.min(jnp.where(lg2 == v2, lane_f, float(LANES)), axis=-1, keepdims=True)
        e = jnp.exp(v2 - v1)
        g1 = 1.0 / (1.0 + e)
        g2 = e / (1.0 + e)
        route_ref[rows, :] = jnp.where(lane == 0, i1, jnp.where(lane == 1, i2,
                                       jnp.where(lane == 2, g1, jnp.where(lane == 3, g2, 0.0))))


def _mixer_out_specs(row):
    return [
        pl.BlockSpec((ROW_TILE, WIDTH_A), row),
        pl.BlockSpec((ROW_TILE, WIDTH_B), row),
        _resident((WIDTH_A + WIDTH_B, D_MODEL)),
        pl.BlockSpec((ROW_TILE, D_MODEL), row),
        _resident((1, D_MODEL)),
    ]


def _out_proj_router_call(oa, ob, w_bf16, x2d, g, w_router_packed):
    n = x2d.shape[0]
    row = lambda i: (i, 0)
    return pl.pallas_call(
        _out_proj_router_kernel,
        grid=(n // ROW_TILE,),
        in_specs=_mixer_out_specs(row) + [_resident((D_MODEL, LANES))],
        out_specs=[
            pl.BlockSpec((ROW_TILE, D_MODEL), row),
            pl.BlockSpec((ROW_TILE, *ROW_SLAB), lambda i: (i, 0, 0)),
            pl.BlockSpec((ROW_TILE, LANES), row),
        ],
        out_shape=[
            jax.ShapeDtypeStruct((n, D_MODEL), F32),
            jax.ShapeDtypeStruct((n, *ROW_SLAB), F32),
            jax.ShapeDtypeStruct((n, LANES), F32),
        ],
        compiler_params=_cparams(("parallel",)),
        name="out_proj_router",
    )(oa, ob, w_bf16, x2d, g, w_router_packed)


def _swiglu_act(g, u):
    return (g / (1.0 + jnp.exp(-g))) * u


def _out_proj_ffn_kernel(oa_ref, ob_ref, w_ref, x_ref, g_ref, wg_ref, wu_ref, wd_ref, o_ref, a_ref):
    x1, hn = _mixer_out_residual_norm(oa_ref, ob_ref, w_ref, x_ref, g_ref)
    h = hn.astype(BF16)
    c0 = 0
    for width in FFN_CHUNKS:
        g = jnp.dot(h, wg_ref[:, c0:c0 + width], preferred_element_type=F32)
        u = jnp.dot(h, wu_ref[:, c0:c0 + width], preferred_element_type=F32)
        a_ref[:, c0:c0 + width] = _swiglu_act(g, u).astype(BF16)
        c0 += width
    o_ref[...] = x1 + jnp.dot(a_ref[...], wd_ref[...], preferred_element_type=F32)


def _out_proj_ffn_call(oa, ob, w_bf16, x2d, g, wg, wu, wd):
    n = x2d.shape[0]
    row = lambda i: (i, 0)
    return pl.pallas_call(
        _out_proj_ffn_kernel,
        grid=(n // ROW_TILE,),
        in_specs=_mixer_out_specs(row) + [
            _resident((D_MODEL, D_FF)),
            _resident((D_MODEL, D_FF)),
            _resident((D_FF, D_MODEL)),
        ],
        out_specs=pl.BlockSpec((ROW_TILE, D_MODEL), row),
        out_shape=jax.ShapeDtypeStruct((n, D_MODEL), F32),
        scratch_shapes=[pltpu.VMEM((ROW_TILE, D_FF), BF16)],
        compiler_params=_cparams(("parallel",)),
        name="out_proj_dense_swiglu",
    )(oa, ob, w_bf16, x2d, g, wg, wu, wd)


ROW_SLAB = (8, LANES)
assert ROW_SLAB[0] * ROW_SLAB[1] == D_MODEL


def _to_slabs(x):
    r = x.shape[0]
    xs = jnp.stack([x[:, LANES * j:LANES * (j + 1)] for j in range(ROW_SLAB[0])], axis=0)
    xs = xs.reshape(ROW_SLAB[0], r // 8, 8, LANES)
    xs = jnp.swapaxes(jnp.swapaxes(xs, 0, 1), 1, 2)
    return xs.reshape(r, *ROW_SLAB)


def _from_slabs(t):
    r = t.shape[0]
    y = jnp.swapaxes(t.reshape(r // 8, 8, *ROW_SLAB), 1, 2)
    return jnp.concatenate([y[:, j].reshape(r, LANES) for j in range(ROW_SLAB[0])], axis=1)


def _row_copy(src_ref, src_row, dst_ref, dst_row, sem):
    return pltpu.make_async_copy(src_ref.at[src_row], dst_ref.at[dst_row], sem)


def _dispatch_kernel(fill_ref, dest_ref, h_ref, o_hbm, stage, zero_ref, sems, zero_sem):
    i = pl.program_id(0)
    last = pl.num_programs(0) - 1
    slot = i % 2

    def drain(s):
        def wait(t, c):
            for k in range(TOP_K):
                _row_copy(stage.at[s], t, o_hbm, 0, sems.at[s]).wait()
            return c
        lax.fori_loop(0, DISPATCH_TILE, wait, 0, unroll=4)

    @pl.when(i >= 2)
    def _():
        drain(slot)

    stage[slot] = h_ref[...]

    def start(t, c):
        for k in range(TOP_K):
            _row_copy(stage.at[slot], t, o_hbm, dest_ref[0, 0, TOP_K * t + k], sems.at[slot]).start(priority=k)
        return c

    lax.fori_loop(0, DISPATCH_TILE, start, 0, unroll=4)

    @pl.when(i == last)
    def _():
        drain(slot)

        @pl.when(last >= 1)
        def _():
            drain(1 - slot)

        zero_ref[...] = jnp.zeros_like(zero_ref)
        for e in range(N_EXPERTS):
            lo, hi = fill_ref[e], fill_ref[N_EXPERTS + e]
            lax.fori_loop(lo, hi, lambda r, c: (_row_copy(zero_ref, 0, o_hbm, r, zero_sem).start(), c)[1], 0)
            lax.fori_loop(lo, hi, lambda r, c: (_row_copy(zero_ref, 0, o_hbm, 0, zero_sem).wait(), c)[1], 0)


def _dispatch_call(fill, dest3d, h, n_rows):
    n = h.shape[0]
    grid_spec = pltpu.PrefetchScalarGridSpec(
        num_scalar_prefetch=1,
        grid=(n // DISPATCH_TILE,),
        in_specs=[
            pl.BlockSpec((1, 1, TOP_K * DISPATCH_TILE), lambda i, f: (i, 0, 0), memory_space=pltpu.SMEM),
            pl.BlockSpec((DISPATCH_TILE, *ROW_SLAB), lambda i, f: (i, 0, 0)),
        ],
        out_specs=pl.BlockSpec(memory_space=pl.ANY),
        scratch_shapes=[pltpu.VMEM((2, DISPATCH_TILE, *ROW_SLAB), h.dtype), pltpu.VMEM((1, *ROW_SLAB), h.dtype),
                        pltpu.SemaphoreType.DMA((2,)), pltpu.SemaphoreType.DMA(())],
    )
    return pl.pallas_call(
        _dispatch_kernel,
        grid_spec=grid_spec,
        out_shape=jax.ShapeDtypeStruct((n_rows, *ROW_SLAB), h.dtype),
        compiler_params=_cparams(("arbitrary",)),
        name="moe_dispatch",
    )(fill, dest3d, h)


def _moe_kernel(be_ref, nact_ref, x_ref, wg_ref, wu_ref, wd_ref, o_ref, a_ref):
    active = pl.program_id(0) < nact_ref[0]

    @pl.when(active)
    def _():
        xb = _from_slabs(x_ref[...]).astype(BF16)
        for c0 in range(0, D_FF_EXPERT, MOE_FF_TILE):
            cols = slice(c0, c0 + MOE_FF_TILE)
            g = jnp.dot(xb, wg_ref[0, :, cols], preferred_element_type=F32)
            u = jnp.dot(xb, wu_ref[0, :, cols], preferred_element_type=F32)
            a_ref[:, cols] = _swiglu_act(g, u).astype(BF16)
        o_ref[...] = _to_slabs(jnp.dot(a_ref[...], wd_ref[0], preferred_element_type=F32))

    @pl.when(jnp.logical_not(active))
    def _():
        o_ref[...] = jnp.zeros_like(o_ref)


def _moe_call(block_expert, n_active, x_pad, wg, wu, wd):
    n_rows = x_pad.shape[0]
    expert = lambda b, be, na: (be[b], 0, 0)
    grid_spec = pltpu.PrefetchScalarGridSpec(
        num_scalar_prefetch=2,
        grid=(n_rows // MOE_BLOCK,),
        in_specs=[
            pl.BlockSpec((MOE_BLOCK, *ROW_SLAB), lambda b, be, na: (b, 0, 0)),
            pl.BlockSpec((1, D_MODEL, D_FF_EXPERT), expert, pipeline_mode=pl.Buffered(1)),
            pl.BlockSpec((1, D_MODEL, D_FF_EXPERT), expert, pipeline_mode=pl.Buffered(1)),
            pl.BlockSpec((1, D_FF_EXPERT, D_MODEL), expert),
        ],
        out_specs=pl.BlockSpec((MOE_BLOCK, *ROW_SLAB), lambda b, be, na: (b, 0, 0)),
        scratch_shapes=[pltpu.VMEM((MOE_BLOCK, D_FF_EXPERT), BF16)],
    )
    return pl.pallas_call(
        _moe_kernel,
        grid_spec=grid_spec,
        out_shape=jax.ShapeDtypeStruct((n_rows, *ROW_SLAB), F32),
        compiler_params=_cparams(("arbitrary",)),
        name="moe_swiglu",
    )(block_expert, n_active, x_pad, wg, wu, wd)


def _combine_kernel(pos_ref, pos_next_ref, y_hbm, x_ref, route_ref, o_ref, buf, sems):
    i = pl.program_id(0)
    slot = i % 2

    def gather(idx_ref, into):
        def start(t, c):
            for k in range(TOP_K):
                _row_copy(y_hbm, idx_ref[0, 0, TOP_K * t + k], buf.at[into, k], t,
                          sems.at[into]).start(priority=k)
            return c
        lax.fori_loop(0, COMBINE_TILE, start, 0, unroll=4)

    @pl.when(i == 0)
    def _():
        gather(pos_ref, 0)

    @pl.when(i + 1 < pl.num_programs(0))
    def _():
        gather(pos_next_ref, 1 - slot)

    def wait(t, c):
        for k in range(TOP_K):
            _row_copy(y_hbm, 0, buf.at[slot, k], t, sems.at[slot]).wait()
        return c

    lax.fori_loop(0, COMBINE_TILE, wait, 0, unroll=4)
    route = route_ref[...]
    g0 = route[:, TOP_K:TOP_K + 1]
    g1 = route[:, TOP_K + 1:TOP_K + 2]
    o_ref[...] = x_ref[...] + g0 * _from_slabs(buf[slot, 0]) + g1 * _from_slabs(buf[slot, 1])


def _combine_call(pos3d, y_pad, x2d, route):
    n = x2d.shape[0]
    n_tiles = n // COMBINE_TILE
    row = lambda i: (i, 0)
    idx_block = (1, 1, TOP_K * COMBINE_TILE)
    return pl.pallas_call(
        _combine_kernel,
        grid=(n_tiles,),
        in_specs=[
            pl.BlockSpec(idx_block, lambda i: (i, 0, 0), memory_space=pltpu.SMEM),
            pl.BlockSpec(idx_block, lambda i: (jnp.minimum(i + 1, n_tiles - 1), 0, 0), memory_space=pltpu.SMEM),
            pl.BlockSpec(memory_space=pl.ANY),
            pl.BlockSpec((COMBINE_TILE, D_MODEL), row),
            pl.BlockSpec((COMBINE_TILE, LANES), row),
        ],
        out_specs=pl.BlockSpec((COMBINE_TILE, D_MODEL), row),
        out_shape=jax.ShapeDtypeStruct((n, D_MODEL), F32),
        scratch_shapes=[pltpu.VMEM((2, TOP_K, COMBINE_TILE, *ROW_SLAB), F32), pltpu.SemaphoreType.DMA((2,))],
        compiler_params=_cparams(("arbitrary",)),
        name="moe_combine",
    )(pos3d, pos3d, y_pad, x2d, route)


def _moe_layer(x1, hn, route, wg, wu, wd):
    n = x1.shape[0]
    e_flat = route[:, :TOP_K].astype(jnp.int32).reshape(-1)
    onehot = (e_flat[:, None] == jnp.arange(N_EXPERTS, dtype=jnp.int32)[None, :]).astype(jnp.int32)
    csum = jnp.cumsum(onehot, axis=0)
    counts = csum[-1]
    padded = (counts + MOE_BLOCK - 1) // MOE_BLOCK * MOE_BLOCK
    pad_ends = jnp.cumsum(padded)
    pad_starts = pad_ends - padded
    dest = jnp.sum((csum - onehot + pad_starts[None, :]) * onehot, axis=1)
    n_blocks = -(-(n * TOP_K) // MOE_BLOCK) + N_EXPERTS
    n_rows = n_blocks * MOE_BLOCK
    block_start = jnp.arange(n_blocks, dtype=jnp.int32) * MOE_BLOCK
    block_expert = jnp.minimum(
        jnp.sum((pad_ends[None, :] <= block_start[:, None]).astype(jnp.int32), axis=1), N_EXPERTS - 1)
    n_active = (pad_ends[-1:] // MOE_BLOCK).astype(jnp.int32)
    fill = jnp.concatenate([pad_starts + counts, pad_starts[1:], jnp.full((1,), n_rows, jnp.int32)])

    x_pad = _dispatch_call(fill.astype(jnp.int32), dest.reshape(n // DISPATCH_TILE, 1, TOP_K * DISPATCH_TILE),
                           hn, n_rows)
    y_pad = _moe_call(block_expert.astype(jnp.int32), n_active, x_pad, wg, wu, wd)
    pos3d = dest.reshape(n // COMBINE_TILE, 1, TOP_K * COMBINE_TILE)
    return _combine_call(pos3d, y_pad, x1, route)


def _head_group_matrix():
    idx = np.arange(MXU_DIM) // HEAD_DIM
    return jnp.asarray((idx[:, None] == idx[None, :]).astype(np.float32), dtype=BF16)


def _pack_router(w_router):
    hi = w_router.astype(BF16)
    lo = (w_router - hi.astype(F32)).astype(BF16)
    pad = jnp.zeros((D_MODEL, LANES - 2 * N_EXPERTS), BF16)
    return jnp.concatenate([hi, lo, pad], axis=1)


def kernel(x, norm_mix_g, norm_ffn_g, w_in, w_out, q_norm_a, k_norm_a, q_norm_b, k_norm_b,
           lambda_q1, lambda_k1, lambda_q2, lambda_k2, subln_g, rel_bias,
           w_gate_dense, w_up_dense, w_down_dense, w_router, w_gate_moe, w_up_moe, w_down_moe):
    b, s, d = x.shape
    assert (s, d) == (SEQ, D_MODEL)
    depth = w_in.shape[0]
    n = b * s
    x2d = x.reshape(n, d)
    gmat = _head_group_matrix()
    bias_a = _bias_tables(rel_bias, _dilated_buckets(), 0, N_HEADS_A)
    bias_b = _bias_tables(rel_bias, _diff_buckets(), N_HEADS_A, N_HEADS_B)
    far_b = rel_bias[_far_bucket(), N_HEADS_A:]
    ones = jnp.ones((PROJ_COL_TILE,), F32)
    rep = PROJ_COL_TILE // HEAD_DIM

    for layer in range(depth):
        gains = jnp.concatenate([
            jnp.tile(q_norm_a[layer], rep) * (ATTN_SCALE * LOG2E), jnp.tile(k_norm_a[layer], rep), ones,
            jnp.tile(q_norm_b[layer], rep) * (ATTN_SCALE * LOG2E), jnp.tile(k_norm_b[layer], rep), ones,
        ]).reshape(1, PROJ_WIDTH)
        proj = _proj_call(x2d, norm_mix_g[layer].reshape(1, d), w_in[layer].astype(BF16), gains, gmat)
        proj3d = proj.reshape(b, s, PROJ_WIDTH)

        oa = _mixer_a_call(proj3d, bias_a)
        lam_init = 0.8 - 0.6 * math.exp(-0.3 * layer)
        lam_rows = jnp.zeros((8, LANES), F32).at[:4, :HEAD_DIM].set(
            jnp.stack([lambda_q1[layer], lambda_k1[layer], lambda_q2[layer], lambda_k2[layer]]))
        ob = _mixer_b_call(proj3d, bias_b, far_b, lam_rows, subln_g[layer].reshape(1, LANES), lam_init)

        is_moe = layer % 2 == 1
        i = layer // 2
        mixed = (oa.reshape(n, WIDTH_A), ob.reshape(n, WIDTH_B), w_out[layer].astype(BF16), x2d,
                 norm_ffn_g[layer].reshape(1, d))
        if is_moe:
            x1, hn, route = _out_proj_router_call(*mixed, _pack_router(w_router[i]))
            x2d = _moe_layer(x1, hn, route, w_gate_moe[i].astype(BF16), w_up_moe[i].astype(BF16),
                             w_down_moe[i].astype(BF16))
        else:
            x2d = _out_proj_ffn_call(*mixed, w_gate_dense[i].astype(BF16), w_up_dense[i].astype(BF16),
                                     w_down_dense[i].astype(BF16))
    return x2d.reshape(b, s, d)
```

```python
import functools
import math

import numpy as np
import jax
import jax.numpy as jnp
from jax import lax
from jax.experimental import pallas as pl
from jax.experimental.pallas import tpu as pltpu

F32 = jnp.float32
BF16 = jnp.bfloat16

D_MODEL = 1024
SEQ = 2048
HEAD_DIM = 64
ATTN_SCALE = HEAD_DIM ** -0.5
LOG2E = math.log2(math.e)
N_HEADS_A = 8
DIL_PATTERNS = ((128, 1), (512, 4), (2048, 16))
DIL_BLOCK = 128
N_HEADS_B = 4
WIDTH_A = N_HEADS_A * HEAD_DIM
WIDTH_B = N_HEADS_B * 2 * HEAD_DIM
PROJ_WIDTH = 3 * (WIDTH_A + WIDTH_B)
N_BUCKETS = 32
MAX_DISTANCE = 128
D_FF = 2816
N_EXPERTS = 8
TOP_K = 2
D_FF_EXPERT = 3584
EPS = 1e-6
NEG = -1e30

LANES = 128
MXU_DIM = 256
VMEM_LIMIT = 56 * 1024 * 1024

ROW_TILE = 512
PROJ_COL_TILE = 512
ROUTER_ROWS = 256
Q_TILE_B = 512
DIL_GROUP = 16
MOE_BLOCK = 512
MOE_FF_TILE = 512
FFN_CHUNKS = (512, 512, 512, 512, 512, 256)
DISPATCH_TILE = 512
COMBINE_TILE = 512


def _cparams(sem):
    return pltpu.CompilerParams(dimension_semantics=sem, vmem_limit_bytes=VMEM_LIMIT)


def _resident(shape):
    nd = len(shape)
    return pl.BlockSpec(shape, lambda *_: (0,) * nd, pipeline_mode=pl.Buffered(1))


def _t5_bucket_np(dist):
    n = np.maximum(dist, 0)
    max_exact = N_BUCKETS // 2
    large = max_exact + (np.log(np.maximum(n, 1).astype(np.float32) / max_exact)
                         / math.log(MAX_DISTANCE / max_exact) * (N_BUCKETS - max_exact)).astype(np.int32)
    large = np.minimum(large, N_BUCKETS - 1)
    return np.where(n < max_exact, n, large).astype(np.int32)


def _dilated_buckets():
    i = np.arange(DIL_BLOCK)[:, None]
    j = np.arange(2 * DIL_BLOCK)[None, :]
    steps = DIL_BLOCK + i - j
    valid = (steps >= 0) & (steps <= DIL_BLOCK)
    tabs = []
    for _, dil in DIL_PATTERNS:
        bucket = np.where(valid, _t5_bucket_np(steps * dil), -1)
        tabs.append(bucket)
        tabs.append(np.where(j >= DIL_BLOCK, bucket, -1))
    return np.stack(tabs).astype(np.int32)


def _diff_buckets():
    t = Q_TILE_B
    qi = np.arange(t)[:, None]
    kj = np.arange(t)[None, :]
    tabs = []
    for off in (0, t):
        dist = qi - kj + off
        tabs.append(np.where(dist >= 0, _t5_bucket_np(dist), -1))
    return np.stack(tabs).astype(np.int32)


def _far_bucket():
    b = _t5_bucket_np(np.arange(Q_TILE_B + 1, SEQ + 1))
    assert (b == b[0]).all()
    return int(b[0])


def _bias_table_kernel(rel_ref, bucket_ref, o_ref, *, head0):
    col = head0 + pl.program_id(1)
    bucket = bucket_ref[0]
    val = jnp.full(bucket.shape, NEG, F32)
    for b in range(N_BUCKETS):
        val = jnp.where(bucket == b, rel_ref[b, col] * LOG2E, val)
    o_ref[0, 0] = val


def _bias_tables(rel_bias, buckets, head0, n_heads):
    n_tab, rows, cols = buckets.shape
    return pl.pallas_call(
        functools.partial(_bias_table_kernel, head0=head0),
        grid=(n_tab, n_heads),
        in_specs=[
            pl.BlockSpec(memory_space=pltpu.SMEM),
            pl.BlockSpec((1, rows, cols), lambda t, h: (t, 0, 0)),
        ],
        out_specs=pl.BlockSpec((1, 1, rows, cols), lambda t, h: (t, h, 0, 0)),
        out_shape=jax.ShapeDtypeStruct((n_tab, n_heads, rows, cols), F32),
        compiler_params=_cparams(("parallel", "parallel")),
        name="bias_tables",
    )(rel_bias, jnp.asarray(buckets))


def _proj_kernel(x_ref, g_ref, w_ref, gain_ref, gmat_ref, o_ref):
    x = x_ref[...]
    ms = jnp.mean(x * x, axis=-1, keepdims=True)
    xn = (x * lax.rsqrt(ms + EPS) * g_ref[...]).astype(BF16)
    for j in range(PROJ_WIDTH // PROJ_COL_TILE):
        cols = slice(j * PROJ_COL_TILE, (j + 1) * PROJ_COL_TILE)
        acc = jnp.dot(xn, w_ref[:, cols], preferred_element_type=F32)
        if j % 3 == 2:
            o_ref[:, cols] = acc.astype(BF16)
        else:
            sq = (acc * acc).astype(BF16)
            ss = jnp.concatenate(
                [jnp.dot(sq[:, c:c + MXU_DIM], gmat_ref[...], preferred_element_type=F32)
                 for c in range(0, PROJ_COL_TILE, MXU_DIM)], axis=1)
            o_ref[:, cols] = (acc * lax.rsqrt(ss * (1.0 / HEAD_DIM) + EPS) * gain_ref[:, cols]).astype(BF16)


def _proj_call(x2d, g, w_bf16, gains, gmat):
    n = x2d.shape[0]
    return pl.pallas_call(
        _proj_kernel,
        grid=(n // ROW_TILE,),
        in_specs=[
            pl.BlockSpec((ROW_TILE, D_MODEL), lambda i: (i, 0)),
            _resident((1, D_MODEL)),
            _resident((D_MODEL, PROJ_WIDTH)),
            _resident((1, PROJ_WIDTH)),
            _resident((MXU_DIM, MXU_DIM)),
        ],
        out_specs=pl.BlockSpec((ROW_TILE, PROJ_WIDTH), lambda i: (i, 0)),
        out_shape=jax.ShapeDtypeStruct((n, PROJ_WIDTH), BF16),
        compiler_params=_cparams(("parallel",)),
        name="norm_in_proj",
    )(x2d, g, w_bf16, gains, gmat)


def _deinterleave(x, dil):
    if dil == 1:
        return x
    return jnp.swapaxes(x.reshape(SEQ // dil, dil, LANES), 0, 1).reshape(SEQ, LANES)


def _interleave(x, dil):
    if dil == 1:
        return x
    return jnp.swapaxes(x.reshape(dil, SEQ // dil, LANES), 0, 1).reshape(SEQ, LANES)


def _mixer_a_kernel(q_ref, k_ref, v_ref, bias_ref, o_ref, qp, kp, vp, out_s, lse_s, nat_s):
    n_tiles = SEQ // DIL_BLOCK
    n_branch = len(DIL_PATTERNS)
    lane_full = lax.broadcasted_iota(jnp.int32, (SEQ, LANES), 1)
    lane = lax.broadcasted_iota(jnp.int32, (DIL_BLOCK, LANES), 1)
    head0 = lane < HEAD_DIM

    pad = jnp.zeros((DIL_BLOCK, LANES), BF16)
    for bi, (_, dil) in enumerate(DIL_PATTERNS):
        qd = _deinterleave(q_ref[0].astype(F32), dil)
        qp[bi, 0] = jnp.where(lane_full < HEAD_DIM, qd, 0.0).astype(BF16)
        qp[bi, 1] = jnp.where(lane_full >= HEAD_DIM, qd, 0.0).astype(BF16)
        kp[bi, :DIL_BLOCK, :] = pad
        vp[bi, :DIL_BLOCK, :] = pad
        kp[bi, DIL_BLOCK:, :] = _deinterleave(k_ref[0].astype(F32), dil).astype(BF16)
        vp[bi, DIL_BLOCK:, :] = _deinterleave(v_ref[0].astype(F32), dil).astype(BF16)

    for bi, (_, dil) in enumerate(DIL_PATTERNS):
        nb = SEQ // dil // DIL_BLOCK

        def tile(t, bi=bi, nb=nb):
            row0 = t * DIL_BLOCK if isinstance(t, int) else pl.multiple_of(t * DIL_BLOCK, DIL_BLOCK)
            rows = pl.ds(row0, DIL_BLOCK)
            keys = pl.ds(row0, 2 * DIL_BLOCK)
            tab = 2 * bi + (t % nb == 0)
            kt = kp[bi, keys, :]
            vt = jnp.concatenate([vp[bi, keys, :], jnp.ones((2 * DIL_BLOCK, LANES), BF16)], axis=1)
            m_h, pv_h = [], []
            for hh in range(2):
                s = lax.dot_general(qp[bi, hh, rows, :], kt, (((1,), (1,)), ((), ())),
                                    preferred_element_type=F32) + bias_ref[tab, hh]
                m = jnp.broadcast_to(jnp.max(s, axis=-1, keepdims=True), (DIL_BLOCK, LANES))
                p = jnp.exp2(s - jnp.concatenate([m, m], axis=1))
                m_h.append(m)
                pv_h.append(jnp.dot(p.astype(BF16), vt, preferred_element_type=F32))
            l = jnp.where(head0, pv_h[0][:, LANES:], pv_h[1][:, LANES:])
            out_s[bi, rows, :] = jnp.where(head0, pv_h[0][:, :LANES], pv_h[1][:, :LANES]) / l
            lse_s[bi, rows, :] = jnp.where(head0, m_h[0], m_h[1]) + jnp.log2(l)

        if DIL_GROUP == n_tiles:
            for t in range(n_tiles):
                tile(t)
        else:
            def group(g, carry, tile=tile):
                for u in range(DIL_GROUP):
                    tile(g * DIL_GROUP + u)
                return carry

            lax.fori_loop(0, n_tiles // DIL_GROUP, group, 0)

    for bi, (_, dil) in enumerate(DIL_PATTERNS):
        if dil > 1:
            nat_s[2 * bi - 2] = _interleave(lse_s[bi], dil)
            nat_s[2 * bi - 1] = _interleave(out_s[bi], dil)

    def combine(c, carry):
        rows = pl.ds(pl.multiple_of(c * DIL_BLOCK, DIL_BLOCK), DIL_BLOCK)
        lse_b = [lse_s[0, rows, :]] + [nat_s[2 * bi - 2, rows, :] for bi in range(1, n_branch)]
        out_b = [out_s[0, rows, :]] + [nat_s[2 * bi - 1, rows, :] for bi in range(1, n_branch)]
        lse_max = functools.reduce(jnp.maximum, lse_b)
        w_b = [jnp.exp2(lse - lse_max) for lse in lse_b]
        num = sum(w * o for w, o in zip(w_b, out_b))
        o_ref[0, rows, :] = (num / sum(w_b)).astype(o_ref.dtype)
        return carry

    lax.fori_loop(0, n_tiles, combine, 0, unroll=2)


def _mixer_a_call(proj3d, bias_a):
    b = proj3d.shape[0]
    n_pairs = N_HEADS_A // 2
    n_branch = len(DIL_PATTERNS)
    q_blk0, k_blk0, v_blk0 = 0, WIDTH_A // LANES, 2 * WIDTH_A // LANES
    return pl.pallas_call(
        _mixer_a_kernel,
        grid=(b, n_pairs),
        in_specs=[
            pl.BlockSpec((1, SEQ, LANES), lambda i, p: (i, 0, q_blk0 + p)),
            pl.BlockSpec((1, SEQ, LANES), lambda i, p: (i, 0, k_blk0 + p)),
            pl.BlockSpec((1, SEQ, LANES), lambda i, p: (i, 0, v_blk0 + p)),
            pl.BlockSpec((bias_a.shape[0], 2, DIL_BLOCK, 2 * DIL_BLOCK), lambda i, p: (0, p, 0, 0)),
        ],
        out_specs=pl.BlockSpec((1, SEQ, LANES), lambda i, p: (i, 0, p)),
        out_shape=jax.ShapeDtypeStruct((b, SEQ, WIDTH_A), BF16),
        scratch_shapes=[
            pltpu.VMEM((n_branch, 2, SEQ, LANES), BF16),
            pltpu.VMEM((n_branch, SEQ + DIL_BLOCK, LANES), BF16),
            pltpu.VMEM((n_branch, SEQ + DIL_BLOCK, LANES), BF16),
            pltpu.VMEM((n_branch, SEQ, LANES), F32),
            pltpu.VMEM((n_branch, SEQ, LANES), F32),
            pltpu.VMEM((2 * (n_branch - 1), SEQ, LANES), F32),
        ],
        compiler_params=_cparams(("parallel", "parallel")),
        name="dilated_attention",
    )(proj3d, proj3d, proj3d, bias_a)


def _mixer_b_kernel(far_ref, lam_ref, q_ref, k_ref, v_ref, bias_ref, subg_ref, o_ref, *, lam_init):
    pair = pl.program_id(1)
    t = Q_TILE_B

    def run(n_q):
        lane = lax.broadcasted_iota(jnp.int32, (t, LANES), 1)
        q_maps = []
        for hd in range(2):
            q = q_ref[0, :, hd * LANES:(hd + 1) * LANES]
            zero = jnp.zeros_like(q)
            q_maps.append((jnp.where(lane < HEAD_DIM, q, zero), jnp.where(lane >= HEAD_DIM, q, zero)))
        ones = jnp.ones((t, LANES), BF16)
        state = [None] * 4

        def key_tile(j, kind):
            keys = slice(j * t, (j + 1) * t)
            for hd in range(2):
                kt = k_ref[0, keys, hd * LANES:(hd + 1) * LANES]
                vt = jnp.concatenate([v_ref[0, keys, hd * LANES:(hd + 1) * LANES], ones], axis=1)
                for half in range(2):
                    c = 2 * hd + half
                    s = lax.dot_general(q_maps[hd][half], kt, (((1,), (1,)), ((), ())),
                                        preferred_element_type=F32)
                    if kind == "far":
                        bias = far_ref[2 * pair + hd] * LOG2E
                        m_cur = jnp.broadcast_to(jnp.max(s, axis=-1, keepdims=True) + bias, (t, LANES))
                    else:
                        s = s + bias_ref[1 if kind == "sub" else 0, hd]
                        m_cur = jnp.broadcast_to(jnp.max(s, axis=-1, keepdims=True), (t, LANES))
                    m_new = m_cur if state[c] is None else jnp.maximum(state[c][0], m_cur)
                    shift = m_new - bias if kind == "far" else m_new
                    p = jnp.exp2(s - jnp.concatenate([shift] * (t // LANES), axis=1))
                    pv = jnp.dot(p.astype(BF16), vt, preferred_element_type=F32)
                    if state[c] is not None:
                        alpha = jnp.exp2(state[c][0] - m_new)
                        pv = state[c][1] * jnp.concatenate([alpha, alpha], axis=1) + pv
                    state[c] = (m_new, pv)

        for j in range(n_q - 1):
            key_tile(j, "far")
        if n_q > 0:
            key_tile(n_q - 1, "sub")
        key_tile(n_q, "diag")

        lam_rows = lam_ref[...]
        e1 = jnp.exp(jnp.sum(lam_rows[0:1, :] * lam_rows[1:2, :], axis=-1, keepdims=True))
        e2 = jnp.exp(jnp.sum(lam_rows[2:3, :] * lam_rows[3:4, :], axis=-1, keepdims=True))
        lam = e1 - e2 + lam_init
        for hd in range(2):
            a1, a2 = state[2 * hd][1], state[2 * hd + 1][1]
            o = a1[:, :LANES] / a1[:, LANES:] - lam * (a2[:, :LANES] / a2[:, LANES:])
            ms = jnp.mean(o * o, axis=-1, keepdims=True)
            o_ref[0, :, hd * LANES:(hd + 1) * LANES] = (
                o * lax.rsqrt(ms + EPS) * subg_ref[...] * (1.0 - lam_init)).astype(o_ref.dtype)

    for n_q in range(SEQ // t):
        pl.when(pl.program_id(2) == n_q)(functools.partial(run, n_q))


def _mixer_b_call(proj3d, bias_b, far_b, lam_rows, subg, lam_init):
    b = proj3d.shape[0]
    width = 2 * LANES
    base = 3 * WIDTH_A // width
    q_blk0, k_blk0, v_blk0 = base, base + WIDTH_B // width, base + 2 * WIDTH_B // width
    t = Q_TILE_B
    grid_spec = pltpu.PrefetchScalarGridSpec(
        num_scalar_prefetch=1,
        grid=(b, N_HEADS_B // 2, SEQ // t),
        in_specs=[
            pl.BlockSpec((8, LANES), lambda bi, p, i, far: (0, 0)),
            pl.BlockSpec((1, t, width), lambda bi, p, i, far: (bi, i, q_blk0 + p)),
            pl.BlockSpec((1, SEQ, width), lambda bi, p, i, far: (bi, 0, k_blk0 + p)),
            pl.BlockSpec((1, SEQ, width), lambda bi, p, i, far: (bi, 0, v_blk0 + p)),
            pl.BlockSpec((2, 2, t, t), lambda bi, p, i, far: (0, p, 0, 0)),
            pl.BlockSpec((1, LANES), lambda bi, p, i, far: (0, 0)),
        ],
        out_specs=pl.BlockSpec((1, t, width), lambda bi, p, i, far: (bi, i, p)),
    )
    return pl.pallas_call(
        functools.partial(_mixer_b_kernel, lam_init=lam_init),
        grid_spec=grid_spec,
        out_shape=jax.ShapeDtypeStruct((b, SEQ, WIDTH_B), BF16),
        compiler_params=_cparams(("parallel", "parallel", "arbitrary")),
        name="diff_attention",
    )(far_b, lam_rows, proj3d, proj3d, proj3d, bias_b, subg)


def _mixer_out_residual_norm(oa_ref, ob_ref, w_ref, x_ref, g_ref, rows=slice(None)):
    acc = jnp.dot(oa_ref[rows, :], w_ref[:WIDTH_A, :], preferred_element_type=F32)
    acc = acc + jnp.dot(ob_ref[rows, :], w_ref[WIDTH_A:, :], preferred_element_type=F32)
    x1 = x_ref[rows, :] + acc
    ms = jnp.mean(x1 * x1, axis=-1, keepdims=True)
    return x1, x1 * lax.rsqrt(ms + EPS) * g_ref[...]


def _out_proj_router_kernel(oa_ref, ob_ref, w_ref, x_ref, g_ref, wr_ref, x1_ref, h_ref, route_ref):
    for r0 in range(0, ROW_TILE, ROUTER_ROWS):
        rows = slice(r0, r0 + ROUTER_ROWS)
        x1, hn = _mixer_out_residual_norm(oa_ref, ob_ref, w_ref, x_ref, g_ref, rows)
        x1_ref[rows, :] = x1
        h_ref[rows] = _to_slabs(hn)
        hi = hn.astype(BF16)
        lo = (hn - hi.astype(F32)).astype(BF16)
        lg = (jnp.dot(hi, wr_ref[...], preferred_element_type=F32)
              + jnp.dot(lo, wr_ref[...], preferred_element_type=F32))
        lg = lg + pltpu.roll(lg, LANES - N_EXPERTS, 1)
        lane = lax.broadcasted_iota(jnp.int32, lg.shape, 1)
        lane_f = lane.astype(F32)
        lg = jnp.where(lane < N_EXPERTS, lg, -jnp.inf)
        v1 = jnp.max(lg, axis=-1, keepdims=True)
        i1 = jnp.min(jnp.where(lg == v1, lane_f, float(LANES)), axis=-1, keepdims=True)
        lg2 = jnp.where(lane_f == i1, -jnp.inf, lg)
        v2 = jnp.max(lg2, axis=-1, keepdims=True)
        i2 = jnp.min(jnp.where(lg2 == v2, lane_f, float(LANES)), axis=-1, keepdims=True)
        e = jnp.exp(v2 - v1)
        g1 = 1.0 / (1.0 + e)
        g2 = e / (1.0 + e)
        route_ref[rows, :] = jnp.where(lane == 0, i1, jnp.where(lane == 1, i2,
                                       jnp.where(lane == 2, g1, jnp.where(lane == 3, g2, 0.0))))


def _mixer_out_specs(row):
    return [
        pl.BlockSpec((ROW_TILE, WIDTH_A), row),
        pl.BlockSpec((ROW_TILE, WIDTH_B), row),
        _resident((WIDTH_A + WIDTH_B, D_MODEL)),
        pl.BlockSpec((ROW_TILE, D_MODEL), row),
        _resident((1, D_MODEL)),
    ]


def _out_proj_router_call(oa, ob, w_bf16, x2d, g, w_router_packed):
    n = x2d.shape[0]
    row = lambda i: (i, 0)
    return pl.pallas_call(
        _out_proj_router_kernel,
        grid=(n // ROW_TILE,),
        in_specs=_mixer_out_specs(row) + [_resident((D_MODEL, LANES))],
        out_specs=[
            pl.BlockSpec((ROW_TILE, D_MODEL), row),
            pl.BlockSpec((ROW_TILE, *ROW_SLAB), lambda i: (i, 0, 0)),
            pl.BlockSpec((ROW_TILE, LANES), row),
        ],
        out_shape=[
            jax.ShapeDtypeStruct((n, D_MODEL), F32),
            jax.ShapeDtypeStruct((n, *ROW_SLAB), F32),
            jax.ShapeDtypeStruct((n, LANES), F32),
        ],
        compiler_params=_cparams(("parallel",)),
        name="out_proj_router",
    )(oa, ob, w_bf16, x2d, g, w_router_packed)


def _swiglu_act(g, u):
    return (g / (1.0 + jnp.exp(-g))) * u


def _out_proj_ffn_kernel(oa_ref, ob_ref, w_ref, x_ref, g_ref, wg_ref, wu_ref, wd_ref, *rest, n_cast):
    cast_in, o_ref, cast_out, a_ref = rest[:n_cast], rest[n_cast], rest[n_cast + 1:-1], rest[-1]
    x1, hn = _mixer_out_residual_norm(oa_ref, ob_ref, w_ref, x_ref, g_ref)
    h = hn.astype(BF16)
    c0 = 0
    for width in FFN_CHUNKS:
        g = jnp.dot(h, wg_ref[:, c0:c0 + width], preferred_element_type=F32)
        u = jnp.dot(h, wu_ref[:, c0:c0 + width], preferred_element_type=F32)
        a_ref[:, c0:c0 + width] = _swiglu_act(g, u).astype(BF16)
        c0 += width
    o_ref[...] = x1 + jnp.dot(a_ref[...], wd_ref[...], preferred_element_type=F32)
    for src, dst in zip(cast_in, cast_out):
        dst[...] = src[...].astype(dst.dtype)


def _out_proj_ffn_call(oa, ob, w_bf16, x2d, g, wg, wu, wd, cast_along=()):
    n = x2d.shape[0]
    steps = n // ROW_TILE
    row = lambda i: (i, 0)
    cast_specs = [pl.BlockSpec((c.shape[0] // steps, c.shape[1]), row) for c in cast_along]
    assert all(c.shape[0] % (16 * steps) == 0 for c in cast_along)
    return pl.pallas_call(
        functools.partial(_out_proj_ffn_kernel, n_cast=len(cast_along)),
        grid=(steps,),
        in_specs=_mixer_out_specs(row) + [
            _resident((D_MODEL, D_FF)),
            _resident((D_MODEL, D_FF)),
            _resident((D_FF, D_MODEL)),
        ] + cast_specs,
        out_specs=[pl.BlockSpec((ROW_TILE, D_MODEL), row)] + cast_specs,
        out_shape=[jax.ShapeDtypeStruct((n, D_MODEL), F32)]
        + [jax.ShapeDtypeStruct(c.shape, BF16) for c in cast_along],
        scratch_shapes=[pltpu.VMEM((ROW_TILE, D_FF), BF16)],
        compiler_params=_cparams(("parallel",)),
        name="out_proj_dense_swiglu",
    )(oa, ob, w_bf16, x2d, g, wg, wu, wd, *cast_along)


ROW_SLAB = (8, LANES)
assert ROW_SLAB[0] * ROW_SLAB[1] == D_MODEL


def _to_slabs(x):
    r = x.shape[0]
    xs = jnp.stack([x[:, LANES * j:LANES * (j + 1)] for j in range(ROW_SLAB[0])], axis=0)
    xs = xs.reshape(ROW_SLAB[0], r // 8, 8, LANES)
    xs = jnp.swapaxes(jnp.swapaxes(xs, 0, 1), 1, 2)
    return xs.reshape(r, *ROW_SLAB)


def _from_slabs(t):
    r = t.shape[0]
    y = jnp.swapaxes(t.reshape(r // 8, 8, *ROW_SLAB), 1, 2)
    return jnp.concatenate([y[:, j].reshape(r, LANES) for j in range(ROW_SLAB[0])], axis=1)


def _row_copy(src_ref, src_row, dst_ref, dst_row, sem):
    return pltpu.make_async_copy(src_ref.at[src_row], dst_ref.at[dst_row], sem)


def _dispatch_kernel(fill_ref, dest_ref, h_ref, o_hbm, stage, zero_ref, sems, zero_sem):
    i = pl.program_id(0)
    last = pl.num_programs(0) - 1
    slot = i % 2

    def drain(s):
        def wait(t, c):
            for k in range(TOP_K):
                _row_copy(stage.at[s], t, o_hbm, 0, sems.at[s]).wait()
            return c
        lax.fori_loop(0, DISPATCH_TILE, wait, 0, unroll=4)

    @pl.when(i >= 2)
    def _():
        drain(slot)

    stage[slot] = h_ref[...]

    def start(t, c):
        for k in range(TOP_K):
            _row_copy(stage.at[slot], t, o_hbm, dest_ref[0, 0, TOP_K * t + k], sems.at[slot]).start(priority=k)
        return c

    lax.fori_loop(0, DISPATCH_TILE, start, 0, unroll=4)

    @pl.when(i == last)
    def _():
        drain(slot)

        @pl.when(last >= 1)
        def _():
            drain(1 - slot)

        zero_ref[...] = jnp.zeros_like(zero_ref)
        for e in range(N_EXPERTS):
            lo, hi = fill_ref[e], fill_ref[N_EXPERTS + e]
            lax.fori_loop(lo, hi, lambda r, c: (_row_copy(zero_ref, 0, o_hbm, r, zero_sem).start(), c)[1], 0)
            lax.fori_loop(lo, hi, lambda r, c: (_row_copy(zero_ref, 0, o_hbm, 0, zero_sem).wait(), c)[1], 0)


def _dispatch_call(fill, dest3d, h, n_rows):
    n = h.shape[0]
    grid_spec = pltpu.PrefetchScalarGridSpec(
        num_scalar_prefetch=1,
        grid=(n // DISPATCH_TILE,),
        in_specs=[
            pl.BlockSpec((1, 1, TOP_K * DISPATCH_TILE), lambda i, f: (i, 0, 0), memory_space=pltpu.SMEM),
            pl.BlockSpec((DISPATCH_TILE, *ROW_SLAB), lambda i, f: (i, 0, 0)),
        ],
        out_specs=pl.BlockSpec(memory_space=pl.ANY),
        scratch_shapes=[pltpu.VMEM((2, DISPATCH_TILE, *ROW_SLAB), h.dtype), pltpu.VMEM((1, *ROW_SLAB), h.dtype),
                        pltpu.SemaphoreType.DMA((2,)), pltpu.SemaphoreType.DMA(())],
    )
    return pl.pallas_call(
        _dispatch_kernel,
        grid_spec=grid_spec,
        out_shape=jax.ShapeDtypeStruct((n_rows, *ROW_SLAB), h.dtype),
        compiler_params=_cparams(("arbitrary",)),
        name="moe_dispatch",
    )(fill, dest3d, h)


def _moe_kernel(be_ref, nact_ref, x_ref, wg_ref, wu_ref, wd_ref, o_ref, a_ref):
    active = pl.program_id(0) < nact_ref[0]

    @pl.when(active)
    def _():
        xb = _from_slabs(x_ref[...]).astype(BF16)
        for c0 in range(0, D_FF_EXPERT, MOE_FF_TILE):
            cols = slice(c0, c0 + MOE_FF_TILE)
            g = jnp.dot(xb, wg_ref[0, :, cols], preferred_element_type=F32)
            u = jnp.dot(xb, wu_ref[0, :, cols], preferred_element_type=F32)
            a_ref[:, cols] = _swiglu_act(g, u).astype(BF16)
        o_ref[...] = _to_slabs(jnp.dot(a_ref[...], wd_ref[0], preferred_element_type=F32))

    @pl.when(jnp.logical_not(active))
    def _():
        o_ref[...] = jnp.zeros_like(o_ref)


def _moe_call(block_expert, n_active, x_pad, wg, wu, wd):
    n_rows = x_pad.shape[0]
    expert = lambda b, be, na: (be[b], 0, 0)
    grid_spec = pltpu.PrefetchScalarGridSpec(
        num_scalar_prefetch=2,
        grid=(n_rows // MOE_BLOCK,),
        in_specs=[
            pl.BlockSpec((MOE_BLOCK, *ROW_SLAB), lambda b, be, na: (b, 0, 0)),
            pl.BlockSpec((1, D_MODEL, D_FF_EXPERT), expert, pipeline_mode=pl.Buffered(1)),
            pl.BlockSpec((1, D_MODEL, D_FF_EXPERT), expert, pipeline_mode=pl.Buffered(1)),
            pl.BlockSpec((1, D_FF_EXPERT, D_MODEL), expert),
        ],
        out_specs=pl.BlockSpec((MOE_BLOCK, *ROW_SLAB), lambda b, be, na: (b, 0, 0)),
        scratch_shapes=[pltpu.VMEM((MOE_BLOCK, D_FF_EXPERT), BF16)],
    )
    return pl.pallas_call(
        _moe_kernel,
        grid_spec=grid_spec,
        out_shape=jax.ShapeDtypeStruct((n_rows, *ROW_SLAB), F32),
        compiler_params=_cparams(("arbitrary",)),
        name="moe_swiglu",
    )(block_expert, n_active, x_pad, wg, wu, wd)


def _combine_kernel(pos_ref, pos_next_ref, y_hbm, x_ref, route_ref, o_ref, buf, sems):
    i = pl.program_id(0)
    slot = i % 2

    def gather(idx_ref, into):
        def start(t, c):
            for k in range(TOP_K):
                _row_copy(y_hbm, idx_ref[0, 0, TOP_K * t + k], buf.at[into, k], t,
                          sems.at[into]).start(priority=k)
            return c
        lax.fori_loop(0, COMBINE_TILE, start, 0, unroll=4)

    @pl.when(i == 0)
    def _():
        gather(pos_ref, 0)

    @pl.when(i + 1 < pl.num_programs(0))
    def _():
        gather(pos_next_ref, 1 - slot)

    def wait(t, c):
        for k in range(TOP_K):
            _row_copy(y_hbm, 0, buf.at[slot, k], t, sems.at[slot]).wait()
        return c

    lax.fori_loop(0, COMBINE_TILE, wait, 0, unroll=4)
    route = route_ref[...]
    g0 = route[:, TOP_K:TOP_K + 1]
    g1 = route[:, TOP_K + 1:TOP_K + 2]
    o_ref[...] = x_ref[...] + g0 * _from_slabs(buf[slot, 0]) + g1 * _from_slabs(buf[slot, 1])


def _combine_call(pos3d, y_pad, x2d, route):
    n = x2d.shape[0]
    n_tiles = n // COMBINE_TILE
    row = lambda i: (i, 0)
    idx_block = (1, 1, TOP_K * COMBINE_TILE)
    return pl.pallas_call(
        _combine_kernel,
        grid=(n_tiles,),
        in_specs=[
            pl.BlockSpec(idx_block, lambda i: (i, 0, 0), memory_space=pltpu.SMEM),
            pl.BlockSpec(idx_block, lambda i: (jnp.minimum(i + 1, n_tiles - 1), 0, 0), memory_space=pltpu.SMEM),
            pl.BlockSpec(memory_space=pl.ANY),
            pl.BlockSpec((COMBINE_TILE, D_MODEL), row),
            pl.BlockSpec((COMBINE_TILE, LANES), row),
        ],
        out_specs=pl.BlockSpec((COMBINE_TILE, D_MODEL), row),
        out_shape=jax.ShapeDtypeStruct((n, D_MODEL), F32),
        scratch_shapes=[pltpu.VMEM((2, TOP_K, COMBINE_TILE, *ROW_SLAB), F32), pltpu.SemaphoreType.DMA((2,))],
        compiler_params=_cparams(("arbitrary",)),
        name="moe_combine",
    )(pos3d, pos3d, y_pad, x2d, route)


def _moe_layer(x1, hn, route, wg, wu, wd):
    n = x1.shape[0]
    e_flat = route[:, :TOP_K].astype(jnp.int32).reshape(-1)
    onehot = (e_flat[:, None] == jnp.arange(N_EXPERTS, dtype=jnp.int32)[None, :]).astype(jnp.int32)
    csum = jnp.cumsum(onehot, axis=0)
    counts = csum[-1]
    padded = (counts + MOE_BLOCK - 1) // MOE_BLOCK * MOE_BLOCK
    pad_ends = jnp.cumsum(padded)
    pad_starts = pad_ends - padded
    dest = jnp.sum((csum - onehot + pad_starts[None, :]) * onehot, axis=1)
    n_blocks = -(-(n * TOP_K) // MOE_BLOCK) + N_EXPERTS
    n_rows = n_blocks * MOE_BLOCK
    block_start = jnp.arange(n_blocks, dtype=jnp.int32) * MOE_BLOCK
    block_expert = jnp.minimum(
        jnp.sum((pad_ends[None, :] <= block_start[:, None]).astype(jnp.int32), axis=1), N_EXPERTS - 1)
    n_active = (pad_ends[-1:] // MOE_BLOCK).astype(jnp.int32)
    fill = jnp.concatenate([pad_starts + counts, pad_starts[1:], jnp.full((1,), n_rows, jnp.int32)])

    x_pad = _dispatch_call(fill.astype(jnp.int32), dest.reshape(n // DISPATCH_TILE, 1, TOP_K * DISPATCH_TILE),
                           hn, n_rows)
    y_pad = _moe_call(block_expert.astype(jnp.int32), n_active, x_pad, wg, wu, wd)
    pos3d = dest.reshape(n // COMBINE_TILE, 1, TOP_K * COMBINE_TILE)
    return _combine_call(pos3d, y_pad, x1, route)


def _head_group_matrix():
    idx = np.arange(MXU_DIM) // HEAD_DIM
    return jnp.asarray((idx[:, None] == idx[None, :]).astype(np.float32), dtype=BF16)


def _pack_router(w_router):
    hi = w_router.astype(BF16)
    lo = (w_router - hi.astype(F32)).astype(BF16)
    pad = jnp.zeros((D_MODEL, LANES - 2 * N_EXPERTS), BF16)
    return jnp.concatenate([hi, lo, pad], axis=1)


def kernel(x, norm_mix_g, norm_ffn_g, w_in, w_out, q_norm_a, k_norm_a, q_norm_b, k_norm_b,
           lambda_q1, lambda_k1, lambda_q2, lambda_k2, subln_g, rel_bias,
           w_gate_dense, w_up_dense, w_down_dense, w_router, w_gate_moe, w_up_moe, w_down_moe):
    b, s, d = x.shape
    assert (s, d) == (SEQ, D_MODEL)
    depth = w_in.shape[0]
    n = b * s
    x2d = x.reshape(n, d)
    gmat = _head_group_matrix()
    bias_a = _bias_tables(rel_bias, _dilated_buckets(), 0, N_HEADS_A)
    bias_b = _bias_tables(rel_bias, _diff_buckets(), N_HEADS_A, N_HEADS_B)
    far_b = rel_bias[_far_bucket(), N_HEADS_A:]
    ones = jnp.ones((PROJ_COL_TILE,), F32)
    rep = PROJ_COL_TILE // HEAD_DIM

    moe_bf16 = {}
    for layer in range(depth):
        gains = jnp.concatenate([
            jnp.tile(q_norm_a[layer], rep) * (ATTN_SCALE * LOG2E), jnp.tile(k_norm_a[layer], rep), ones,
            jnp.tile(q_norm_b[layer], rep) * (ATTN_SCALE * LOG2E), jnp.tile(k_norm_b[layer], rep), ones,
        ]).reshape(1, PROJ_WIDTH)
        proj = _proj_call(x2d, norm_mix_g[layer].reshape(1, d), w_in[layer].astype(BF16), gains, gmat)
        proj3d = proj.reshape(b, s, PROJ_WIDTH)

        oa = _mixer_a_call(proj3d, bias_a)
        lam_init = 0.8 - 0.6 * math.exp(-0.3 * layer)
        lam_rows = jnp.zeros((8, LANES), F32).at[:4, :HEAD_DIM].set(
            jnp.stack([lambda_q1[layer], lambda_k1[layer], lambda_q2[layer], lambda_k2[layer]]))
        ob = _mixer_b_call(proj3d, bias_b, far_b, lam_rows, subln_g[layer].reshape(1, LANES), lam_init)

        is_moe = layer % 2 == 1
        i = layer // 2
        mixed = (oa.reshape(n, WIDTH_A), ob.reshape(n, WIDTH_B), w_out[layer].astype(BF16), x2d,
                 norm_ffn_g[layer].reshape(1, d))
        if is_moe:
            x1, hn, route = _out_proj_router_call(*mixed, _pack_router(w_router[i]))
            if i not in moe_bf16:
                moe_bf16[i] = tuple(w[i].astype(BF16) for w in (w_gate_moe, w_up_moe, w_down_moe))
            x2d = _moe_layer(x1, hn, route, *moe_bf16[i])
        else:
            nxt = [w[i] for w in (w_gate_moe, w_up_moe, w_down_moe)] if layer + 1 < depth else []
            x2d, *cast = _out_proj_ffn_call(
                *mixed, w_gate_dense[i].astype(BF16), w_up_dense[i].astype(BF16),
                w_down_dense[i].astype(BF16), tuple(w.reshape(-1, w.shape[-1]) for w in nxt))
            if nxt:
                moe_bf16[i] = tuple(c.reshape(w.shape) for c, w in zip(cast, nxt))
    return x2d.reshape(b, s, d)
```

```python
import functools
import math

import numpy as np
import jax
import jax.numpy as jnp
from jax import lax
from jax.experimental import pallas as pl
from jax.experimental.pallas import tpu as pltpu

F32 = jnp.float32
BF16 = jnp.bfloat16

D_MODEL = 1024
SEQ = 2048
HEAD_DIM = 64
ATTN_SCALE = HEAD_DIM ** -0.5
LOG2E = math.log2(math.e)
N_HEADS_A = 8
DIL_PATTERNS = ((128, 1), (512, 4), (2048, 16))
DIL_BLOCK = 128
N_HEADS_B = 4
WIDTH_A = N_HEADS_A * HEAD_DIM
WIDTH_B = N_HEADS_B * 2 * HEAD_DIM
PROJ_WIDTH = 3 * (WIDTH_A + WIDTH_B)
N_BUCKETS = 32
MAX_DISTANCE = 128
D_FF = 2816
N_EXPERTS = 8
TOP_K = 2
D_FF_EXPERT = 3584
EPS = 1e-6
NEG = -1e30

LANES = 128
MXU_DIM = 256
VMEM_LIMIT = 56 * 1024 * 1024

ROW_TILE = 512
PROJ_ROW_TILE = 1024
PROJ_COL_TILE = 512
ROUTER_ROWS = 256
Q_TILE_B = 512
DIL_GROUP = 16
MOE_BLOCK = 512
MOE_FF_TILE = 512
FFN_CHUNKS = (512, 512, 512, 512, 512, 256)
DISPATCH_TILE = 512
COMBINE_TILE = 512


def _cparams(sem):
    return pltpu.CompilerParams(dimension_semantics=sem, vmem_limit_bytes=VMEM_LIMIT)


def _resident(shape):
    nd = len(shape)
    return pl.BlockSpec(shape, lambda *_: (0,) * nd, pipeline_mode=pl.Buffered(1))


def _t5_bucket_np(dist):
    n = np.maximum(dist, 0)
    max_exact = N_BUCKETS // 2
    large = max_exact + (np.log(np.maximum(n, 1).astype(np.float32) / max_exact)
                         / math.log(MAX_DISTANCE / max_exact) * (N_BUCKETS - max_exact)).astype(np.int32)
    large = np.minimum(large, N_BUCKETS - 1)
    return np.where(n < max_exact, n, large).astype(np.int32)


def _dilated_buckets():
    i = np.arange(DIL_BLOCK)[:, None]
    j = np.arange(2 * DIL_BLOCK)[None, :]
    steps = DIL_BLOCK + i - j
    valid = (steps >= 0) & (steps <= DIL_BLOCK)
    tabs = []
    for _, dil in DIL_PATTERNS:
        bucket = np.where(valid, _t5_bucket_np(steps * dil), -1)
        tabs.append(bucket)
        tabs.append(np.where(j >= DIL_BLOCK, bucket, -1))
    return np.stack(tabs).astype(np.int32)


def _diff_buckets():
    t = Q_TILE_B
    qi = np.arange(t)[:, None]
    kj = np.arange(t)[None, :]
    tabs = []
    for off in (0, t):
        dist = qi - kj + off
        tabs.append(np.where(dist >= 0, _t5_bucket_np(dist), -1))
    return np.stack(tabs).astype(np.int32)


def _far_bucket():
    b = _t5_bucket_np(np.arange(Q_TILE_B + 1, SEQ + 1))
    assert (b == b[0]).all()
    return int(b[0])


def _bias_table_kernel(rel_ref, bucket_ref, o_ref, *, head0):
    col = head0 + pl.program_id(1)
    bucket = bucket_ref[0]
    val = jnp.full(bucket.shape, NEG, F32)
    for b in range(N_BUCKETS):
        val = jnp.where(bucket == b, rel_ref[b, col] * LOG2E, val)
    o_ref[0, 0] = val


def _bias_tables(rel_bias, buckets, head0, n_heads):
    n_tab, rows, cols = buckets.shape
    return pl.pallas_call(
        functools.partial(_bias_table_kernel, head0=head0),
        grid=(n_tab, n_heads),
        in_specs=[
            pl.BlockSpec(memory_space=pltpu.SMEM),
            pl.BlockSpec((1, rows, cols), lambda t, h: (t, 0, 0)),
        ],
        out_specs=pl.BlockSpec((1, 1, rows, cols), lambda t, h: (t, h, 0, 0)),
        out_shape=jax.ShapeDtypeStruct((n_tab, n_heads, rows, cols), F32),
        compiler_params=_cparams(("parallel", "parallel")),
        name="bias_tables",
    )(rel_bias, jnp.asarray(buckets))


def _proj_kernel(x_ref, g_ref, w_ref, gain_ref, gmat_ref, o_ref):
    x = x_ref[...]
    ms = jnp.mean(x * x, axis=-1, keepdims=True)
    xn = (x * lax.rsqrt(ms + EPS) * g_ref[...]).astype(BF16)
    for j in range(PROJ_WIDTH // PROJ_COL_TILE):
        cols = slice(j * PROJ_COL_TILE, (j + 1) * PROJ_COL_TILE)
        acc = jnp.dot(xn, w_ref[:, cols], preferred_element_type=F32)
        if j % 3 == 2:
            o_ref[:, cols] = acc.astype(BF16)
        else:
            sq = (acc * acc).astype(BF16)
            ss = jnp.concatenate(
                [jnp.dot(sq[:, c:c + MXU_DIM], gmat_ref[...], preferred_element_type=F32)
                 for c in range(0, PROJ_COL_TILE, MXU_DIM)], axis=1)
            o_ref[:, cols] = (acc * lax.rsqrt(ss * (1.0 / HEAD_DIM) + EPS) * gain_ref[:, cols]).astype(BF16)


def _proj_call(x2d, g, w_bf16, gains, gmat):
    n = x2d.shape[0]
    return pl.pallas_call(
        _proj_kernel,
        grid=(n // PROJ_ROW_TILE,),
        in_specs=[
            pl.BlockSpec((PROJ_ROW_TILE, D_MODEL), lambda i: (i, 0)),
            _resident((1, D_MODEL)),
            _resident((D_MODEL, PROJ_WIDTH)),
            _resident((1, PROJ_WIDTH)),
            _resident((MXU_DIM, MXU_DIM)),
        ],
        out_specs=pl.BlockSpec((PROJ_ROW_TILE, PROJ_WIDTH), lambda i: (i, 0)),
        out_shape=jax.ShapeDtypeStruct((n, PROJ_WIDTH), BF16),
        compiler_params=_cparams(("parallel",)),
        name="norm_in_proj",
    )(x2d, g, w_bf16, gains, gmat)


def _deinterleave(x, dil):
    if dil == 1:
        return x
    return jnp.swapaxes(x.reshape(SEQ // dil, dil, LANES), 0, 1).reshape(SEQ, LANES)


def _interleave(x, dil):
    if dil == 1:
        return x
    return jnp.swapaxes(x.reshape(dil, SEQ // dil, LANES), 0, 1).reshape(SEQ, LANES)


def _mixer_a_kernel(q_ref, k_ref, v_ref, bias_ref, o_ref, qp, kp, vp, out_s, lse_s, nat_s):
    n_tiles = SEQ // DIL_BLOCK
    n_branch = len(DIL_PATTERNS)
    lane_full = lax.broadcasted_iota(jnp.int32, (SEQ, LANES), 1)
    lane = lax.broadcasted_iota(jnp.int32, (DIL_BLOCK, LANES), 1)
    head0 = lane < HEAD_DIM

    pad = jnp.zeros((DIL_BLOCK, LANES), BF16)
    for bi, (_, dil) in enumerate(DIL_PATTERNS):
        qd = _deinterleave(q_ref[0].astype(F32), dil)
        qp[bi, 0] = jnp.where(lane_full < HEAD_DIM, qd, 0.0).astype(BF16)
        qp[bi, 1] = jnp.where(lane_full >= HEAD_DIM, qd, 0.0).astype(BF16)
        kp[bi, :DIL_BLOCK, :] = pad
        vp[bi, :DIL_BLOCK, :] = pad
        kp[bi, DIL_BLOCK:, :] = _deinterleave(k_ref[0].astype(F32), dil).astype(BF16)
        vp[bi, DIL_BLOCK:, :] = _deinterleave(v_ref[0].astype(F32), dil).astype(BF16)

    for bi, (_, dil) in enumerate(DIL_PATTERNS):
        nb = SEQ // dil // DIL_BLOCK

        def tile(t, bi=bi, nb=nb):
            row0 = t * DIL_BLOCK if isinstance(t, int) else pl.multiple_of(t * DIL_BLOCK, DIL_BLOCK)
            rows = pl.ds(row0, DIL_BLOCK)
            keys = pl.ds(row0, 2 * DIL_BLOCK)
            tab = 2 * bi + (t % nb == 0)
            kt = kp[bi, keys, :]
            vt = jnp.concatenate([vp[bi, keys, :], jnp.ones((2 * DIL_BLOCK, LANES), BF16)], axis=1)
            m_h, pv_h = [], []
            for hh in range(2):
                s = lax.dot_general(qp[bi, hh, rows, :], kt, (((1,), (1,)), ((), ())),
                                    preferred_element_type=F32) + bias_ref[tab, hh]
                m = jnp.broadcast_to(jnp.max(s, axis=-1, keepdims=True), (DIL_BLOCK, LANES))
                p = jnp.exp2(s - jnp.concatenate([m, m], axis=1))
                m_h.append(m)
                pv_h.append(jnp.dot(p.astype(BF16), vt, preferred_element_type=F32))
            l = jnp.where(head0, pv_h[0][:, LANES:], pv_h[1][:, LANES:])
            out_s[bi, rows, :] = jnp.where(head0, pv_h[0][:, :LANES], pv_h[1][:, :LANES]) / l
            lse_s[bi, rows, :] = jnp.where(head0, m_h[0], m_h[1]) + jnp.log2(l)

        if DIL_GROUP == n_tiles:
            for t in range(n_tiles):
                tile(t)
        else:
            def group(g, carry, tile=tile):
                for u in range(DIL_GROUP):
                    tile(g * DIL_GROUP + u)
                return carry

            lax.fori_loop(0, n_tiles // DIL_GROUP, group, 0)

    for bi, (_, dil) in enumerate(DIL_PATTERNS):
        if dil > 1:
            nat_s[2 * bi - 2] = _interleave(lse_s[bi], dil)
            nat_s[2 * bi - 1] = _interleave(out_s[bi], dil)

    def combine(c, carry):
        rows = pl.ds(pl.multiple_of(c * DIL_BLOCK, DIL_BLOCK), DIL_BLOCK)
        lse_b = [lse_s[0, rows, :]] + [nat_s[2 * bi - 2, rows, :] for bi in range(1, n_branch)]
        out_b = [out_s[0, rows, :]] + [nat_s[2 * bi - 1, rows, :] for bi in range(1, n_branch)]
        lse_max = functools.reduce(jnp.maximum, lse_b)
        w_b = [jnp.exp2(lse - lse_max) for lse in lse_b]
        num = sum(w * o for w, o in zip(w_b, out_b))
        o_ref[0, rows, :] = (num / sum(w_b)).astype(o_ref.dtype)
        return carry

    lax.fori_loop(0, n_tiles, combine, 0, unroll=2)


def _mixer_a_call(proj3d, bias_a):
    b = proj3d.shape[0]
    n_pairs = N_HEADS_A // 2
    n_branch = len(DIL_PATTERNS)
    q_blk0, k_blk0, v_blk0 = 0, WIDTH_A // LANES, 2 * WIDTH_A // LANES
    return pl.pallas_call(
        _mixer_a_kernel,
        grid=(b, n_pairs),
        in_specs=[
            pl.BlockSpec((1, SEQ, LANES), lambda i, p: (i, 0, q_blk0 + p)),
            pl.BlockSpec((1, SEQ, LANES), lambda i, p: (i, 0, k_blk0 + p)),
            pl.BlockSpec((1, SEQ, LANES), lambda i, p: (i, 0, v_blk0 + p)),
            pl.BlockSpec((bias_a.shape[0], 2, DIL_BLOCK, 2 * DIL_BLOCK), lambda i, p: (0, p, 0, 0)),
        ],
        out_specs=pl.BlockSpec((1, SEQ, LANES), lambda i, p: (i, 0, p)),
        out_shape=jax.ShapeDtypeStruct((b, SEQ, WIDTH_A), BF16),
        scratch_shapes=[
            pltpu.VMEM((n_branch, 2, SEQ, LANES), BF16),
            pltpu.VMEM((n_branch, SEQ + DIL_BLOCK, LANES), BF16),
            pltpu.VMEM((n_branch, SEQ + DIL_BLOCK, LANES), BF16),
            pltpu.VMEM((n_branch, SEQ, LANES), F32),
            pltpu.VMEM((n_branch, SEQ, LANES), F32),
            pltpu.VMEM((2 * (n_branch - 1), SEQ, LANES), F32),
        ],
        compiler_params=_cparams(("parallel", "parallel")),
        name="dilated_attention",
    )(proj3d, proj3d, proj3d, bias_a)


def _mixer_b_kernel(far_ref, lam_ref, q_ref, k_ref, v_ref, bias_ref, subg_ref, o_ref, *, lam_init):
    pair = pl.program_id(1)
    t = Q_TILE_B

    def run(n_q):
        lane = lax.broadcasted_iota(jnp.int32, (t, LANES), 1)
        q_maps = []
        for hd in range(2):
            q = q_ref[0, :, hd * LANES:(hd + 1) * LANES]
            zero = jnp.zeros_like(q)
            q_maps.append((jnp.where(lane < HEAD_DIM, q, zero), jnp.where(lane >= HEAD_DIM, q, zero)))
        ones = jnp.ones((t, LANES), BF16)
        state = [None] * 4

        def key_tile(j, kind):
            keys = slice(j * t, (j + 1) * t)
            for hd in range(2):
                kt = k_ref[0, keys, hd * LANES:(hd + 1) * LANES]
                vt = jnp.concatenate([v_ref[0, keys, hd * LANES:(hd + 1) * LANES], ones], axis=1)
                for half in range(2):
                    c = 2 * hd + half
                    s = lax.dot_general(q_maps[hd][half], kt, (((1,), (1,)), ((), ())),
                                        preferred_element_type=F32)
                    if kind == "far":
                        bias = far_ref[2 * pair + hd] * LOG2E
                        m_cur = jnp.broadcast_to(jnp.max(s, axis=-1, keepdims=True) + bias, (t, LANES))
                    else:
                        s = s + bias_ref[1 if kind == "sub" else 0, hd]
                        m_cur = jnp.broadcast_to(jnp.max(s, axis=-1, keepdims=True), (t, LANES))
                    m_new = m_cur if state[c] is None else jnp.maximum(state[c][0], m_cur)
                    shift = m_new - bias if kind == "far" else m_new
                    p = jnp.exp2(s - jnp.concatenate([shift] * (t // LANES), axis=1))
                    pv = jnp.dot(p.astype(BF16), vt, preferred_element_type=F32)
                    if state[c] is not None:
                        alpha = jnp.exp2(state[c][0] - m_new)
                        pv = state[c][1] * jnp.concatenate([alpha, alpha], axis=1) + pv
                    state[c] = (m_new, pv)

        for j in range(n_q - 1):
            key_tile(j, "far")
        if n_q > 0:
            key_tile(n_q - 1, "sub")
        key_tile(n_q, "diag")

        lam_rows = lam_ref[...]
        e1 = jnp.exp(jnp.sum(lam_rows[0:1, :] * lam_rows[1:2, :], axis=-1, keepdims=True))
        e2 = jnp.exp(jnp.sum(lam_rows[2:3, :] * lam_rows[3:4, :], axis=-1, keepdims=True))
        lam = e1 - e2 + lam_init
        for hd in range(2):
            a1, a2 = state[2 * hd][1], state[2 * hd + 1][1]
            o = a1[:, :LANES] / a1[:, LANES:] - lam * (a2[:, :LANES] / a2[:, LANES:])
            ms = jnp.mean(o * o, axis=-1, keepdims=True)
            o_ref[0, :, hd * LANES:(hd + 1) * LANES] = (
                o * lax.rsqrt(ms + EPS) * subg_ref[...] * (1.0 - lam_init)).astype(o_ref.dtype)

    for n_q in range(SEQ // t):
        pl.when(pl.program_id(2) == n_q)(functools.partial(run, n_q))


def _mixer_b_call(proj3d, bias_b, far_b, lam_rows, subg, lam_init):
    b = proj3d.shape[0]
    width = 2 * LANES
    base = 3 * WIDTH_A // width
    q_blk0, k_blk0, v_blk0 = base, base + WIDTH_B // width, base + 2 * WIDTH_B // width
    t = Q_TILE_B
    grid_spec = pltpu.PrefetchScalarGridSpec(
        num_scalar_prefetch=1,
        grid=(b, N_HEADS_B // 2, SEQ // t),
        in_specs=[
            pl.BlockSpec((8, LANES), lambda bi, p, i, far: (0, 0)),
            pl.BlockSpec((1, t, width), lambda bi, p, i, far: (bi, i, q_blk0 + p)),
            pl.BlockSpec((1, SEQ, width), lambda bi, p, i, far: (bi, 0, k_blk0 + p)),
            pl.BlockSpec((1, SEQ, width), lambda bi, p, i, far: (bi, 0, v_blk0 + p)),
            pl.BlockSpec((2, 2, t, t), lambda bi, p, i, far: (0, p, 0, 0)),
            pl.BlockSpec((1, LANES), lambda bi, p, i, far: (0, 0)),
        ],
        out_specs=pl.BlockSpec((1, t, width), lambda bi, p, i, far: (bi, i, p)),
    )
    return pl.pallas_call(
        functools.partial(_mixer_b_kernel, lam_init=lam_init),
        grid_spec=grid_spec,
        out_shape=jax.ShapeDtypeStruct((b, SEQ, WIDTH_B), BF16),
        compiler_params=_cparams(("parallel", "parallel", "arbitrary")),
        name="diff_attention",
    )(far_b, lam_rows, proj3d, proj3d, proj3d, bias_b, subg)


def _mixer_out_residual_norm(oa_ref, ob_ref, w_ref, x_ref, g_ref, rows=slice(None)):
    acc = jnp.dot(oa_ref[rows, :], w_ref[:WIDTH_A, :], preferred_element_type=F32)
    acc = acc + jnp.dot(ob_ref[rows, :], w_ref[WIDTH_A:, :], preferred_element_type=F32)
    x1 = x_ref[rows, :] + acc
    ms = jnp.mean(x1 * x1, axis=-1, keepdims=True)
    return x1, x1 * lax.rsqrt(ms + EPS) * g_ref[...]


def _out_proj_router_kernel(oa_ref, ob_ref, w_ref, x_ref, g_ref, wr_ref, x1_ref, h_ref, route_ref):
    for r0 in range(0, ROW_TILE, ROUTER_ROWS):
        rows = slice(r0, r0 + ROUTER_ROWS)
        x1, hn = _mixer_out_residual_norm(oa_ref, ob_ref, w_ref, x_ref, g_ref, rows)
        x1_ref[rows, :] = x1
        h_ref[rows] = _to_slabs(hn)
        hi = hn.astype(BF16)
        lo = (hn - hi.astype(F32)).astype(BF16)
        lg = (jnp.dot(hi, wr_ref[...], preferred_element_type=F32)
              + jnp.dot(lo, wr_ref[...], preferred_element_type=F32))
        lg = lg + pltpu.roll(lg, LANES - N_EXPERTS, 1)
        lane = lax.broadcasted_iota(jnp.int32, lg.shape, 1)
        lane_f = lane.astype(F32)
        lg = jnp.where(lane < N_EXPERTS, lg, -jnp.inf)
        v1 = jnp.max(lg, axis=-1, keepdims=True)
        i1 = jnp.min(jnp.where(lg == v1, lane_f, float(LANES)), axis=-1, keepdims=True)
        lg2 = jnp.where(lane_f == i1, -jnp.inf, lg)
        v2 = jnp.max(lg2, axis=-1, keepdims=True)
        i2 = jnp.min(jnp.where(lg2 == v2, lane_f, float(LANES)), axis=-1, keepdims=True)
        e = jnp.exp(v2 - v1)
        g1 = 1.0 / (1.0 + e)
        g2 = e / (1.0 + e)
        route_ref[rows, :] = jnp.where(lane == 0, i1, jnp.where(lane == 1, i2,
                                       jnp.where(lane == 2, g1, jnp.where(lane == 3, g2, 0.0))))


def _mixer_out_specs(row):
    return [
        pl.BlockSpec((ROW_TILE, WIDTH_A), row),
        pl.BlockSpec((ROW_TILE, WIDTH_B), row),
        _resident((WIDTH_A + WIDTH_B, D_MODEL)),
        pl.BlockSpec((ROW_TILE, D_MODEL), row),
        _resident((1, D_MODEL)),
    ]


def _out_proj_router_call(oa, ob, w_bf16, x2d, g, w_router_packed):
    n = x2d.shape[0]
    row = lambda i: (i, 0)
    return pl.pallas_call(
        _out_proj_router_kernel,
        grid=(n // ROW_TILE,),
        in_specs=_mixer_out_specs(row) + [_resident((D_MODEL, LANES))],
        out_specs=[
            pl.BlockSpec((ROW_TILE, D_MODEL), row),
            pl.BlockSpec((ROW_TILE, *ROW_SLAB), lambda i: (i, 0, 0)),
            pl.BlockSpec((ROW_TILE, LANES), row),
        ],
        out_shape=[
            jax.ShapeDtypeStruct((n, D_MODEL), F32),
            jax.ShapeDtypeStruct((n, *ROW_SLAB), F32),
            jax.ShapeDtypeStruct((n, LANES), F32),
        ],
        compiler_params=_cparams(("parallel",)),
        name="out_proj_router",
    )(oa, ob, w_bf16, x2d, g, w_router_packed)


def _swiglu_act(g, u):
    return (g / (1.0 + jnp.exp(-g))) * u


def _out_proj_ffn_kernel(oa_ref, ob_ref, w_ref, x_ref, g_ref, wg_ref, wu_ref, wd_ref, *rest, n_cast):
    cast_in, o_ref, cast_out, a_ref = rest[:n_cast], rest[n_cast], rest[n_cast + 1:-1], rest[-1]
    x1, hn = _mixer_out_residual_norm(oa_ref, ob_ref, w_ref, x_ref, g_ref)
    h = hn.astype(BF16)
    c0 = 0
    for width in FFN_CHUNKS:
        g = jnp.dot(h, wg_ref[:, c0:c0 + width], preferred_element_type=F32)
        u = jnp.dot(h, wu_ref[:, c0:c0 + width], preferred_element_type=F32)
        a_ref[:, c0:c0 + width] = _swiglu_act(g, u).astype(BF16)
        c0 += width
    o_ref[...] = x1 + jnp.dot(a_ref[...], wd_ref[...], preferred_element_type=F32)
    for src, dst in zip(cast_in, cast_out):
        dst[...] = src[...].astype(dst.dtype)


def _out_proj_ffn_call(oa, ob, w_bf16, x2d, g, wg, wu, wd, cast_along=()):
    n = x2d.shape[0]
    steps = n // ROW_TILE
    row = lambda i: (i, 0)
    cast_specs = [pl.BlockSpec((c.shape[0] // steps, c.shape[1]), row) for c in cast_along]
    assert all(c.shape[0] % (16 * steps) == 0 for c in cast_along)
    return pl.pallas_call(
        functools.partial(_out_proj_ffn_kernel, n_cast=len(cast_along)),
        grid=(steps,),
        in_specs=_mixer_out_specs(row) + [
            _resident((D_MODEL, D_FF)),
            _resident((D_MODEL, D_FF)),
            _resident((D_FF, D_MODEL)),
        ] + cast_specs,
        out_specs=[pl.BlockSpec((ROW_TILE, D_MODEL), row)] + cast_specs,
        out_shape=[jax.ShapeDtypeStruct((n, D_MODEL), F32)]
        + [jax.ShapeDtypeStruct(c.shape, BF16) for c in cast_along],
        scratch_shapes=[pltpu.VMEM((ROW_TILE, D_FF), BF16)],
        compiler_params=_cparams(("parallel",)),
        name="out_proj_dense_swiglu",
    )(oa, ob, w_bf16, x2d, g, wg, wu, wd, *cast_along)


ROW_SLAB = (8, LANES)
assert ROW_SLAB[0] * ROW_SLAB[1] == D_MODEL


def _to_slabs(x):
    r = x.shape[0]
    xs = jnp.stack([x[:, LANES * j:LANES * (j + 1)] for j in range(ROW_SLAB[0])], axis=0)
    xs = xs.reshape(ROW_SLAB[0], r // 8, 8, LANES)
    xs = jnp.swapaxes(jnp.swapaxes(xs, 0, 1), 1, 2)
    return xs.reshape(r, *ROW_SLAB)


def _from_slabs(t):
    r = t.shape[0]
    y = jnp.swapaxes(t.reshape(r // 8, 8, *ROW_SLAB), 1, 2)
    return jnp.concatenate([y[:, j].reshape(r, LANES) for j in range(ROW_SLAB[0])], axis=1)


def _row_copy(src_ref, src_row, dst_ref, dst_row, sem):
    return pltpu.make_async_copy(src_ref.at[src_row], dst_ref.at[dst_row], sem)


def _dispatch_kernel(fill_ref, dest_ref, h_ref, o_hbm, stage, zero_ref, sems, zero_sem):
    i = pl.program_id(0)
    last = pl.num_programs(0) - 1
    slot = i % 2

    def drain(s):
        def wait(t, c):
            for k in range(TOP_K):
                _row_copy(stage.at[s], t, o_hbm, 0, sems.at[s]).wait()
            return c
        lax.fori_loop(0, DISPATCH_TILE, wait, 0, unroll=4)

    @pl.when(i >= 2)
    def _():
        drain(slot)

    stage[slot] = h_ref[...]

    def start(t, c):
        for k in range(TOP_K):
            _row_copy(stage.at[slot], t, o_hbm, dest_ref[0, 0, TOP_K * t + k], sems.at[slot]).start(priority=k)
        return c

    lax.fori_loop(0, DISPATCH_TILE, start, 0, unroll=4)

    @pl.when(i == last)
    def _():
        drain(slot)

        @pl.when(last >= 1)
        def _():
            drain(1 - slot)

        zero_ref[...] = jnp.zeros_like(zero_ref)
        for e in range(N_EXPERTS):
            lo, hi = fill_ref[e], fill_ref[N_EXPERTS + e]
            lax.fori_loop(lo, hi, lambda r, c: (_row_copy(zero_ref, 0, o_hbm, r, zero_sem).start(), c)[1], 0)
            lax.fori_loop(lo, hi, lambda r, c: (_row_copy(zero_ref, 0, o_hbm, 0, zero_sem).wait(), c)[1], 0)


def _dispatch_call(fill, dest3d, h, n_rows):
    n = h.shape[0]
    grid_spec = pltpu.PrefetchScalarGridSpec(
        num_scalar_prefetch=1,
        grid=(n // DISPATCH_TILE,),
        in_specs=[
            pl.BlockSpec((1, 1, TOP_K * DISPATCH_TILE), lambda i, f: (i, 0, 0), memory_space=pltpu.SMEM),
            pl.BlockSpec((DISPATCH_TILE, *ROW_SLAB), lambda i, f: (i, 0, 0)),
        ],
        out_specs=pl.BlockSpec(memory_space=pl.ANY),
        scratch_shapes=[pltpu.VMEM((2, DISPATCH_TILE, *ROW_SLAB), h.dtype), pltpu.VMEM((1, *ROW_SLAB), h.dtype),
                        pltpu.SemaphoreType.DMA((2,)), pltpu.SemaphoreType.DMA(())],
    )
    return pl.pallas_call(
        _dispatch_kernel,
        grid_spec=grid_spec,
        out_shape=jax.ShapeDtypeStruct((n_rows, *ROW_SLAB), h.dtype),
        compiler_params=_cparams(("arbitrary",)),
        name="moe_dispatch",
    )(fill, dest3d, h)


def _moe_kernel(be_ref, nact_ref, x_ref, wg_ref, wu_ref, wd_ref, o_ref, a_ref):
    active = pl.program_id(0) < nact_ref[0]

    @pl.when(active)
    def _():
        xb = _from_slabs(x_ref[...]).astype(BF16)
        for c0 in range(0, D_FF_EXPERT, MOE_FF_TILE):
            cols = slice(c0, c0 + MOE_FF_TILE)
            g = jnp.dot(xb, wg_ref[0, :, cols], preferred_element_type=F32)
            u = jnp.dot(xb, wu_ref[0, :, cols], preferred_element_type=F32)
            a_ref[:, cols] = _swiglu_act(g, u).astype(BF16)
        o_ref[...] = _to_slabs(jnp.dot(a_ref[...], wd_ref[0], preferred_element_type=F32))

    @pl.when(jnp.logical_not(active))
    def _():
        o_ref[...] = jnp.zeros_like(o_ref)


def _moe_call(block_expert, n_active, x_pad, wg, wu, wd):
    n_rows = x_pad.shape[0]
    expert = lambda b, be, na: (be[b], 0, 0)
    grid_spec = pltpu.PrefetchScalarGridSpec(
        num_scalar_prefetch=2,
        grid=(n_rows // MOE_BLOCK,),
        in_specs=[
            pl.BlockSpec((MOE_BLOCK, *ROW_SLAB), lambda b, be, na: (b, 0, 0)),
            pl.BlockSpec((1, D_MODEL, D_FF_EXPERT), expert, pipeline_mode=pl.Buffered(1)),
            pl.BlockSpec((1, D_MODEL, D_FF_EXPERT), expert, pipeline_mode=pl.Buffered(1)),
            pl.BlockSpec((1, D_FF_EXPERT, D_MODEL), expert),
        ],
        out_specs=pl.BlockSpec((MOE_BLOCK, *ROW_SLAB), lambda b, be, na: (b, 0, 0)),
        scratch_shapes=[pltpu.VMEM((MOE_BLOCK, D_FF_EXPERT), BF16)],
    )
    return pl.pallas_call(
        _moe_kernel,
        grid_spec=grid_spec,
        out_shape=jax.ShapeDtypeStruct((n_rows, *ROW_SLAB), F32),
        compiler_params=_cparams(("arbitrary",)),
        name="moe_swiglu",
    )(block_expert, n_active, x_pad, wg, wu, wd)


def _combine_kernel(pos_ref, pos_next_ref, y_hbm, x_ref, route_ref, o_ref, buf, sems):
    i = pl.program_id(0)
    slot = i % 2

    def gather(idx_ref, into):
        def start(t, c):
            for k in range(TOP_K):
                _row_copy(y_hbm, idx_ref[0, 0, TOP_K * t + k], buf.at[into, k], t,
                          sems.at[into]).start(priority=k)
            return c
        lax.fori_loop(0, COMBINE_TILE, start, 0, unroll=4)

    @pl.when(i == 0)
    def _():
        gather(pos_ref, 0)

    @pl.when(i + 1 < pl.num_programs(0))
    def _():
        gather(pos_next_ref, 1 - slot)

    def wait(t, c):
        for k in range(TOP_K):
            _row_copy(y_hbm, 0, buf.at[slot, k], t, sems.at[slot]).wait()
        return c

    lax.fori_loop(0, COMBINE_TILE, wait, 0, unroll=4)
    route = route_ref[...]
    g0 = route[:, TOP_K:TOP_K + 1]
    g1 = route[:, TOP_K + 1:TOP_K + 2]
    o_ref[...] = x_ref[...] + g0 * _from_slabs(buf[slot, 0]) + g1 * _from_slabs(buf[slot, 1])


def _combine_call(pos3d, y_pad, x2d, route):
    n = x2d.shape[0]
    n_tiles = n // COMBINE_TILE
    row = lambda i: (i, 0)
    idx_block = (1, 1, TOP_K * COMBINE_TILE)
    return pl.pallas_call(
        _combine_kernel,
        grid=(n_tiles,),
        in_specs=[
            pl.BlockSpec(idx_block, lambda i: (i, 0, 0), memory_space=pltpu.SMEM),
            pl.BlockSpec(idx_block, lambda i: (jnp.minimum(i + 1, n_tiles - 1), 0, 0), memory_space=pltpu.SMEM),
            pl.BlockSpec(memory_space=pl.ANY),
            pl.BlockSpec((COMBINE_TILE, D_MODEL), row),
            pl.BlockSpec((COMBINE_TILE, LANES), row),
        ],
        out_specs=pl.BlockSpec((COMBINE_TILE, D_MODEL), row),
        out_shape=jax.ShapeDtypeStruct((n, D_MODEL), F32),
        scratch_shapes=[pltpu.VMEM((2, TOP_K, COMBINE_TILE, *ROW_SLAB), F32), pltpu.SemaphoreType.DMA((2,))],
        compiler_params=_cparams(("arbitrary",)),
        name="moe_combine",
    )(pos3d, pos3d, y_pad, x2d, route)


def _moe_layer(x1, hn, route, wg, wu, wd):
    n = x1.shape[0]
    e_flat = route[:, :TOP_K].astype(jnp.int32).reshape(-1)
    onehot = (e_flat[:, None] == jnp.arange(N_EXPERTS, dtype=jnp.int32)[None, :]).astype(jnp.int32)
    csum = jnp.cumsum(onehot, axis=0)
    counts = csum[-1]
    padded = (counts + MOE_BLOCK - 1) // MOE_BLOCK * MOE_BLOCK
    pad_ends = jnp.cumsum(padded)
    pad_starts = pad_ends - padded
    dest = jnp.sum((csum - onehot + pad_starts[None, :]) * onehot, axis=1)
    n_blocks = -(-(n * TOP_K) // MOE_BLOCK) + N_EXPERTS
    n_rows = n_blocks * MOE_BLOCK
    block_start = jnp.arange(n_blocks, dtype=jnp.int32) * MOE_BLOCK
    block_expert = jnp.minimum(
        jnp.sum((pad_ends[None, :] <= block_start[:, None]).astype(jnp.int32), axis=1), N_EXPERTS - 1)
    n_active = (pad_ends[-1:] // MOE_BLOCK).astype(jnp.int32)
    fill = jnp.concatenate([pad_starts + counts, pad_starts[1:], jnp.full((1,), n_rows, jnp.int32)])

    x_pad = _dispatch_call(fill.astype(jnp.int32), dest.reshape(n // DISPATCH_TILE, 1, TOP_K * DISPATCH_TILE),
                           hn, n_rows)
    y_pad = _moe_call(block_expert.astype(jnp.int32), n_active, x_pad, wg, wu, wd)
    pos3d = dest.reshape(n // COMBINE_TILE, 1, TOP_K * COMBINE_TILE)
    return _combine_call(pos3d, y_pad, x1, route)


def _head_group_matrix():
    idx = np.arange(MXU_DIM) // HEAD_DIM
    return jnp.asarray((idx[:, None] == idx[None, :]).astype(np.float32), dtype=BF16)


def _pack_router(w_router):
    hi = w_router.astype(BF16)
    lo = (w_router - hi.astype(F32)).astype(BF16)
    pad = jnp.zeros((D_MODEL, LANES - 2 * N_EXPERTS), BF16)
    return jnp.concatenate([hi, lo, pad], axis=1)


def kernel(x, norm_mix_g, norm_ffn_g, w_in, w_out, q_norm_a, k_norm_a, q_norm_b, k_norm_b,
           lambda_q1, lambda_k1, lambda_q2, lambda_k2, subln_g, rel_bias,
           w_gate_dense, w_up_dense, w_down_dense, w_router, w_gate_moe, w_up_moe, w_down_moe):
    b, s, d = x.shape
    assert (s, d) == (SEQ, D_MODEL)
    depth = w_in.shape[0]
    n = b * s
    x2d = x.reshape(n, d)
    gmat = _head_group_matrix()
    bias_a = _bias_tables(rel_bias, _dilated_buckets(), 0, N_HEADS_A)
    bias_b = _bias_tables(rel_bias, _diff_buckets(), N_HEADS_A, N_HEADS_B)
    far_b = rel_bias[_far_bucket(), N_HEADS_A:]
    ones = jnp.ones((PROJ_COL_TILE,), F32)
    rep = PROJ_COL_TILE // HEAD_DIM

    precast = {}

    def bf16_weight(name, w, idx):
        return precast[name, idx] if (name, idx) in precast else w[idx].astype(BF16)

    for layer in range(depth):
        gains = jnp.concatenate([
            jnp.tile(q_norm_a[layer], rep) * (ATTN_SCALE * LOG2E), jnp.tile(k_norm_a[layer], rep), ones,
            jnp.tile(q_norm_b[layer], rep) * (ATTN_SCALE * LOG2E), jnp.tile(k_norm_b[layer], rep), ones,
        ]).reshape(1, PROJ_WIDTH)
        proj = _proj_call(x2d, norm_mix_g[layer].reshape(1, d), bf16_weight("w_in", w_in, layer), gains, gmat)
        proj3d = proj.reshape(b, s, PROJ_WIDTH)

        oa = _mixer_a_call(proj3d, bias_a)
        lam_init = 0.8 - 0.6 * math.exp(-0.3 * layer)
        lam_rows = jnp.zeros((8, LANES), F32).at[:4, :HEAD_DIM].set(
            jnp.stack([lambda_q1[layer], lambda_k1[layer], lambda_q2[layer], lambda_k2[layer]]))
        ob = _mixer_b_call(proj3d, bias_b, far_b, lam_rows, subln_g[layer].reshape(1, LANES), lam_init)

        is_moe = layer % 2 == 1
        i = layer // 2
        mixed = (oa.reshape(n, WIDTH_A), ob.reshape(n, WIDTH_B), bf16_weight("w_out", w_out, layer), x2d,
                 norm_ffn_g[layer].reshape(1, d))
        if is_moe:
            x1, hn, route = _out_proj_router_call(*mixed, _pack_router(w_router[i]))
            x2d = _moe_layer(x1, hn, route, bf16_weight("w_gate_moe", w_gate_moe, i),
                             bf16_weight("w_up_moe", w_up_moe, i), bf16_weight("w_down_moe", w_down_moe, i))
        else:
            nxt = {}
            if layer + 1 < depth:
                nxt = {("w_in", layer + 1): w_in[layer + 1], ("w_out", layer + 1): w_out[layer + 1],
                       ("w_gate_moe", i): w_gate_moe[i], ("w_up_moe", i): w_up_moe[i],
                       ("w_down_moe", i): w_down_moe[i]}
            x2d, *cast = _out_proj_ffn_call(
                *mixed, w_gate_dense[i].astype(BF16), w_up_dense[i].astype(BF16), w_down_dense[i].astype(BF16),
                tuple(w.reshape(-1, w.shape[-1]) for w in nxt.values()))
            for (key, w), c in zip(nxt.items(), cast):
                precast[key] = c.reshape(w.shape)
    return x2d.reshape(b, s, d)
```

```python
import functools
import math

import numpy as np
import jax
import jax.numpy as jnp
from jax import lax
from jax.experimental import pallas as pl
from jax.experimental.pallas import tpu as pltpu

F32 = jnp.float32
BF16 = jnp.bfloat16

D_MODEL = 1024
SEQ = 2048
HEAD_DIM = 64
ATTN_SCALE = HEAD_DIM ** -0.5
LOG2E = math.log2(math.e)
N_HEADS_A = 8
DIL_PATTERNS = ((128, 1), (512, 4), (2048, 16))
DIL_BLOCK = 128
N_HEADS_B = 4
WIDTH_A = N_HEADS_A * HEAD_DIM
WIDTH_B = N_HEADS_B * 2 * HEAD_DIM
PROJ_WIDTH = 3 * (WIDTH_A + WIDTH_B)
N_BUCKETS = 32
MAX_DISTANCE = 128
D_FF = 2816
N_EXPERTS = 8
TOP_K = 2
D_FF_EXPERT = 3584
EPS = 1e-6
NEG = -1e30

LANES = 128
MXU_DIM = 256
VMEM_LIMIT = 56 * 1024 * 1024

ROW_TILE = 512
PROJ_ROW_TILE = 1024
PROJ_COL_TILE = 512
ROUTER_ROWS = 256
Q_TILE_B = 512
MOE_BLOCK = 512
MOE_FF_TILE = 512
FFN_CHUNKS = (512, 512, 512, 512, 512, 256)
DISPATCH_TILE = 512
COMBINE_TILE = 512


def _cparams(sem):
    return pltpu.CompilerParams(dimension_semantics=sem, vmem_limit_bytes=VMEM_LIMIT)


def _resident(shape):
    nd = len(shape)
    return pl.BlockSpec(shape, lambda *_: (0,) * nd, pipeline_mode=pl.Buffered(1))


def _t5_bucket_np(dist):
    n = np.maximum(dist, 0)
    max_exact = N_BUCKETS // 2
    large = max_exact + (np.log(np.maximum(n, 1).astype(np.float32) / max_exact)
                         / math.log(MAX_DISTANCE / max_exact) * (N_BUCKETS - max_exact)).astype(np.int32)
    large = np.minimum(large, N_BUCKETS - 1)
    return np.where(n < max_exact, n, large).astype(np.int32)


def _dilated_buckets():
    i = np.arange(DIL_BLOCK)[:, None]
    j = np.arange(2 * DIL_BLOCK)[None, :]
    steps = DIL_BLOCK + i - j
    valid = (steps >= 0) & (steps <= DIL_BLOCK)
    tabs = []
    for _, dil in DIL_PATTERNS:
        bucket = np.where(valid, _t5_bucket_np(steps * dil), -1)
        tabs.append(bucket)
        tabs.append(np.where(j >= DIL_BLOCK, bucket, -1))
    return np.stack(tabs).astype(np.int32)


def _diff_buckets():
    t = Q_TILE_B
    qi = np.arange(t)[:, None]
    kj = np.arange(t)[None, :]
    tabs = []
    for off in (0, t):
        dist = qi - kj + off
        tabs.append(np.where(dist >= 0, _t5_bucket_np(dist), -1))
    return np.stack(tabs).astype(np.int32)


def _far_bucket():
    b = _t5_bucket_np(np.arange(Q_TILE_B + 1, SEQ + 1))
    assert (b == b[0]).all()
    return int(b[0])


def _bias_table_kernel(rel_ref, bucket_ref, o_ref, *, head0):
    col = head0 + pl.program_id(1)
    bucket = bucket_ref[0]
    val = jnp.full(bucket.shape, NEG, F32)
    for b in range(N_BUCKETS):
        val = jnp.where(bucket == b, rel_ref[b, col] * LOG2E, val)
    o_ref[0, 0] = val


def _bias_tables(rel_bias, buckets, head0, n_heads):
    n_tab, rows, cols = buckets.shape
    return pl.pallas_call(
        functools.partial(_bias_table_kernel, head0=head0),
        grid=(n_tab, n_heads),
        in_specs=[
            pl.BlockSpec(memory_space=pltpu.SMEM),
            pl.BlockSpec((1, rows, cols), lambda t, h: (t, 0, 0)),
        ],
        out_specs=pl.BlockSpec((1, 1, rows, cols), lambda t, h: (t, h, 0, 0)),
        out_shape=jax.ShapeDtypeStruct((n_tab, n_heads, rows, cols), F32),
        compiler_params=_cparams(("parallel", "parallel")),
        name="bias_tables",
    )(rel_bias, jnp.asarray(buckets))


def _proj_kernel(x_ref, g_ref, w_ref, gain_ref, gmat_ref, o_ref):
    x = x_ref[...]
    ms = jnp.mean(x * x, axis=-1, keepdims=True)
    xn = (x * lax.rsqrt(ms + EPS) * g_ref[...]).astype(BF16)
    for j in range(PROJ_WIDTH // PROJ_COL_TILE):
        cols = slice(j * PROJ_COL_TILE, (j + 1) * PROJ_COL_TILE)
        acc = jnp.dot(xn, w_ref[:, cols], preferred_element_type=F32)
        if j % 3 == 2:
            o_ref[:, cols] = acc.astype(BF16)
        else:
            sq = (acc * acc).astype(BF16)
            ss = jnp.concatenate(
                [jnp.dot(sq[:, c:c + MXU_DIM], gmat_ref[...], preferred_element_type=F32)
                 for c in range(0, PROJ_COL_TILE, MXU_DIM)], axis=1)
            o_ref[:, cols] = (acc * lax.rsqrt(ss * (1.0 / HEAD_DIM) + EPS) * gain_ref[:, cols]).astype(BF16)


def _proj_call(x2d, g, w_bf16, gains, gmat):
    n = x2d.shape[0]
    return pl.pallas_call(
        _proj_kernel,
        grid=(n // PROJ_ROW_TILE,),
        in_specs=[
            pl.BlockSpec((PROJ_ROW_TILE, D_MODEL), lambda i: (i, 0)),
            _resident((1, D_MODEL)),
            _resident((D_MODEL, PROJ_WIDTH)),
            _resident((1, PROJ_WIDTH)),
            _resident((MXU_DIM, MXU_DIM)),
        ],
        out_specs=pl.BlockSpec((PROJ_ROW_TILE, PROJ_WIDTH), lambda i: (i, 0)),
        out_shape=jax.ShapeDtypeStruct((n, PROJ_WIDTH), BF16),
        compiler_params=_cparams(("parallel",)),
        name="norm_in_proj",
    )(x2d, g, w_bf16, gains, gmat)


def _deinterleave(x, dil):
    if dil == 1:
        return x
    return jnp.swapaxes(x.reshape(SEQ // dil, dil, LANES), 0, 1).reshape(SEQ, LANES)


def _interleave(x, dil):
    if dil == 1:
        return x
    return jnp.swapaxes(x.reshape(dil, SEQ // dil, LANES), 0, 1).reshape(SEQ, LANES)


def _mixer_a_kernel(q_ref, k_ref, v_ref, bias_ref, o_ref, qp, kp, vp, out_s, lse_s, nat_s):
    n_tiles = SEQ // DIL_BLOCK
    n_branch = len(DIL_PATTERNS)
    lane_full = lax.broadcasted_iota(jnp.int32, (SEQ, LANES), 1)
    lane = lax.broadcasted_iota(jnp.int32, (DIL_BLOCK, LANES), 1)
    head0 = lane < HEAD_DIM

    pad = jnp.zeros((DIL_BLOCK, LANES), BF16)
    for bi, (_, dil) in enumerate(DIL_PATTERNS):
        qd = _deinterleave(q_ref[0].astype(F32), dil)
        qp[bi, 0] = jnp.where(lane_full < HEAD_DIM, qd, 0.0).astype(BF16)
        qp[bi, 1] = jnp.where(lane_full >= HEAD_DIM, qd, 0.0).astype(BF16)
        kp[bi, :DIL_BLOCK, :] = pad
        vp[bi, :DIL_BLOCK, :] = pad
        kp[bi, DIL_BLOCK:, :] = _deinterleave(k_ref[0].astype(F32), dil).astype(BF16)
        vp[bi, DIL_BLOCK:, :] = _deinterleave(v_ref[0].astype(F32), dil).astype(BF16)

    for bi, (_, dil) in enumerate(DIL_PATTERNS):
        nb = SEQ // dil // DIL_BLOCK

        def tile(t, bi=bi, nb=nb):
            rows = slice(t * DIL_BLOCK, (t + 1) * DIL_BLOCK)
            keys = slice(t * DIL_BLOCK, (t + 2) * DIL_BLOCK)
            tab = 2 * bi + (t % nb == 0)
            kt = kp[bi, keys, :]
            vt = jnp.concatenate([vp[bi, keys, :], jnp.ones((2 * DIL_BLOCK, LANES), BF16)], axis=1)
            m_h, pv_h = [], []
            for hh in range(2):
                s = lax.dot_general(qp[bi, hh, rows, :], kt, (((1,), (1,)), ((), ())),
                                    preferred_element_type=F32) + bias_ref[tab, hh]
                m = jnp.broadcast_to(jnp.max(s, axis=-1, keepdims=True), (DIL_BLOCK, LANES))
                p = jnp.exp2(s - jnp.concatenate([m, m], axis=1))
                m_h.append(m)
                pv_h.append(jnp.dot(p.astype(BF16), vt, preferred_element_type=F32))
            l = jnp.where(head0, pv_h[0][:, LANES:], pv_h[1][:, LANES:])
            out_s[bi, rows, :] = jnp.where(head0, pv_h[0][:, :LANES], pv_h[1][:, :LANES]) / l
            lse_s[bi, rows, :] = jnp.where(head0, m_h[0], m_h[1]) + jnp.log2(l)

        for t in range(n_tiles):
            tile(t)

    for bi, (_, dil) in enumerate(DIL_PATTERNS):
        if dil > 1:
            nat_s[2 * bi - 2] = _interleave(lse_s[bi], dil)
            nat_s[2 * bi - 1] = _interleave(out_s[bi], dil)

    def combine(c, carry):
        rows = pl.ds(pl.multiple_of(c * DIL_BLOCK, DIL_BLOCK), DIL_BLOCK)
        lse_b = [lse_s[0, rows, :]] + [nat_s[2 * bi - 2, rows, :] for bi in range(1, n_branch)]
        out_b = [out_s[0, rows, :]] + [nat_s[2 * bi - 1, rows, :] for bi in range(1, n_branch)]
        lse_max = functools.reduce(jnp.maximum, lse_b)
        w_b = [jnp.exp2(lse - lse_max) for lse in lse_b]
        num = sum(w * o for w, o in zip(w_b, out_b))
        o_ref[0, rows, :] = (num / sum(w_b)).astype(o_ref.dtype)
        return carry

    lax.fori_loop(0, n_tiles, combine, 0, unroll=2)


def _mixer_a_call(proj3d, bias_a):
    b = proj3d.shape[0]
    n_pairs = N_HEADS_A // 2
    n_branch = len(DIL_PATTERNS)
    q_blk0, k_blk0, v_blk0 = 0, WIDTH_A // LANES, 2 * WIDTH_A // LANES
    return pl.pallas_call(
        _mixer_a_kernel,
        grid=(b, n_pairs),
        in_specs=[
            pl.BlockSpec((1, SEQ, LANES), lambda i, p: (i, 0, q_blk0 + p)),
            pl.BlockSpec((1, SEQ, LANES), lambda i, p: (i, 0, k_blk0 + p)),
            pl.BlockSpec((1, SEQ, LANES), lambda i, p: (i, 0, v_blk0 + p)),
            pl.BlockSpec((bias_a.shape[0], 2, DIL_BLOCK, 2 * DIL_BLOCK), lambda i, p: (0, p, 0, 0)),
        ],
        out_specs=pl.BlockSpec((1, SEQ, LANES), lambda i, p: (i, 0, p)),
        out_shape=jax.ShapeDtypeStruct((b, SEQ, WIDTH_A), BF16),
        scratch_shapes=[
            pltpu.VMEM((n_branch, 2, SEQ, LANES), BF16),
            pltpu.VMEM((n_branch, SEQ + DIL_BLOCK, LANES), BF16),
            pltpu.VMEM((n_branch, SEQ + DIL_BLOCK, LANES), BF16),
            pltpu.VMEM((n_branch, SEQ, LANES), F32),
            pltpu.VMEM((n_branch, SEQ, LANES), F32),
            pltpu.VMEM((2 * (n_branch - 1), SEQ, LANES), F32),
        ],
        compiler_params=_cparams(("parallel", "parallel")),
        name="dilated_attention",
    )(proj3d, proj3d, proj3d, bias_a)


def _mixer_b_kernel(far_ref, lam_ref, q_ref, k_ref, v_ref, bias_ref, subg_ref, o_ref, *, lam_init):
    pair = pl.program_id(1)
    t = Q_TILE_B

    def run(n_q):
        lane = lax.broadcasted_iota(jnp.int32, (t, LANES), 1)
        q_maps = []
        for hd in range(2):
            q = q_ref[0, :, hd * LANES:(hd + 1) * LANES]
            zero = jnp.zeros_like(q)
            q_maps.append((jnp.where(lane < HEAD_DIM, q, zero), jnp.where(lane >= HEAD_DIM, q, zero)))
        ones = jnp.ones((t, LANES), BF16)
        state = [None] * 4

        def key_tile(j, kind):
            keys = slice(j * t, (j + 1) * t)
            for hd in range(2):
                kt = k_ref[0, keys, hd * LANES:(hd + 1) * LANES]
                vt = jnp.concatenate([v_ref[0, keys, hd * LANES:(hd + 1) * LANES], ones], axis=1)
                for half in range(2):
                    c = 2 * hd + half
                    s = lax.dot_general(q_maps[hd][half], kt, (((1,), (1,)), ((), ())),
                                        preferred_element_type=F32)
                    if kind == "far":
                        bias = far_ref[2 * pair + hd] * LOG2E
                        m_cur = jnp.broadcast_to(jnp.max(s, axis=-1, keepdims=True) + bias, (t, LANES))
                    else:
                        s = s + bias_ref[1 if kind == "sub" else 0, hd]
                        m_cur = jnp.broadcast_to(jnp.max(s, axis=-1, keepdims=True), (t, LANES))
                    m_new = m_cur if state[c] is None else jnp.maximum(state[c][0], m_cur)
                    shift = m_new - bias if kind == "far" else m_new
                    p = jnp.exp2(s - jnp.concatenate([shift] * (t // LANES), axis=1))
                    pv = jnp.dot(p.astype(BF16), vt, preferred_element_type=F32)
                    if state[c] is not None:
                        alpha = jnp.exp2(state[c][0] - m_new)
                        pv = state[c][1] * jnp.concatenate([alpha, alpha], axis=1) + pv
                    state[c] = (m_new, pv)

        for j in range(n_q - 1):
            key_tile(j, "far")
        if n_q > 0:
            key_tile(n_q - 1, "sub")
        key_tile(n_q, "diag")

        lam_rows = lam_ref[...]
        e1 = jnp.exp(jnp.sum(lam_rows[0:1, :] * lam_rows[1:2, :], axis=-1, keepdims=True))
        e2 = jnp.exp(jnp.sum(lam_rows[2:3, :] * lam_rows[3:4, :], axis=-1, keepdims=True))
        lam = e1 - e2 + lam_init
        for hd in range(2):
            a1, a2 = state[2 * hd][1], state[2 * hd + 1][1]
            o = a1[:, :LANES] / a1[:, LANES:] - lam * (a2[:, :LANES] / a2[:, LANES:])
            ms = jnp.mean(o * o, axis=-1, keepdims=True)
            o_ref[0, :, hd * LANES:(hd + 1) * LANES] = (
                o * lax.rsqrt(ms + EPS) * subg_ref[...] * (1.0 - lam_init)).astype(o_ref.dtype)

    for n_q in range(SEQ // t):
        pl.when(pl.program_id(2) == n_q)(functools.partial(run, n_q))


def _mixer_b_call(proj3d, bias_b, far_b, lam_rows, subg, lam_init):
    b = proj3d.shape[0]
    width = 2 * LANES
    base = 3 * WIDTH_A // width
    q_blk0, k_blk0, v_blk0 = base, base + WIDTH_B // width, base + 2 * WIDTH_B // width
    t = Q_TILE_B
    grid_spec = pltpu.PrefetchScalarGridSpec(
        num_scalar_prefetch=1,
        grid=(b, N_HEADS_B // 2, SEQ // t),
        in_specs=[
            pl.BlockSpec((8, LANES), lambda bi, p, i, far: (0, 0)),
            pl.BlockSpec((1, t, width), lambda bi, p, i, far: (bi, i, q_blk0 + p)),
            pl.BlockSpec((1, SEQ, width), lambda bi, p, i, far: (bi, 0, k_blk0 + p)),
            pl.BlockSpec((1, SEQ, width), lambda bi, p, i, far: (bi, 0, v_blk0 + p)),
            pl.BlockSpec((2, 2, t, t), lambda bi, p, i, far: (0, p, 0, 0)),
            pl.BlockSpec((1, LANES), lambda bi, p, i, far: (0, 0)),
        ],
        out_specs=pl.BlockSpec((1, t, width), lambda bi, p, i, far: (bi, i, p)),
    )
    return pl.pallas_call(
        functools.partial(_mixer_b_kernel, lam_init=lam_init),
        grid_spec=grid_spec,
        out_shape=jax.ShapeDtypeStruct((b, SEQ, WIDTH_B), BF16),
        compiler_params=_cparams(("parallel", "parallel", "arbitrary")),
        name="diff_attention",
    )(far_b, lam_rows, proj3d, proj3d, proj3d, bias_b, subg)


def _mixer_out_residual_norm(oa_ref, ob_ref, w_ref, x_ref, g_ref, rows=slice(None)):
    acc = jnp.dot(oa_ref[rows, :], w_ref[:WIDTH_A, :], preferred_element_type=F32)
    acc = acc + jnp.dot(ob_ref[rows, :], w_ref[WIDTH_A:, :], preferred_element_type=F32)
    x1 = x_ref[rows, :] + acc
    ms = jnp.mean(x1 * x1, axis=-1, keepdims=True)
    return x1, x1 * lax.rsqrt(ms + EPS) * g_ref[...]


def _out_proj_router_kernel(oa_ref, ob_ref, w_ref, x_ref, g_ref, wr_ref, x1_ref, h_ref, route_ref):
    for r0 in range(0, ROW_TILE, ROUTER_ROWS):
        rows = slice(r0, r0 + ROUTER_ROWS)
        x1, hn = _mixer_out_residual_norm(oa_ref, ob_ref, w_ref, x_ref, g_ref, rows)
        x1_ref[rows, :] = x1
        h_ref[rows] = _to_slabs(hn)
        hi = hn.astype(BF16)
        lo = (hn - hi.astype(F32)).astype(BF16)
        lg = (jnp.dot(hi, wr_ref[...], preferred_element_type=F32)
              + jnp.dot(lo, wr_ref[...], preferred_element_type=F32))
        lg = lg + pltpu.roll(lg, LANES - N_EXPERTS, 1)
        lane = lax.broadcasted_iota(jnp.int32, lg.shape, 1)
        lane_f = lane.astype(F32)
        lg = jnp.where(lane < N_EXPERTS, lg, -jnp.inf)
        v1 = jnp.max(lg, axis=-1, keepdims=True)
        i1 = jnp.min(jnp.where(lg == v1, lane_f, float(LANES)), axis=-1, keepdims=True)
        lg2 = jnp.where(lane_f == i1, -jnp.inf, lg)
        v2 = jnp.max(lg2, axis=-1, keepdims=True)
        i2 = jnp.min(jnp.where(lg2 == v2, lane_f, float(LANES)), axis=-1, keepdims=True)
        e = jnp.exp(v2 - v1)
        g1 = 1.0 / (1.0 + e)
        g2 = e / (1.0 + e)
        route_ref[rows, :] = jnp.where(lane == 0, i1, jnp.where(lane == 1, i2,
                                       jnp.where(lane == 2, g1, jnp.where(lane == 3, g2, 0.0))))


def _mixer_out_specs(row):
    return [
        pl.BlockSpec((ROW_TILE, WIDTH_A), row),
        pl.BlockSpec((ROW_TILE, WIDTH_B), row),
        _resident((WIDTH_A + WIDTH_B, D_MODEL)),
        pl.BlockSpec((ROW_TILE, D_MODEL), row),
        _resident((1, D_MODEL)),
    ]


def _out_proj_router_call(oa, ob, w_bf16, x2d, g, w_router_packed):
    n = x2d.shape[0]
    row = lambda i: (i, 0)
    return pl.pallas_call(
        _out_proj_router_kernel,
        grid=(n // ROW_TILE,),
        in_specs=_mixer_out_specs(row) + [_resident((D_MODEL, LANES))],
        out_specs=[
            pl.BlockSpec((ROW_TILE, D_MODEL), row),
            pl.BlockSpec((ROW_TILE, *ROW_SLAB), lambda i: (i, 0, 0)),
            pl.BlockSpec((ROW_TILE, LANES), row),
        ],
        out_shape=[
            jax.ShapeDtypeStruct((n, D_MODEL), F32),
            jax.ShapeDtypeStruct((n, *ROW_SLAB), F32),
            jax.ShapeDtypeStruct((n, LANES), F32),
        ],
        compiler_params=_cparams(("parallel",)),
        name="out_proj_router",
    )(oa, ob, w_bf16, x2d, g, w_router_packed)


def _swiglu_act(g, u):
    return (g / (1.0 + jnp.exp(-g))) * u


def _out_proj_ffn_kernel(oa_ref, ob_ref, w_ref, x_ref, g_ref, wg_ref, wu_ref, wd_ref, *rest, n_cast):
    cast_in, o_ref, cast_out, a_ref = rest[:n_cast], rest[n_cast], rest[n_cast + 1:-1], rest[-1]
    x1, hn = _mixer_out_residual_norm(oa_ref, ob_ref, w_ref, x_ref, g_ref)
    h = hn.astype(BF16)
    c0 = 0
    for width in FFN_CHUNKS:
        g = jnp.dot(h, wg_ref[:, c0:c0 + width], preferred_element_type=F32)
        u = jnp.dot(h, wu_ref[:, c0:c0 + width], preferred_element_type=F32)
        a_ref[:, c0:c0 + width] = _swiglu_act(g, u).astype(BF16)
        c0 += width
    o_ref[...] = x1 + jnp.dot(a_ref[...], wd_ref[...], preferred_element_type=F32)
    for src, dst in zip(cast_in, cast_out):
        dst[...] = src[...].astype(dst.dtype)


def _out_proj_ffn_call(oa, ob, w_bf16, x2d, g, wg, wu, wd, cast_along=()):
    n = x2d.shape[0]
    steps = n // ROW_TILE
    row = lambda i: (i, 0)
    cast_specs = [pl.BlockSpec((c.shape[0] // steps, c.shape[1]), row) for c in cast_along]
    assert all(c.shape[0] % (16 * steps) == 0 for c in cast_along)
    return pl.pallas_call(
        functools.partial(_out_proj_ffn_kernel, n_cast=len(cast_along)),
        grid=(steps,),
        in_specs=_mixer_out_specs(row) + [
            _resident((D_MODEL, D_FF)),
            _resident((D_MODEL, D_FF)),
            _resident((D_FF, D_MODEL)),
        ] + cast_specs,
        out_specs=[pl.BlockSpec((ROW_TILE, D_MODEL), row)] + cast_specs,
        out_shape=[jax.ShapeDtypeStruct((n, D_MODEL), F32)]
        + [jax.ShapeDtypeStruct(c.shape, BF16) for c in cast_along],
        scratch_shapes=[pltpu.VMEM((ROW_TILE, D_FF), BF16)],
        compiler_params=_cparams(("parallel",)),
        name="out_proj_dense_swiglu",
    )(oa, ob, w_bf16, x2d, g, wg, wu, wd, *cast_along)


ROW_SLAB = (8, LANES)
assert ROW_SLAB[0] * ROW_SLAB[1] == D_MODEL


def _to_slabs(x):
    r = x.shape[0]
    xs = jnp.stack([x[:, LANES * j:LANES * (j + 1)] for j in range(ROW_SLAB[0])], axis=0)
    xs = xs.reshape(ROW_SLAB[0], r // 8, 8, LANES)
    xs = jnp.swapaxes(jnp.swapaxes(xs, 0, 1), 1, 2)
    return xs.reshape(r, *ROW_SLAB)


def _from_slabs(t):
    r = t.shape[0]
    y = jnp.swapaxes(t.reshape(r // 8, 8, *ROW_SLAB), 1, 2)
    return jnp.concatenate([y[:, j].reshape(r, LANES) for j in range(ROW_SLAB[0])], axis=1)


def _row_copy(src_ref, src_row, dst_ref, dst_row, sem):
    return pltpu.make_async_copy(src_ref.at[src_row], dst_ref.at[dst_row], sem)


def _dispatch_kernel(fill_ref, dest_ref, h_ref, o_hbm, stage, zero_ref, sems, zero_sem):
    i = pl.program_id(0)
    last = pl.num_programs(0) - 1
    slot = i % 2

    def drain(s):
        def wait(t, c):
            for k in range(TOP_K):
                _row_copy(stage.at[s], t, o_hbm, 0, sems.at[s]).wait()
            return c
        lax.fori_loop(0, DISPATCH_TILE, wait, 0, unroll=4)

    @pl.when(i >= 2)
    def _():
        drain(slot)

    stage[slot] = h_ref[...]

    def start(t, c):
        for k in range(TOP_K):
            _row_copy(stage.at[slot], t, o_hbm, dest_ref[0, 0, TOP_K * t + k], sems.at[slot]).start(priority=k)
        return c

    lax.fori_loop(0, DISPATCH_TILE, start, 0, unroll=4)

    @pl.when(i == last)
    def _():
        drain(slot)

        @pl.when(last >= 1)
        def _():
            drain(1 - slot)

        zero_ref[...] = jnp.zeros_like(zero_ref)
        for e in range(N_EXPERTS):
            lo, hi = fill_ref[e], fill_ref[N_EXPERTS + e]
            lax.fori_loop(lo, hi, lambda r, c: (_row_copy(zero_ref, 0, o_hbm, r, zero_sem).start(), c)[1], 0)
            lax.fori_loop(lo, hi, lambda r, c: (_row_copy(zero_ref, 0, o_hbm, 0, zero_sem).wait(), c)[1], 0)


def _dispatch_call(fill, dest3d, h, n_rows):
    n = h.shape[0]
    grid_spec = pltpu.PrefetchScalarGridSpec(
        num_scalar_prefetch=1,
        grid=(n // DISPATCH_TILE,),
        in_specs=[
            pl.BlockSpec((1, 1, TOP_K * DISPATCH_TILE), lambda i, f: (i, 0, 0), memory_space=pltpu.SMEM),
            pl.BlockSpec((DISPATCH_TILE, *ROW_SLAB), lambda i, f: (i, 0, 0)),
        ],
        out_specs=pl.BlockSpec(memory_space=pl.ANY),
        scratch_shapes=[pltpu.VMEM((2, DISPATCH_TILE, *ROW_SLAB), h.dtype), pltpu.VMEM((1, *ROW_SLAB), h.dtype),
                        pltpu.SemaphoreType.DMA((2,)), pltpu.SemaphoreType.DMA(())],
    )
    return pl.pallas_call(
        _dispatch_kernel,
        grid_spec=grid_spec,
        out_shape=jax.ShapeDtypeStruct((n_rows, *ROW_SLAB), h.dtype),
        compiler_params=_cparams(("arbitrary",)),
        name="moe_dispatch",
    )(fill, dest3d, h)


def _moe_kernel(be_ref, nact_ref, x_ref, wg_ref, wu_ref, wd_ref, o_ref, a_ref):
    active = pl.program_id(0) < nact_ref[0]

    @pl.when(active)
    def _():
        xb = _from_slabs(x_ref[...]).astype(BF16)
        for c0 in range(0, D_FF_EXPERT, MOE_FF_TILE):
            cols = slice(c0, c0 + MOE_FF_TILE)
            g = jnp.dot(xb, wg_ref[0, :, cols], preferred_element_type=F32)
            u = jnp.dot(xb, wu_ref[0, :, cols], preferred_element_type=F32)
            a_ref[:, cols] = _swiglu_act(g, u).astype(BF16)
        o_ref[...] = _to_slabs(jnp.dot(a_ref[...], wd_ref[0], preferred_element_type=F32))

    @pl.when(jnp.logical_not(active))
    def _():
        o_ref[...] = jnp.zeros_like(o_ref)


def _moe_call(block_expert, n_active, x_pad, wg, wu, wd):
    n_rows = x_pad.shape[0]
    expert = lambda b, be, na: (be[b], 0, 0)
    grid_spec = pltpu.PrefetchScalarGridSpec(
        num_scalar_prefetch=2,
        grid=(n_rows // MOE_BLOCK,),
        in_specs=[
            pl.BlockSpec((MOE_BLOCK, *ROW_SLAB), lambda b, be, na: (b, 0, 0)),
            pl.BlockSpec((1, D_MODEL, D_FF_EXPERT), expert, pipeline_mode=pl.Buffered(1)),
            pl.BlockSpec((1, D_MODEL, D_FF_EXPERT), expert, pipeline_mode=pl.Buffered(1)),
            pl.BlockSpec((1, D_FF_EXPERT, D_MODEL), expert),
        ],
        out_specs=pl.BlockSpec((MOE_BLOCK, *ROW_SLAB), lambda b, be, na: (b, 0, 0)),
        scratch_shapes=[pltpu.VMEM((MOE_BLOCK, D_FF_EXPERT), BF16)],
    )
    return pl.pallas_call(
        _moe_kernel,
        grid_spec=grid_spec,
        out_shape=jax.ShapeDtypeStruct((n_rows, *ROW_SLAB), F32),
        compiler_params=_cparams(("arbitrary",)),
        name="moe_swiglu",
    )(block_expert, n_active, x_pad, wg, wu, wd)


def _combine_kernel(pos_ref, pos_next_ref, y_hbm, x_ref, route_ref, o_ref, buf, sems):
    i = pl.program_id(0)
    slot = i % 2

    def gather(idx_ref, into):
        def start(t, c):
            for k in range(TOP_K):
                _row_copy(y_hbm, idx_ref[0, 0, TOP_K * t + k], buf.at[into, k], t,
                          sems.at[into]).start(priority=k)
            return c
        lax.fori_loop(0, COMBINE_TILE, start, 0, unroll=4)

    @pl.when(i == 0)
    def _():
        gather(pos_ref, 0)

    @pl.when(i + 1 < pl.num_programs(0))
    def _():
        gather(pos_next_ref, 1 - slot)

    def wait(t, c):
        for k in range(TOP_K):
            _row_copy(y_hbm, 0, buf.at[slot, k], t, sems.at[slot]).wait()
        return c

    lax.fori_loop(0, COMBINE_TILE, wait, 0, unroll=4)
    route = route_ref[...]
    g0 = route[:, TOP_K:TOP_K + 1]
    g1 = route[:, TOP_K + 1:TOP_K + 2]
    o_ref[...] = x_ref[...] + g0 * _from_slabs(buf[slot, 0]) + g1 * _from_slabs(buf[slot, 1])


def _combine_call(pos3d, y_pad, x2d, route):
    n = x2d.shape[0]
    n_tiles = n // COMBINE_TILE
    row = lambda i: (i, 0)
    idx_block = (1, 1, TOP_K * COMBINE_TILE)
    return pl.pallas_call(
        _combine_kernel,
        grid=(n_tiles,),
        in_specs=[
            pl.BlockSpec(idx_block, lambda i: (i, 0, 0), memory_space=pltpu.SMEM),
            pl.BlockSpec(idx_block, lambda i: (jnp.minimum(i + 1, n_tiles - 1), 0, 0), memory_space=pltpu.SMEM),
            pl.BlockSpec(memory_space=pl.ANY),
            pl.BlockSpec((COMBINE_TILE, D_MODEL), row),
            pl.BlockSpec((COMBINE_TILE, LANES), row),
        ],
        out_specs=pl.BlockSpec((COMBINE_TILE, D_MODEL), row),
        out_shape=jax.ShapeDtypeStruct((n, D_MODEL), F32),
        scratch_shapes=[pltpu.VMEM((2, TOP_K, COMBINE_TILE, *ROW_SLAB), F32), pltpu.SemaphoreType.DMA((2,))],
        compiler_params=_cparams(("arbitrary",)),
        name="moe_combine",
    )(pos3d, pos3d, y_pad, x2d, route)


def _moe_layer(x1, hn, route, wg, wu, wd):
    n = x1.shape[0]
    e_flat = route[:, :TOP_K].astype(jnp.int32).reshape(-1)
    onehot = (e_flat[:, None] == jnp.arange(N_EXPERTS, dtype=jnp.int32)[None, :]).astype(jnp.int32)
    csum = jnp.cumsum(onehot, axis=0)
    counts = csum[-1]
    padded = (counts + MOE_BLOCK - 1) // MOE_BLOCK * MOE_BLOCK
    pad_ends = jnp.cumsum(padded)
    pad_starts = pad_ends - padded
    dest = jnp.sum((csum - onehot + pad_starts[None, :]) * onehot, axis=1)
    n_blocks = -(-(n * TOP_K) // MOE_BLOCK) + N_EXPERTS
    n_rows = n_blocks * MOE_BLOCK
    block_start = jnp.arange(n_blocks, dtype=jnp.int32) * MOE_BLOCK
    block_expert = jnp.minimum(
        jnp.sum((pad_ends[None, :] <= block_start[:, None]).astype(jnp.int32), axis=1), N_EXPERTS - 1)
    n_active = (pad_ends[-1:] // MOE_BLOCK).astype(jnp.int32)
    fill = jnp.concatenate([pad_starts + counts, pad_starts[1:], jnp.full((1,), n_rows, jnp.int32)])

    x_pad = _dispatch_call(fill.astype(jnp.int32), dest.reshape(n // DISPATCH_TILE, 1, TOP_K * DISPATCH_TILE),
                           hn, n_rows)
    y_pad = _moe_call(block_expert.astype(jnp.int32), n_active, x_pad, wg, wu, wd)
    pos3d = dest.reshape(n // COMBINE_TILE, 1, TOP_K * COMBINE_TILE)
    return _combine_call(pos3d, y_pad, x1, route)


def _head_group_matrix():
    idx = np.arange(MXU_DIM) // HEAD_DIM
    return jnp.asarray((idx[:, None] == idx[None, :]).astype(np.float32), dtype=BF16)


def _pack_router(w_router):
    hi = w_router.astype(BF16)
    lo = (w_router - hi.astype(F32)).astype(BF16)
    pad = jnp.zeros((D_MODEL, LANES - 2 * N_EXPERTS), BF16)
    return jnp.concatenate([hi, lo, pad], axis=1)


def kernel(x, norm_mix_g, norm_ffn_g, w_in, w_out, q_norm_a, k_norm_a, q_norm_b, k_norm_b,
           lambda_q1, lambda_k1, lambda_q2, lambda_k2, subln_g, rel_bias,
           w_gate_dense, w_up_dense, w_down_dense, w_router, w_gate_moe, w_up_moe, w_down_moe):
    b, s, d = x.shape
    assert (s, d) == (SEQ, D_MODEL)
    depth = w_in.shape[0]
    n = b * s
    x2d = x.reshape(n, d)
    gmat = _head_group_matrix()
    bias_a = _bias_tables(rel_bias, _dilated_buckets(), 0, N_HEADS_A)
    bias_b = _bias_tables(rel_bias, _diff_buckets(), N_HEADS_A, N_HEADS_B)
    far_b = rel_bias[_far_bucket(), N_HEADS_A:]
    ones = jnp.ones((PROJ_COL_TILE,), F32)
    rep = PROJ_COL_TILE // HEAD_DIM

    precast = {}

    def bf16_weight(name, w, idx):
        return precast[name, idx] if (name, idx) in precast else w[idx].astype(BF16)

    for layer in range(depth):
        gains = jnp.concatenate([
            jnp.tile(q_norm_a[layer], rep) * (ATTN_SCALE * LOG2E), jnp.tile(k_norm_a[layer], rep), ones,
            jnp.tile(q_norm_b[layer], rep) * (ATTN_SCALE * LOG2E), jnp.tile(k_norm_b[layer], rep), ones,
        ]).reshape(1, PROJ_WIDTH)
        proj = _proj_call(x2d, norm_mix_g[layer].reshape(1, d), bf16_weight("w_in", w_in, layer), gains, gmat)
        proj3d = proj.reshape(b, s, PROJ_WIDTH)

        oa = _mixer_a_call(proj3d, bias_a)
        lam_init = 0.8 - 0.6 * math.exp(-0.3 * layer)
        lam_rows = jnp.zeros((8, LANES), F32).at[:4, :HEAD_DIM].set(
            jnp.stack([lambda_q1[layer], lambda_k1[layer], lambda_q2[layer], lambda_k2[layer]]))
        ob = _mixer_b_call(proj3d, bias_b, far_b, lam_rows, subln_g[layer].reshape(1, LANES), lam_init)

        is_moe = layer % 2 == 1
        i = layer // 2
        mixed = (oa.reshape(n, WIDTH_A), ob.reshape(n, WIDTH_B), bf16_weight("w_out", w_out, layer), x2d,
                 norm_ffn_g[layer].reshape(1, d))
        if is_moe:
            x1, hn, route = _out_proj_router_call(*mixed, _pack_router(w_router[i]))
            x2d = _moe_layer(x1, hn, route, bf16_weight("w_gate_moe", w_gate_moe, i),
                             bf16_weight("w_up_moe", w_up_moe, i), bf16_weight("w_down_moe", w_down_moe, i))
        else:
            nxt = {}
            if layer + 1 < depth:
                nxt = {("w_in", layer + 1): w_in[layer + 1], ("w_out", layer + 1): w_out[layer + 1],
                       ("w_gate_moe", i): w_gate_moe[i], ("w_up_moe", i): w_up_moe[i],
                       ("w_down_moe", i): w_down_moe[i]}
            x2d, *cast = _out_proj_ffn_call(
                *mixed, w_gate_dense[i].astype(BF16), w_up_dense[i].astype(BF16), w_down_dense[i].astype(BF16),
                tuple(w.reshape(-1, w.shape[-1]) for w in nxt.values()))
            for (key, w), c in zip(nxt.items(), cast):
                precast[key] = c.reshape(w.shape)
    return x2d.reshape(b, s, d)
```
